```python
import jax, jax.numpy as jnp
from jax import lax
import numpy as np

D_MODEL = 2048
BATCH = 1
SEQ = 8192
DEPTH = 1

GRID_W = 64
MIX_WIDTH = D_MODEL
MLA_HEADS = 8
MLA_NOPE = 128
MLA_ROPE = 64
MLA_V = 128
MLA_Q_RANK = 512
MLA_KV_RANK = 256
ROPE_THETA = 10000.0
NA_HEADS = 8
NA_HEAD_DIM = 128
NA_ROWS_MAX = 8
NA_COLS = 16
D_FF = 4 * D_MODEL
Q_BLOCK = 128
NORM_EPS = 1e-6

MLA_WIDTH = MLA_HEADS * MLA_V
NA_WIDTH = NA_HEADS * NA_HEAD_DIM
MLA_QK = MLA_NOPE + MLA_ROPE
IN_COLS = MLA_Q_RANK + MLA_KV_RANK + MLA_ROPE + 3 * NA_WIDTH

kernel_name = "hybrid_mla_natten_sqrelu_block"


def rms_norm(x, g):
    xf = x.astype(jnp.float32)
    y = xf * lax.rsqrt(jnp.mean(xf * xf, axis=-1, keepdims=True) + NORM_EPS)
    return (y * g.astype(jnp.float32)).astype(x.dtype)


def apply_rope(x, pos):
    half = x.shape[-1] // 2
    inv = ROPE_THETA ** (-jnp.arange(half, dtype=jnp.float32) / half)
    ang = pos.astype(jnp.float32)[:, None] * inv[None, :]
    c = jnp.cos(ang)[None, :, None, :]
    s = jnp.sin(ang)[None, :, None, :]
    xf = x.astype(jnp.float32)
    x1, x2 = xf[..., :half], xf[..., half:]
    return jnp.concatenate([x1 * c - x2 * s, x1 * s + x2 * c], axis=-1).astype(x.dtype)


def mla_mixer(c_q, c_kv, k_rope, q_norm_g, w_uq, kv_norm_g, w_ukv):
    B, S, _ = c_q.shape
    H = MLA_HEADS
    pos = jnp.arange(S)
    q = (rms_norm(c_q, q_norm_g) @ w_uq).reshape(B, S, H, MLA_QK)
    q_nope, q_rope = q[..., :MLA_NOPE], apply_rope(q[..., MLA_NOPE:], pos)
    kv = (rms_norm(c_kv, kv_norm_g) @ w_ukv).reshape(B, S, H, MLA_NOPE + MLA_V)
    k_nope, v = kv[..., :MLA_NOPE], kv[..., MLA_NOPE:]
    k_r = apply_rope(k_rope[:, :, None, :], pos)[:, :, 0, :]
    scale = MLA_QK ** -0.5
    nb = S // Q_BLOCK
    qn_b = q_nope.reshape(B, nb, Q_BLOCK, H, MLA_NOPE).transpose(1, 0, 2, 3, 4)
    qr_b = q_rope.reshape(B, nb, Q_BLOCK, H, MLA_ROPE).transpose(1, 0, 2, 3, 4)

    def block(args):
        qn, qr = args
        s = (jnp.einsum('bqhd,bkhd->bhqk', qn, k_nope, preferred_element_type=jnp.float32)
             + jnp.einsum('bqhr,bkr->bhqk', qr, k_r, preferred_element_type=jnp.float32))
        p = jax.nn.softmax(s * scale, axis=-1).astype(v.dtype)
        return jnp.einsum('bhqk,bkhd->bqhd', p, v)

    o = lax.map(block, (qn_b, qr_b))
    return o.transpose(1, 0, 2, 3, 4).reshape(B, S, MLA_WIDTH)


def na_mixer(q, k, v, rpb):
    B, S, _ = q.shape
    W = GRID_W
    rows = S // W
    kr = min(NA_ROWS_MAX, rows)
    kc = NA_COLS
    H, D = NA_HEADS, NA_HEAD_DIM
    qg = q.reshape(B, rows, W, H, D).transpose(1, 0, 2, 3, 4)
    kg = k.reshape(B, rows, W, H, D)
    vg = v.reshape(B, rows, W, H, D)
    col = jnp.arange(W)
    col_start = jnp.clip(col - kc // 2, 0, W - kc)
    col_idx = col_start[:, None] + jnp.arange(kc)[None, :]
    dc = col_idx - col[:, None] + (NA_COLS - 1)
    scale = D ** -0.5

    def row_block(args):
        r, qr = args
        rs = jnp.clip(r - kr // 2, 0, rows - kr)
        kb = lax.dynamic_slice_in_dim(kg, rs, kr, axis=1)
        vb = lax.dynamic_slice_in_dim(vg, rs, kr, axis=1)
        kw = kb[:, :, col_idx]
        vw = vb[:, :, col_idx]
        dr = rs + jnp.arange(kr) - r + (NA_ROWS_MAX - 1)
        bias = rpb[:, dr[:, None, None], dc[None, :, :]]
        bias = bias.transpose(0, 2, 1, 3).astype(jnp.float32)
        s = jnp.einsum('bchd,bicjhd->bhcij', qr, kw, preferred_element_type=jnp.float32) * scale
        s = (s + bias[None]).reshape(B, H, W, kr * kc)
        p = jax.nn.softmax(s, axis=-1).astype(vw.dtype).reshape(B, H, W, kr, kc)
        return jnp.einsum('bhcij,bicjhd->bchd', p, vw)

    o = lax.map(row_block, (jnp.arange(rows), qg))
    return o.transpose(1, 0, 2, 3, 4).reshape(B, S, NA_WIDTH)


def setup_inputs(seed: int = 0) -> dict:
    key = jax.random.key(seed)
    ks = jax.random.split(key, 20)
    f32 = jnp.float32

    def nrm(k, shape, scale):
        return jax.random.normal(k, shape, f32) * scale

    def gain(k, n):
        return 1.0 + 0.02 * jax.random.normal(k, (DEPTH, n), f32)

    return {
        "x": jax.random.normal(ks[0], (BATCH, SEQ, D_MODEL), f32),
        "attn_norm_g": gain(ks[1], D_MODEL),
        "w_in": nrm(ks[2], (DEPTH, D_MODEL, IN_COLS), D_MODEL ** -0.5),
        "q_norm_g": gain(ks[3], MLA_Q_RANK),
        "w_uq": nrm(ks[4], (DEPTH, MLA_Q_RANK, MLA_HEADS * MLA_QK), MLA_Q_RANK ** -0.5),
        "kv_norm_g": gain(ks[5], MLA_KV_RANK),
        "w_ukv": nrm(ks[6], (DEPTH, MLA_KV_RANK, MLA_HEADS * (MLA_NOPE + MLA_V)), MLA_KV_RANK ** -0.5),
        "na_rpb": nrm(ks[7], (DEPTH, NA_HEADS, 2 * NA_ROWS_MAX - 1, 2 * NA_COLS - 1), 0.5),
        "mla_out_norm_g": gain(ks[8], MLA_WIDTH),
        "na_out_norm_g": gain(ks[9], NA_WIDTH),
        "w_out": nrm(ks[10], (DEPTH, MIX_WIDTH, D_MODEL), MIX_WIDTH ** -0.5),
        "mlp_norm_g": gain(ks[11], D_MODEL),
        "w_ff1": nrm(ks[12], (DEPTH, D_MODEL, D_FF), D_MODEL ** -0.5),
        "w_ff2": nrm(ks[13], (DEPTH, D_FF, D_MODEL), (1.5 * D_FF) ** -0.5),
        "final_norm_g": 1.0 + 0.02 * jax.random.normal(ks[14], (D_MODEL,), f32),
    }


def reference(x, attn_norm_g, w_in, q_norm_g, w_uq, kv_norm_g, w_ukv, na_rpb,
              mla_out_norm_g, na_out_norm_g, w_out, mlp_norm_g, w_ff1, w_ff2,
              final_norm_g):
    splits = np.cumsum([MLA_Q_RANK, MLA_KV_RANK, MLA_ROPE, NA_WIDTH, NA_WIDTH]).tolist()
    for l in range(DEPTH):
        h = rms_norm(x, attn_norm_g[l])
        z = h @ w_in[l]
        c_q, c_kv, k_rope, na_q, na_k, na_v = jnp.split(z, splits, axis=-1)
        a_out = mla_mixer(c_q, c_kv, k_rope, q_norm_g[l], w_uq[l], kv_norm_g[l], w_ukv[l])
        n_out = na_mixer(na_q, na_k, na_v, na_rpb[l])
        mixed = jnp.concatenate([rms_norm(a_out, mla_out_norm_g[l]),
                                 rms_norm(n_out, na_out_norm_g[l])], axis=-1)
        x = x + mixed @ w_out[l]
        h = rms_norm(x, mlp_norm_g[l])
        x = x + jnp.square(jax.nn.relu(h @ w_ff1[l])) @ w_ff2[l]
    return rms_norm(x, final_norm_g)
```

```python
import functools

import jax
import jax.numpy as jnp
from jax import lax
from jax.experimental import pallas as pl
from jax.experimental.pallas import tpu as pltpu

F32 = jnp.float32
BF16 = jnp.bfloat16

D_MODEL = 2048
SEQ = 8192
GRID_W = 64
GRID_ROWS = SEQ // GRID_W
HEADS = 8
MLA_NOPE = 128
MLA_ROPE = 64
MLA_V = 128
MLA_QK = MLA_NOPE + MLA_ROPE
MLA_Q_RANK = 512
MLA_KV_RANK = 256
ROPE_THETA = 10000.0
NA_HEAD_DIM = 128
NA_ROWS = 8
NA_COLS = 16
NA_WIDTH = HEADS * NA_HEAD_DIM
MLA_WIDTH = HEADS * MLA_V
D_FF = 4 * D_MODEL
NORM_EPS = 1e-6

LANES = 128
QK_PAD = 2 * LANES
NEG_BIG = -1e30

_C_CQ = 0
_C_CKV = _C_CQ + MLA_Q_RANK
_C_NQ = _C_CKV + MLA_KV_RANK
_C_NK = _C_NQ + NA_WIDTH
_C_NV = _C_NK + NA_WIDTH
_C_KR = _C_NV + NA_WIDTH
IN_COLS_PAD = _C_KR + 2 * MLA_ROPE

VMEM_LIMIT = 56 * 1024 * 1024


def _rms(xf, g):
    y = xf * lax.rsqrt(jnp.mean(xf * xf, axis=-1, keepdims=True) + NORM_EPS)
    return y * g


def _dot(a, b):
    return jnp.dot(a, b, preferred_element_type=F32)


def _dot_nt(a, b):
    return lax.dot_general(a, b, (((1,), (1,)), ((), ())), preferred_element_type=F32)


def _const_spec(shape):
    nd = len(shape)
    return pl.BlockSpec(shape, lambda *_: (0,) * nd, pipeline_mode=pl.Buffered(1))


def _rope_lanes(x, c, s, lo_half):
    partner = jnp.where(lo_half, pltpu.roll(x, LANES - MLA_ROPE // 2, 1), pltpu.roll(x, MLA_ROPE // 2, 1))
    return x * c + partner * s


def _proj_kernel(x_ref, g_ref, win_ref, qg_ref, wuq_ref, kvg_ref, wukv_ref, cos_ref, sin_ref,
                 q_ref, k_ref, v_ref, nq_ref, nk_ref, nv_ref):
    tm = x_ref.shape[0]
    h = _rms(x_ref[...], g_ref[...]).astype(BF16)

    def proj(lo, hi):
        return _dot(h, win_ref[:, lo:hi])

    nq_ref[...] = (proj(_C_NQ, _C_NK) * (NA_HEAD_DIM ** -0.5)).astype(BF16)
    nk_ref[...] = proj(_C_NK, _C_NV).astype(BF16)
    nv_ref[...] = proj(_C_NV, _C_KR).astype(BF16)

    cos = cos_ref[...]
    sin = sin_ref[...]
    lane = lax.broadcasted_iota(jnp.int32, (tm, LANES), 1)
    lo_half = (lane % MLA_ROPE) < (MLA_ROPE // 2)

    cqn = _rms(proj(_C_CQ, _C_CKV), qg_ref[...]).astype(BF16)
    q = _dot(cqn, wuq_ref[...]) * (MLA_QK ** -0.5)
    rope0 = HEADS * MLA_NOPE
    for pair in range(HEADS // 2):
        qr = _rope_lanes(q[:, rope0 + pair * LANES: rope0 + (pair + 1) * LANES], cos, sin, lo_half)
        qr = qr.astype(BF16)
        for hh in (2 * pair, 2 * pair + 1):
            q_ref[hh, :, 0:LANES] = q[:, hh * MLA_NOPE:(hh + 1) * MLA_NOPE].astype(BF16)
            q_ref[hh, :, LANES:QK_PAD] = qr

    ckvn = _rms(proj(_C_CKV, _C_NQ), kvg_ref[...]).astype(BF16)
    kv = _dot(ckvn, wukv_ref[...])
    kr = _rope_lanes(proj(_C_KR, IN_COLS_PAD), cos, sin, lo_half)
    kr_even = jnp.where(lane < MLA_ROPE, kr, 0.0).astype(BF16)
    kr_odd = jnp.where(lane >= MLA_ROPE, kr, 0.0).astype(BF16)
    v0 = HEADS * MLA_NOPE
    for hh in range(HEADS):
        k_ref[hh, :, 0:LANES] = kv[:, hh * MLA_NOPE:(hh + 1) * MLA_NOPE].astype(BF16)
        k_ref[hh, :, LANES:QK_PAD] = kr_even if hh % 2 == 0 else kr_odd
        v_ref[hh] = kv[:, v0 + hh * MLA_V: v0 + (hh + 1) * MLA_V].astype(BF16)


def _proj(x, g, win, qg, wuq, kvg, wukv, cos, sin, tm=256):
    row = lambda w: pl.BlockSpec((tm, w), lambda i: (i, 0))
    head = lambda w: pl.BlockSpec((HEADS, tm, w), lambda i: (0, i, 0))
    return pl.pallas_call(
        _proj_kernel,
        grid=(SEQ // tm,),
        in_specs=[row(D_MODEL), _const_spec((1, D_MODEL)), _const_spec((D_MODEL, IN_COLS_PAD)),
                  _const_spec((1, MLA_Q_RANK)), _const_spec((MLA_Q_RANK, HEADS * MLA_QK)),
                  _const_spec((1, MLA_KV_RANK)), _const_spec((MLA_KV_RANK, HEADS * (MLA_NOPE + MLA_V))),
                  row(LANES), row(LANES)],
        out_specs=[head(QK_PAD), head(QK_PAD), head(MLA_V), row(NA_WIDTH), row(NA_WIDTH), row(NA_WIDTH)],
        out_shape=[jax.ShapeDtypeStruct((HEADS, SEQ, QK_PAD), BF16),
                   jax.ShapeDtypeStruct((HEADS, SEQ, QK_PAD), BF16),
                   jax.ShapeDtypeStruct((HEADS, SEQ, MLA_V), BF16),
                   jax.ShapeDtypeStruct((SEQ, NA_WIDTH), BF16),
                   jax.ShapeDtypeStruct((SEQ, NA_WIDTH), BF16),
                   jax.ShapeDtypeStruct((SEQ, NA_WIDTH), BF16)],
        compiler_params=pltpu.CompilerParams(dimension_semantics=("arbitrary",),
                                             vmem_limit_bytes=VMEM_LIMIT),
        name="proj",
    )(x, g, win, qg, wuq, kvg, wukv, cos, sin)


def _mla_kernel(q_ref, k_ref, v_ref, o_ref, *, tk):
    tq = q_ref.shape[0]
    q = q_ref[...]

    def body(j, carry):
        m, l, acc = carry
        off = pl.multiple_of(j * tk, tk)
        s = _dot_nt(q, k_ref[pl.ds(off, tk), :])
        m_new = jnp.maximum(m, jnp.max(s, axis=-1, keepdims=True))
        alpha = jnp.exp(m - m_new)
        p = jnp.exp(s - m_new)
        l = alpha * l + jnp.sum(p, axis=-1, keepdims=True)
        acc = alpha * acc + _dot(p.astype(BF16), v_ref[pl.ds(off, tk), :])
        return m_new, l, acc

    m0 = jnp.full((tq, 1), -jnp.inf, F32)
    l0 = jnp.zeros((tq, 1), F32)
    acc0 = jnp.zeros((tq, MLA_V), F32)
    _, l, acc = lax.fori_loop(0, SEQ // tk, body, (m0, l0, acc0))
    o_ref[...] = acc / l


def _mla(q, k, v, tq=512, tk=512):
    return pl.pallas_call(
        functools.partial(_mla_kernel, tk=tk),
        grid=(HEADS, SEQ // tq),
        in_specs=[pl.BlockSpec((None, tq, QK_PAD), lambda h, i: (h, i, 0)),
                  pl.BlockSpec((None, SEQ, QK_PAD), lambda h, i: (h, 0, 0)),
                  pl.BlockSpec((None, SEQ, MLA_V), lambda h, i: (h, 0, 0))],
        out_specs=pl.BlockSpec((tq, MLA_V), lambda h, i: (i, h)),
        out_shape=jax.ShapeDtypeStruct((SEQ, MLA_WIDTH), F32),
        compiler_params=pltpu.CompilerParams(dimension_semantics=("arbitrary", "arbitrary"),
                                             vmem_limit_bytes=VMEM_LIMIT),
        name="mla",
    )(q, k, v)


NA_QROWS = 4
NA_KROWS = 12
NA_TQ = NA_QROWS * GRID_W
NA_TK = NA_KROWS * GRID_W
NA_STEPS = GRID_ROWS // NA_QROWS
NA_BIAS_ROWS = 2 * NA_ROWS - 1


def _na_bias_plan(variant):
    plan = {}
    for a in range(NA_QROWS):
        for b in range(NA_KROWS):
            if variant == "first":
                dr, ok = b - a + NA_ROWS - 1, b < NA_ROWS
            elif variant == "last":
                dr, ok = b - a - 1, b >= NA_KROWS - NA_ROWS
            else:
                dr, ok = b - a + NA_ROWS // 2 - 1, a <= b < a + NA_ROWS
            plan[(a, b)] = dr if ok else None
    return plan


def _na_kernel(q_ref, k0_ref, k1_ref, k2_ref, v0_ref, v1_ref, v2_ref, t_ref, o_ref, bias_ref):
    g = pl.program_id(0)

    def build(variant):
        plan = _na_bias_plan(variant)

        def per_head(h, carry):
            for (a, b), dr in plan.items():
                if dr is None:
                    tile = jnp.full((GRID_W, GRID_W), NEG_BIG, F32)
                else:
                    tile = t_ref[h, dr]
                bias_ref[h, a * GRID_W:(a + 1) * GRID_W, b * GRID_W:(b + 1) * GRID_W] = tile
            return carry

        lax.fori_loop(0, HEADS, per_head, 0)

    pl.when(g == 0)(lambda: build("first"))
    pl.when(g == 1)(lambda: build("mid"))
    pl.when(g == NA_STEPS - 1)(lambda: build("last"))

    k_refs = (k0_ref, k1_ref, k2_ref)
    v_refs = (v0_ref, v1_ref, v2_ref)
    for h in range(HEADS):
        cols = slice(h * NA_HEAD_DIM, (h + 1) * NA_HEAD_DIM)
        qh = q_ref[:, cols]
        s = jnp.concatenate([_dot_nt(qh, kr[:, cols]) for kr in k_refs], axis=1) + bias_ref[h]
        m = jnp.max(s, axis=-1, keepdims=True)
        p = jnp.exp(s - m)
        l = jnp.sum(p, axis=-1, keepdims=True)
        pb = p.astype(BF16)
        o = _dot(pb[:, 0:NA_TQ], v_refs[0][:, cols])
        for t in (1, 2):
            o = o + _dot(pb[:, t * NA_TQ:(t + 1) * NA_TQ], v_refs[t][:, cols])
        o_ref[:, cols] = o / l


def _na(nq, nk, nv, table):
    first_blk = lambda g: jnp.clip(g - 1, 0, NA_STEPS - 3)
    kv_specs = [pl.BlockSpec((NA_TQ, NA_WIDTH), functools.partial(lambda g, t: (first_blk(g) + t, 0), t=t))
                for t in range(3)]
    return pl.pallas_call(
        _na_kernel,
        grid=(NA_STEPS,),
        in_specs=[pl.BlockSpec((NA_TQ, NA_WIDTH), lambda g: (g, 0))] + kv_specs + kv_specs
                 + [_const_spec((HEADS, NA_BIAS_ROWS, GRID_W, GRID_W))],
        out_specs=pl.BlockSpec((NA_TQ, NA_WIDTH), lambda g: (g, 0)),
        out_shape=jax.ShapeDtypeStruct((SEQ, NA_WIDTH), F32),
        scratch_shapes=[pltpu.VMEM((HEADS, NA_TQ, NA_TK), F32)],
        compiler_params=pltpu.CompilerParams(dimension_semantics=("arbitrary",),
                                             vmem_limit_bytes=VMEM_LIMIT),
        name="na",
    )(nq, nk, nk, nk, nv, nv, nv, table)


def _outproj_kernel(a_ref, n_ref, ga_ref, gn_ref, w_ref, x_ref, o_ref):
    an = _rms(a_ref[...], ga_ref[...]).astype(BF16)
    nn = _rms(n_ref[...], gn_ref[...]).astype(BF16)
    o_ref[...] = x_ref[...] + _dot(an, w_ref[0:MLA_WIDTH, :]) + _dot(nn, w_ref[MLA_WIDTH:, :])


def _outproj(a, n, ga, gn, w, x, tm=512):
    row = lambda w_: pl.BlockSpec((tm, w_), lambda i: (i, 0))
    return pl.pallas_call(
        _outproj_kernel,
        grid=(SEQ // tm,),
        in_specs=[row(MLA_WIDTH), row(NA_WIDTH), _const_spec((1, MLA_WIDTH)), _const_spec((1, NA_WIDTH)),
                  _const_spec((MLA_WIDTH + NA_WIDTH, D_MODEL)), row(D_MODEL)],
        out_specs=row(D_MODEL),
        out_shape=jax.ShapeDtypeStruct((SEQ, D_MODEL), F32),
        compiler_params=pltpu.CompilerParams(dimension_semantics=("arbitrary",),
                                             vmem_limit_bytes=VMEM_LIMIT),
        name="outproj",
    )(a, n, ga, gn, w, x)


def _mlp_kernel(x_ref, g_ref, w1_ref, w2_ref, gf_ref, o_ref, h_ref, acc_ref):
    j = pl.program_id(1)

    @pl.when(j == 0)
    def _():
        x = x_ref[...]
        h_ref[...] = _rms(x, g_ref[...]).astype(BF16)
        acc_ref[...] = x

    hid = jnp.maximum(_dot(h_ref[...], w1_ref[...]), 0.0)
    acc_ref[...] += _dot((hid * hid).astype(BF16), w2_ref[...])

    @pl.when(j == pl.num_programs(1) - 1)
    def _():
        o_ref[...] = _rms(acc_ref[...], gf_ref[...])


def _mlp(x, g, w1, w2, gf, tm=512, tf=1024):
    return pl.pallas_call(
        _mlp_kernel,
        grid=(SEQ // tm, D_FF // tf),
        in_specs=[pl.BlockSpec((tm, D_MODEL), lambda i, j: (i, 0)),
                  _const_spec((1, D_MODEL)),
                  pl.BlockSpec((D_MODEL, tf), lambda i, j: (0, j)),
                  pl.BlockSpec((tf, D_MODEL), lambda i, j: (j, 0)),
                  _const_spec((1, D_MODEL))],
        out_specs=pl.BlockSpec((tm, D_MODEL), lambda i, j: (i, 0)),
        out_shape=jax.ShapeDtypeStruct((SEQ, D_MODEL), F32),
        scratch_shapes=[pltpu.VMEM((tm, D_MODEL), BF16), pltpu.VMEM((tm, D_MODEL), F32)],
        compiler_params=pltpu.CompilerParams(dimension_semantics=("arbitrary", "arbitrary"),
                                             vmem_limit_bytes=VMEM_LIMIT),
        name="mlp",
    )(x, g, w1, w2, gf)


def _rope_tables():
    half = MLA_ROPE // 2
    inv = ROPE_THETA ** (-jnp.arange(half, dtype=F32) / half)
    ang = jnp.arange(SEQ).astype(F32)[:, None] * inv[None, :]
    c, s = jnp.cos(ang), jnp.sin(ang)
    reps = LANES // MLA_ROPE
    return jnp.tile(jnp.concatenate([c, c], axis=1), (1, reps)), jnp.tile(jnp.concatenate([-s, s], axis=1), (1, reps))


def _na_bias_table(rpb):
    col = jnp.arange(GRID_W)
    start = jnp.clip(col - NA_COLS // 2, 0, GRID_W - NA_COLS)
    kc = col[None, :]
    inside = (kc >= start[:, None]) & (kc < start[:, None] + NA_COLS)
    dc = jnp.clip(kc - col[:, None] + NA_COLS - 1, 0, 2 * NA_COLS - 2)
    return jnp.where(inside[None, None], rpb[:, :, dc].astype(F32), NEG_BIG)


def kernel(x, attn_norm_g, w_in, q_norm_g, w_uq, kv_norm_g, w_ukv, na_rpb, mla_out_norm_g, na_out_norm_g,
           w_out, mlp_norm_g, w_ff1, w_ff2, final_norm_g):
    assert x.shape == (1, SEQ, D_MODEL)
    assert w_in.shape[0] == 1, "the final norm is fused into the (single) layer's MLP kernel"
    cos, sin = _rope_tables()
    xs = x[0]
    for l in range(1):
        wi = w_in[l]
        kr_cols = wi[:, MLA_Q_RANK + MLA_KV_RANK: MLA_Q_RANK + MLA_KV_RANK + MLA_ROPE]
        win = jnp.concatenate([wi[:, :MLA_Q_RANK + MLA_KV_RANK], wi[:, MLA_Q_RANK + MLA_KV_RANK + MLA_ROPE:],
                               kr_cols, kr_cols], axis=1).astype(BF16)
        wq = w_uq[l].reshape(MLA_Q_RANK, HEADS, MLA_QK)
        wuq = jnp.concatenate([wq[:, :, :MLA_NOPE].reshape(MLA_Q_RANK, HEADS * MLA_NOPE),
                               wq[:, :, MLA_NOPE:].reshape(MLA_Q_RANK, HEADS * MLA_ROPE)], axis=1).astype(BF16)
        wk = w_ukv[l].reshape(MLA_KV_RANK, HEADS, MLA_NOPE + MLA_V)
        wukv = jnp.concatenate([wk[:, :, :MLA_NOPE].reshape(MLA_KV_RANK, HEADS * MLA_NOPE),
                                wk[:, :, MLA_NOPE:].reshape(MLA_KV_RANK, HEADS * MLA_V)], axis=1).astype(BF16)

        q, k, v, nq, nk, nv = _proj(xs, attn_norm_g[l][None], win, q_norm_g[l][None], wuq,
                                    kv_norm_g[l][None], wukv, cos, sin)
        a_out = _mla(q, k, v)
        n_out = _na(nq, nk, nv, _na_bias_table(na_rpb[l]))
        xs = _outproj(a_out, n_out, mla_out_norm_g[l][None], na_out_norm_g[l][None],
                      w_out[l].astype(BF16), xs)
        xs = _mlp(xs, mlp_norm_g[l][None], w_ff1[l].astype(BF16), w_ff2[l].astype(BF16), final_norm_g[None])
    return xs[None]
```

```python
import functools

import jax
import jax.numpy as jnp
from jax import lax
from jax.experimental import pallas as pl
from jax.experimental.pallas import tpu as pltpu

F32 = jnp.float32
BF16 = jnp.bfloat16

D_MODEL = 2048
SEQ = 8192
GRID_W = 64
GRID_ROWS = SEQ // GRID_W
HEADS = 8
MLA_NOPE = 128
MLA_ROPE = 64
MLA_V = 128
MLA_QK = MLA_NOPE + MLA_ROPE
MLA_Q_RANK = 512
MLA_KV_RANK = 256
ROPE_THETA = 10000.0
NA_HEAD_DIM = 128
NA_ROWS = 8
NA_COLS = 16
NA_WIDTH = HEADS * NA_HEAD_DIM
MLA_WIDTH = HEADS * MLA_V
D_FF = 4 * D_MODEL
NORM_EPS = 1e-6

LANES = 128
QK_PAD = 2 * LANES
V_PAD = 2 * LANES
LOG2_E = 1.4426950408889634
NEG_BIG = -1e30

_C_CQ = 0
_C_CKV = _C_CQ + MLA_Q_RANK
_C_NQ = _C_CKV + MLA_KV_RANK
_C_NK = _C_NQ + NA_WIDTH
_C_NV = _C_NK + NA_WIDTH
_C_KR = _C_NV + NA_WIDTH
IN_COLS_PAD = _C_KR + 2 * MLA_ROPE

VMEM_LIMIT = 56 * 1024 * 1024


def _rms(xf, g):
    y = xf * lax.rsqrt(jnp.mean(xf * xf, axis=-1, keepdims=True) + NORM_EPS)
    return y * g


def _dot(a, b):
    return jnp.dot(a, b, preferred_element_type=F32)


def _dot_nt(a, b):
    return lax.dot_general(a, b, (((1,), (1,)), ((), ())), preferred_element_type=F32)


def _const_spec(shape):
    nd = len(shape)
    return pl.BlockSpec(shape, lambda *_: (0,) * nd, pipeline_mode=pl.Buffered(1))


def _rope_lanes(x, c, s, lo_half):
    partner = jnp.where(lo_half, pltpu.roll(x, LANES - MLA_ROPE // 2, 1), pltpu.roll(x, MLA_ROPE // 2, 1))
    return x * c + partner * s


def _proj_kernel(x_ref, g_ref, win_ref, qg_ref, wuq_ref, kvg_ref, wukv_ref, cos_ref, sin_ref,
                 q_ref, k_ref, v_ref, nq_ref, nk_ref, nv_ref):
    tm = x_ref.shape[0]
    h = _rms(x_ref[...], g_ref[...]).astype(BF16)

    def proj(lo, hi):
        return _dot(h, win_ref[:, lo:hi])

    nq_ref[...] = (proj(_C_NQ, _C_NK) * (NA_HEAD_DIM ** -0.5)).astype(BF16)
    nk_ref[...] = proj(_C_NK, _C_NV).astype(BF16)
    nv_ref[...] = proj(_C_NV, _C_KR).astype(BF16)

    cos = cos_ref[...]
    sin = sin_ref[...]
    lane = lax.broadcasted_iota(jnp.int32, (tm, LANES), 1)
    lo_half = (lane % MLA_ROPE) < (MLA_ROPE // 2)

    cqn = _rms(proj(_C_CQ, _C_CKV), qg_ref[...]).astype(BF16)
    q = _dot(cqn, wuq_ref[...]) * (LOG2_E * MLA_QK ** -0.5)
    rope0 = HEADS * MLA_NOPE
    for pair in range(HEADS // 2):
        qr = _rope_lanes(q[:, rope0 + pair * LANES: rope0 + (pair + 1) * LANES], cos, sin, lo_half)
        qr = qr.astype(BF16)
        for hh in (2 * pair, 2 * pair + 1):
            q_ref[hh, :, 0:LANES] = q[:, hh * MLA_NOPE:(hh + 1) * MLA_NOPE].astype(BF16)
            q_ref[hh, :, LANES:QK_PAD] = qr

    ckvn = _rms(proj(_C_CKV, _C_NQ), kvg_ref[...]).astype(BF16)
    kv = _dot(ckvn, wukv_ref[...])
    kr = _rope_lanes(proj(_C_KR, IN_COLS_PAD), cos, sin, lo_half)
    kr_even = jnp.where(lane < MLA_ROPE, kr, 0.0).astype(BF16)
    kr_odd = jnp.where(lane >= MLA_ROPE, kr, 0.0).astype(BF16)
    v0 = HEADS * MLA_NOPE
    ones_col = jnp.where(lane == 0, 1.0, 0.0).astype(BF16)
    for hh in range(HEADS):
        k_ref[hh, :, 0:LANES] = kv[:, hh * MLA_NOPE:(hh + 1) * MLA_NOPE].astype(BF16)
        k_ref[hh, :, LANES:QK_PAD] = kr_even if hh % 2 == 0 else kr_odd
        v_ref[hh, :, 0:MLA_V] = kv[:, v0 + hh * MLA_V: v0 + (hh + 1) * MLA_V].astype(BF16)
        v_ref[hh, :, MLA_V:V_PAD] = ones_col


def _proj(x, g, win, qg, wuq, kvg, wukv, cos, sin, tm=256):
    row = lambda w: pl.BlockSpec((tm, w), lambda i: (i, 0))
    head = lambda w: pl.BlockSpec((HEADS, tm, w), lambda i: (0, i, 0))
    return pl.pallas_call(
        _proj_kernel,
        grid=(SEQ // tm,),
        in_specs=[row(D_MODEL), _const_spec((1, D_MODEL)), _const_spec((D_MODEL, IN_COLS_PAD)),
                  _const_spec((1, MLA_Q_RANK)), _const_spec((MLA_Q_RANK, HEADS * MLA_QK)),
                  _const_spec((1, MLA_KV_RANK)), _const_spec((MLA_KV_RANK, HEADS * (MLA_NOPE + MLA_V))),
                  row(LANES), row(LANES)],
        out_specs=[head(QK_PAD), head(QK_PAD), head(V_PAD), row(NA_WIDTH), row(NA_WIDTH), row(NA_WIDTH)],
        out_shape=[jax.ShapeDtypeStruct((HEADS, SEQ, QK_PAD), BF16),
                   jax.ShapeDtypeStruct((HEADS, SEQ, QK_PAD), BF16),
                   jax.ShapeDtypeStruct((HEADS, SEQ, V_PAD), BF16),
                   jax.ShapeDtypeStruct((SEQ, NA_WIDTH), BF16),
                   jax.ShapeDtypeStruct((SEQ, NA_WIDTH), BF16),
                   jax.ShapeDtypeStruct((SEQ, NA_WIDTH), BF16)],
        compiler_params=pltpu.CompilerParams(dimension_semantics=("arbitrary",),
                                             vmem_limit_bytes=VMEM_LIMIT),
        name="proj",
    )(x, g, win, qg, wuq, kvg, wukv, cos, sin)


MLA_STRIP = 64


def _mla_kernel(q_ref, k_ref, v_ref, o_ref, s0_ref, s1_ref, p0_ref, p1_ref, a0_ref, a1_ref, acc_ref, m_ref,
                *, tk):
    tq = q_ref.shape[0]
    nk = SEQ // tk
    s_refs, p_refs, a_refs = (s0_ref, s1_ref), (p0_ref, p1_ref), (a0_ref, a1_ref)

    def chunk(c):
        return pl.ds(pl.multiple_of(c * tk, tk), tk)

    def scores(c, slot):
        s_refs[slot][...] = _dot_nt(q_ref[...], k_ref[chunk(c), :])

    def softmax(slot):
        for r in range(tq // MLA_STRIP):
            rows = slice(r * MLA_STRIP, (r + 1) * MLA_STRIP)
            tiles = [s_refs[slot][rows, c * LANES:(c + 1) * LANES] for c in range(tk // LANES)]
            tile_max = functools.reduce(jnp.maximum, tiles)
            m_old = m_ref[rows, :]
            m_new = jnp.maximum(m_old, jnp.max(tile_max, axis=-1, keepdims=True))
            m_ref[rows, :] = m_new
            a_refs[slot][rows, :] = jnp.exp2(m_old - m_new)
            for c, s in enumerate(tiles):
                p_refs[slot][rows, c * LANES:(c + 1) * LANES] = jnp.exp2((s - m_new).astype(BF16))

    def accumulate(c, slot):
        pv = _dot(p_refs[slot][...], v_ref[chunk(c), :])
        alpha = a_refs[slot][...]
        for c in range(V_PAD // LANES):
            cols = slice(c * LANES, (c + 1) * LANES)
            acc_ref[:, cols] = alpha * acc_ref[:, cols] + pv[:, cols]

    m_ref[...] = jnp.full(m_ref.shape, -jnp.inf, F32)
    acc_ref[...] = jnp.zeros(acc_ref.shape, F32)
    scores(0, 0)
    scores(1, 1)
    softmax(0)

    def body(tt, carry):
        for slot in (0, 1):
            t = 2 * tt + 2 + slot
            softmax(1 - slot)
            scores(t, slot)
            accumulate(t - 2, slot)
        return carry

    lax.fori_loop(0, (nk - 2) // 2, body, 0)
    softmax(1)
    accumulate(nk - 2, 0)
    accumulate(nk - 1, 1)
    acc = acc_ref[...]
    o_ref[...] = acc[:, :MLA_V] / acc[:, MLA_V:MLA_V + 1]


def _mla(q, k, v, tq=512, tk=512):
    return pl.pallas_call(
        functools.partial(_mla_kernel, tk=tk),
        grid=(HEADS, SEQ // tq),
        in_specs=[pl.BlockSpec((None, tq, QK_PAD), lambda h, i: (h, i, 0)),
                  pl.BlockSpec((None, SEQ, QK_PAD), lambda h, i: (h, 0, 0)),
                  pl.BlockSpec((None, SEQ, V_PAD), lambda h, i: (h, 0, 0))],
        out_specs=pl.BlockSpec((tq, MLA_V), lambda h, i: (i, h)),
        out_shape=jax.ShapeDtypeStruct((SEQ, MLA_WIDTH), F32),
        scratch_shapes=[pltpu.VMEM((tq, tk), F32), pltpu.VMEM((tq, tk), F32),
                        pltpu.VMEM((tq, tk), BF16), pltpu.VMEM((tq, tk), BF16),
                        pltpu.VMEM((tq, LANES), F32), pltpu.VMEM((tq, LANES), F32),
                        pltpu.VMEM((tq, V_PAD), F32), pltpu.VMEM((tq, LANES), F32)],
        compiler_params=pltpu.CompilerParams(dimension_semantics=("arbitrary", "arbitrary"),
                                             vmem_limit_bytes=VMEM_LIMIT),
        name="mla",
    )(q, k, v)


NA_QROWS = 4
NA_KROWS = 12
NA_TQ = NA_QROWS * GRID_W
NA_TK = NA_KROWS * GRID_W
NA_STEPS = GRID_ROWS // NA_QROWS
NA_BIAS_ROWS = 2 * NA_ROWS - 1


def _na_bias_plan(variant):
    plan = {}
    for a in range(NA_QROWS):
        for b in range(NA_KROWS):
            if variant == "first":
                dr, ok = b - a + NA_ROWS - 1, b < NA_ROWS
            elif variant == "last":
                dr, ok = b - a - 1, b >= NA_KROWS - NA_ROWS
            else:
                dr, ok = b - a + NA_ROWS // 2 - 1, a <= b < a + NA_ROWS
            plan[(a, b)] = dr if ok else None
    return plan


def _na_kernel(q_ref, k0_ref, k1_ref, k2_ref, v0_ref, v1_ref, v2_ref, t_ref, o_ref, bias_ref):
    g = pl.program_id(0)

    def build(variant):
        plan = _na_bias_plan(variant)

        def per_head(h, carry):
            for (a, b), dr in plan.items():
                if dr is None:
                    tile = jnp.full((GRID_W, GRID_W), NEG_BIG, F32)
                else:
                    tile = t_ref[h, dr]
                bias_ref[h, a * GRID_W:(a + 1) * GRID_W, b * GRID_W:(b + 1) * GRID_W] = tile
            return carry

        lax.fori_loop(0, HEADS, per_head, 0)

    pl.when(g == 0)(lambda: build("first"))
    pl.when(g == 1)(lambda: build("mid"))
    pl.when(g == NA_STEPS - 1)(lambda: build("last"))

    k_refs = (k0_ref, k1_ref, k2_ref)
    v_refs = (v0_ref, v1_ref, v2_ref)
    for h in range(HEADS):
        cols = slice(h * NA_HEAD_DIM, (h + 1) * NA_HEAD_DIM)
        qh = q_ref[:, cols]
        s = jnp.concatenate([_dot_nt(qh, kr[:, cols]) for kr in k_refs], axis=1) + bias_ref[h]
        m = jnp.max(s, axis=-1, keepdims=True)
        p = jnp.exp(s - m)
        l = jnp.sum(p, axis=-1, keepdims=True)
        pb = p.astype(BF16)
        o = _dot(pb[:, 0:NA_TQ], v_refs[0][:, cols])
        for t in (1, 2):
            o = o + _dot(pb[:, t * NA_TQ:(t + 1) * NA_TQ], v_refs[t][:, cols])
        o_ref[:, cols] = o / l


def _na(nq, nk, nv, table):
    first_blk = lambda g: jnp.clip(g - 1, 0, NA_STEPS - 3)
    kv_specs = [pl.BlockSpec((NA_TQ, NA_WIDTH), functools.partial(lambda g, t: (first_blk(g) + t, 0), t=t))
                for t in range(3)]
    return pl.pallas_call(
        _na_kernel,
        grid=(NA_STEPS,),
        in_specs=[pl.BlockSpec((NA_TQ, NA_WIDTH), lambda g: (g, 0))] + kv_specs + kv_specs
                 + [_const_spec((HEADS, NA_BIAS_ROWS, GRID_W, GRID_W))],
        out_specs=pl.BlockSpec((NA_TQ, NA_WIDTH), lambda g: (g, 0)),
        out_shape=jax.ShapeDtypeStruct((SEQ, NA_WIDTH), F32),
        scratch_shapes=[pltpu.VMEM((HEADS, NA_TQ, NA_TK), F32)],
        compiler_params=pltpu.CompilerParams(dimension_semantics=("arbitrary",),
                                             vmem_limit_bytes=VMEM_LIMIT),
        name="na",
    )(nq, nk, nk, nk, nv, nv, nv, table)


def _outproj_kernel(a_ref, n_ref, ga_ref, gn_ref, w_ref, x_ref, o_ref):
    an = _rms(a_ref[...], ga_ref[...]).astype(BF16)
    nn = _rms(n_ref[...], gn_ref[...]).astype(BF16)
    o_ref[...] = x_ref[...] + _dot(an, w_ref[0:MLA_WIDTH, :]) + _dot(nn, w_ref[MLA_WIDTH:, :])


def _outproj(a, n, ga, gn, w, x, tm=512):
    row = lambda w_: pl.BlockSpec((tm, w_), lambda i: (i, 0))
    return pl.pallas_call(
        _outproj_kernel,
        grid=(SEQ // tm,),
        in_specs=[row(MLA_WIDTH), row(NA_WIDTH), _const_spec((1, MLA_WIDTH)), _const_spec((1, NA_WIDTH)),
                  _const_spec((MLA_WIDTH + NA_WIDTH, D_MODEL)), row(D_MODEL)],
        out_specs=row(D_MODEL),
        out_shape=jax.ShapeDtypeStruct((SEQ, D_MODEL), F32),
        compiler_params=pltpu.CompilerParams(dimension_semantics=("arbitrary",),
                                             vmem_limit_bytes=VMEM_LIMIT),
        name="outproj",
    )(a, n, ga, gn, w, x)


def _mlp_kernel(x_ref, g_ref, w1_ref, w2_ref, gf_ref, o_ref, h_ref, acc_ref):
    j = pl.program_id(1)

    @pl.when(j == 0)
    def _():
        x = x_ref[...]
        h_ref[...] = _rms(x, g_ref[...]).astype(BF16)
        acc_ref[...] = x

    hid = jnp.maximum(_dot(h_ref[...], w1_ref[...]), 0.0)
    acc_ref[...] += _dot((hid * hid).astype(BF16), w2_ref[...])

    @pl.when(j == pl.num_programs(1) - 1)
    def _():
        o_ref[...] = _rms(acc_ref[...], gf_ref[...])


def _mlp(x, g, w1, w2, gf, tm=512, tf=1024):
    return pl.pallas_call(
        _mlp_kernel,
        grid=(SEQ // tm, D_FF // tf),
        in_specs=[pl.BlockSpec((tm, D_MODEL), lambda i, j: (i, 0)),
                  _const_spec((1, D_MODEL)),
                  pl.BlockSpec((D_MODEL, tf), lambda i, j: (0, j)),
                  pl.BlockSpec((tf, D_MODEL), lambda i, j: (j, 0)),
                  _const_spec((1, D_MODEL))],
        out_specs=pl.BlockSpec((tm, D_MODEL), lambda i, j: (i, 0)),
        out_shape=jax.ShapeDtypeStruct((SEQ, D_MODEL), F32),
        scratch_shapes=[pltpu.VMEM((tm, D_MODEL), BF16), pltpu.VMEM((tm, D_MODEL), F32)],
        compiler_params=pltpu.CompilerParams(dimension_semantics=("arbitrary", "arbitrary"),
                                             vmem_limit_bytes=VMEM_LIMIT),
        name="mlp",
    )(x, g, w1, w2, gf)


def _rope_tables():
    half = MLA_ROPE // 2
    inv = ROPE_THETA ** (-jnp.arange(half, dtype=F32) / half)
    ang = jnp.arange(SEQ).astype(F32)[:, None] * inv[None, :]
    c, s = jnp.cos(ang), jnp.sin(ang)
    reps = LANES // MLA_ROPE
    return jnp.tile(jnp.concatenate([c, c], axis=1), (1, reps)), jnp.tile(jnp.concatenate([-s, s], axis=1), (1, reps))


def _na_bias_table(rpb):
    col = jnp.arange(GRID_W)
    start = jnp.clip(col - NA_COLS // 2, 0, GRID_W - NA_COLS)
    kc = col[None, :]
    inside = (kc >= start[:, None]) & (kc < start[:, None] + NA_COLS)
    dc = jnp.clip(kc - col[:, None] + NA_COLS - 1, 0, 2 * NA_COLS - 2)
    return jnp.where(inside[None, None], rpb[:, :, dc].astype(F32), NEG_BIG)


def kernel(x, attn_norm_g, w_in, q_norm_g, w_uq, kv_norm_g, w_ukv, na_rpb, mla_out_norm_g, na_out_norm_g,
           w_out, mlp_norm_g, w_ff1, w_ff2, final_norm_g):
    assert x.shape == (1, SEQ, D_MODEL)
    assert w_in.shape[0] == 1, "the final norm is fused into the (single) layer's MLP kernel"
    cos, sin = _rope_tables()
    xs = x[0]
    for l in range(1):
        wi = w_in[l]
        kr_cols = wi[:, MLA_Q_RANK + MLA_KV_RANK: MLA_Q_RANK + MLA_KV_RANK + MLA_ROPE]
        win = jnp.concatenate([wi[:, :MLA_Q_RANK + MLA_KV_RANK], wi[:, MLA_Q_RANK + MLA_KV_RANK + MLA_ROPE:],
                               kr_cols, kr_cols], axis=1).astype(BF16)
        wq = w_uq[l].reshape(MLA_Q_RANK, HEADS, MLA_QK)
        wuq = jnp.concatenate([wq[:, :, :MLA_NOPE].reshape(MLA_Q_RANK, HEADS * MLA_NOPE),
                               wq[:, :, MLA_NOPE:].reshape(MLA_Q_RANK, HEADS * MLA_ROPE)], axis=1).astype(BF16)
        wk = w_ukv[l].reshape(MLA_KV_RANK, HEADS, MLA_NOPE + MLA_V)
        wukv = jnp.concatenate([wk[:, :, :MLA_NOPE].reshape(MLA_KV_RANK, HEADS * MLA_NOPE),
                                wk[:, :, MLA_NOPE:].reshape(MLA_KV_RANK, HEADS * MLA_V)], axis=1).astype(BF16)

        q, k, v, nq, nk, nv = _proj(xs, attn_norm_g[l][None], win, q_norm_g[l][None], wuq,
                                    kv_norm_g[l][None], wukv, cos, sin)
        a_out = _mla(q, k, v)
        n_out = _na(nq, nk, nv, _na_bias_table(na_rpb[l]))
        xs = _outproj(a_out, n_out, mla_out_norm_g[l][None], na_out_norm_g[l][None],
                      w_out[l].astype(BF16), xs)
        xs = _mlp(xs, mlp_norm_g[l][None], w_ff1[l].astype(BF16), w_ff2[l].astype(BF16), final_norm_g[None])
    return xs[None]
```

```python
import functools

import jax
import jax.numpy as jnp
from jax import lax
from jax.experimental import pallas as pl
from jax.experimental.pallas import tpu as pltpu

F32 = jnp.float32
BF16 = jnp.bfloat16

D_MODEL = 2048
SEQ = 8192
GRID_W = 64
GRID_ROWS = SEQ // GRID_W
HEADS = 8
MLA_NOPE = 128
MLA_ROPE = 64
MLA_V = 128
MLA_QK = MLA_NOPE + MLA_ROPE
MLA_Q_RANK = 512
MLA_KV_RANK = 256
ROPE_THETA = 10000.0
NA_HEAD_DIM = 128
NA_ROWS = 8
NA_COLS = 16
NA_WIDTH = HEADS * NA_HEAD_DIM
MLA_WIDTH = HEADS * MLA_V
D_FF = 4 * D_MODEL
NORM_EPS = 1e-6

LANES = 128
QK_PAD = 2 * LANES
V_PAD = 2 * LANES
LOG2_E = 1.4426950408889634
NEG_BIG = -1e30

_C_CQ = 0
_C_CKV = _C_CQ + MLA_Q_RANK
_C_NQ = _C_CKV + MLA_KV_RANK
_C_NK = _C_NQ + NA_WIDTH
_C_NV = _C_NK + NA_WIDTH
_C_KR = _C_NV + NA_WIDTH
IN_COLS_PAD = _C_KR + 2 * MLA_ROPE

VMEM_LIMIT = 56 * 1024 * 1024


def _rms(xf, g):
    y = xf * lax.rsqrt(jnp.mean(xf * xf, axis=-1, keepdims=True) + NORM_EPS)
    return y * g


def _dot(a, b):
    return jnp.dot(a, b, preferred_element_type=F32)


def _dot_nt(a, b):
    return lax.dot_general(a, b, (((1,), (1,)), ((), ())), preferred_element_type=F32)


def _const_spec(shape):
    nd = len(shape)
    return pl.BlockSpec(shape, lambda *_: (0,) * nd, pipeline_mode=pl.Buffered(1))


def _rope_lanes(x, c, s, lo_half):
    partner = jnp.where(lo_half, pltpu.roll(x, LANES - MLA_ROPE // 2, 1), pltpu.roll(x, MLA_ROPE // 2, 1))
    return x * c + partner * s


def _proj_kernel(x_ref, g_ref, win_ref, qg_ref, wuq_ref, kvg_ref, wukv_ref, cos_ref, sin_ref,
                 q_ref, k_ref, v_ref, nq_ref, nk_ref, nv_ref):
    tm = x_ref.shape[0]
    h = _rms(x_ref[...], g_ref[...]).astype(BF16)

    def proj(lo, hi):
        return _dot(h, win_ref[:, lo:hi])

    nq_ref[...] = (proj(_C_NQ, _C_NK) * (LOG2_E * NA_HEAD_DIM ** -0.5)).astype(BF16)
    nk_ref[...] = proj(_C_NK, _C_NV).astype(BF16)
    nv_ref[...] = proj(_C_NV, _C_KR).astype(BF16)

    cos = cos_ref[...]
    sin = sin_ref[...]
    lane = lax.broadcasted_iota(jnp.int32, (tm, LANES), 1)
    lo_half = (lane % MLA_ROPE) < (MLA_ROPE // 2)

    cqn = _rms(proj(_C_CQ, _C_CKV), qg_ref[...]).astype(BF16)
    q = _dot(cqn, wuq_ref[...]) * (LOG2_E * MLA_QK ** -0.5)
    rope0 = HEADS * MLA_NOPE
    for pair in range(HEADS // 2):
        qr = _rope_lanes(q[:, rope0 + pair * LANES: rope0 + (pair + 1) * LANES], cos, sin, lo_half)
        qr = qr.astype(BF16)
        for hh in (2 * pair, 2 * pair + 1):
            q_ref[hh, :, 0:LANES] = q[:, hh * MLA_NOPE:(hh + 1) * MLA_NOPE].astype(BF16)
            q_ref[hh, :, LANES:QK_PAD] = qr

    ckvn = _rms(proj(_C_CKV, _C_NQ), kvg_ref[...]).astype(BF16)
    kv = _dot(ckvn, wukv_ref[...])
    kr = _rope_lanes(proj(_C_KR, IN_COLS_PAD), cos, sin, lo_half)
    kr_even = jnp.where(lane < MLA_ROPE, kr, 0.0).astype(BF16)
    kr_odd = jnp.where(lane >= MLA_ROPE, kr, 0.0).astype(BF16)
    v0 = HEADS * MLA_NOPE
    ones_col = jnp.where(lane == 0, 1.0, 0.0).astype(BF16)
    for hh in range(HEADS):
        k_ref[hh, :, 0:LANES] = kv[:, hh * MLA_NOPE:(hh + 1) * MLA_NOPE].astype(BF16)
        k_ref[hh, :, LANES:QK_PAD] = kr_even if hh % 2 == 0 else kr_odd
        v_ref[hh, :, 0:MLA_V] = kv[:, v0 + hh * MLA_V: v0 + (hh + 1) * MLA_V].astype(BF16)
        v_ref[hh, :, MLA_V:V_PAD] = ones_col


def _proj(x, g, win, qg, wuq, kvg, wukv, cos, sin, tm=256):
    row = lambda w: pl.BlockSpec((tm, w), lambda i: (i, 0))
    head = lambda w: pl.BlockSpec((HEADS, tm, w), lambda i: (0, i, 0))
    return pl.pallas_call(
        _proj_kernel,
        grid=(SEQ // tm,),
        in_specs=[row(D_MODEL), _const_spec((1, D_MODEL)), _const_spec((D_MODEL, IN_COLS_PAD)),
                  _const_spec((1, MLA_Q_RANK)), _const_spec((MLA_Q_RANK, HEADS * MLA_QK)),
                  _const_spec((1, MLA_KV_RANK)), _const_spec((MLA_KV_RANK, HEADS * (MLA_NOPE + MLA_V))),
                  row(LANES), row(LANES)],
        out_specs=[head(QK_PAD), head(QK_PAD), head(V_PAD), row(NA_WIDTH), row(NA_WIDTH), row(NA_WIDTH)],
        out_shape=[jax.ShapeDtypeStruct((HEADS, SEQ, QK_PAD), BF16),
                   jax.ShapeDtypeStruct((HEADS, SEQ, QK_PAD), BF16),
                   jax.ShapeDtypeStruct((HEADS, SEQ, V_PAD), BF16),
                   jax.ShapeDtypeStruct((SEQ, NA_WIDTH), BF16),
                   jax.ShapeDtypeStruct((SEQ, NA_WIDTH), BF16),
                   jax.ShapeDtypeStruct((SEQ, NA_WIDTH), BF16)],
        compiler_params=pltpu.CompilerParams(dimension_semantics=("arbitrary",),
                                             vmem_limit_bytes=VMEM_LIMIT),
        name="proj",
    )(x, g, win, qg, wuq, kvg, wukv, cos, sin)


MLA_STRIP = 64


def _mla_kernel(q_ref, k_ref, v_ref, o_ref, s0_ref, s1_ref, p0_ref, p1_ref, a0_ref, a1_ref, acc_ref, m_ref,
                *, tk):
    tq = q_ref.shape[0]
    nk = SEQ // tk
    s_refs, p_refs, a_refs = (s0_ref, s1_ref), (p0_ref, p1_ref), (a0_ref, a1_ref)

    def chunk(c):
        return pl.ds(pl.multiple_of(c * tk, tk), tk)

    def scores(c, slot):
        s_refs[slot][...] = _dot_nt(q_ref[...], k_ref[chunk(c), :])

    def softmax(slot):
        for r in range(tq // MLA_STRIP):
            rows = slice(r * MLA_STRIP, (r + 1) * MLA_STRIP)
            tiles = [s_refs[slot][rows, c * LANES:(c + 1) * LANES] for c in range(tk // LANES)]
            tile_max = functools.reduce(jnp.maximum, tiles)
            m_old = m_ref[rows, :]
            m_new = jnp.maximum(m_old, jnp.max(tile_max, axis=-1, keepdims=True))
            m_ref[rows, :] = m_new
            a_refs[slot][rows, :] = jnp.exp2(m_old - m_new)
            for c, s in enumerate(tiles):
                p_refs[slot][rows, c * LANES:(c + 1) * LANES] = jnp.exp2(s - m_new).astype(BF16)

    def accumulate(c, slot):
        pv = _dot(p_refs[slot][...], v_ref[chunk(c), :])
        alpha = a_refs[slot][...]
        for c in range(V_PAD // LANES):
            cols = slice(c * LANES, (c + 1) * LANES)
            acc_ref[:, cols] = alpha * acc_ref[:, cols] + pv[:, cols]

    m_ref[...] = jnp.full(m_ref.shape, -jnp.inf, F32)
    acc_ref[...] = jnp.zeros(acc_ref.shape, F32)
    scores(0, 0)
    scores(1, 1)
    softmax(0)

    def body(tt, carry):
        for slot in (0, 1):
            t = 2 * tt + 2 + slot
            softmax(1 - slot)
            scores(t, slot)
            accumulate(t - 2, slot)
        return carry

    lax.fori_loop(0, (nk - 2) // 2, body, 0)
    softmax(1)
    accumulate(nk - 2, 0)
    accumulate(nk - 1, 1)
    acc = acc_ref[...]
    o_ref[...] = acc[:, :MLA_V] / acc[:, MLA_V:MLA_V + 1]


def _mla(q, k, v, tq=512, tk=512):
    return pl.pallas_call(
        functools.partial(_mla_kernel, tk=tk),
        grid=(HEADS, SEQ // tq),
        in_specs=[pl.BlockSpec((None, tq, QK_PAD), lambda h, i: (h, i, 0)),
                  pl.BlockSpec((None, SEQ, QK_PAD), lambda h, i: (h, 0, 0)),
                  pl.BlockSpec((None, SEQ, V_PAD), lambda h, i: (h, 0, 0))],
        out_specs=pl.BlockSpec((tq, MLA_V), lambda h, i: (i, h)),
        out_shape=jax.ShapeDtypeStruct((SEQ, MLA_WIDTH), F32),
        scratch_shapes=[pltpu.VMEM((tq, tk), F32), pltpu.VMEM((tq, tk), F32),
                        pltpu.VMEM((tq, tk), BF16), pltpu.VMEM((tq, tk), BF16),
                        pltpu.VMEM((tq, LANES), F32), pltpu.VMEM((tq, LANES), F32),
                        pltpu.VMEM((tq, V_PAD), F32), pltpu.VMEM((tq, LANES), F32)],
        compiler_params=pltpu.CompilerParams(dimension_semantics=("arbitrary", "arbitrary"),
                                             vmem_limit_bytes=VMEM_LIMIT),
        name="mla",
    )(q, k, v)


NA_QROWS = 4
NA_KROWS = 12
NA_TQ = NA_QROWS * GRID_W
NA_TK = NA_KROWS * GRID_W
NA_STEPS = GRID_ROWS // NA_QROWS
NA_BIAS_ROWS = 2 * NA_ROWS - 1
NA_BIAS_COLS = 2 * NA_COLS - 1
NA_STRIP = 32


def _na_bias_plan(variant):
    plan = {}
    for a in range(NA_QROWS):
        for b in range(NA_KROWS):
            if variant == "first":
                dr, ok = b - a + NA_ROWS - 1, b < NA_ROWS
            elif variant == "last":
                dr, ok = b - a - 1, b >= NA_KROWS - NA_ROWS
            else:
                dr, ok = b - a + NA_ROWS // 2 - 1, a <= b < a + NA_ROWS
            plan[(a, b)] = dr if ok else None
    return plan


def _na_kernel(rpb_ref, q_ref, k0_ref, k1_ref, k2_ref, v0_ref, v1_ref, v2_ref, o_ref, t_ref, bias_ref,
               s0_ref, s1_ref, p0_ref, p1_ref, l0_ref, l1_ref):
    g = pl.program_id(0)
    s_refs, p_refs, l_refs = (s0_ref, s1_ref), (p0_ref, p1_ref), (l0_ref, l1_ref)

    @pl.when(g == 0)
    def _():
        c = lax.broadcasted_iota(jnp.int32, (GRID_W, GRID_W), 0)
        kc = lax.broadcasted_iota(jnp.int32, (GRID_W, GRID_W), 1)
        start = jnp.clip(c - NA_COLS // 2, 0, GRID_W - NA_COLS)
        dc = jnp.where((kc >= start) & (kc < start + NA_COLS), kc - c + NA_COLS - 1, -1)

        def per_row(i, carry):
            tile = jnp.full((GRID_W, GRID_W), NEG_BIG, F32)
            for d in range(NA_BIAS_COLS):
                tile = jnp.where(dc == d, rpb_ref[i * NA_BIAS_COLS + d] * LOG2_E, tile)
            t_ref[i] = tile
            return carry

        lax.fori_loop(0, HEADS * NA_BIAS_ROWS, per_row, 0)

    def build(variant):
        plan = _na_bias_plan(variant)

        def per_head(h, carry):
            for (a, b), dr in plan.items():
                if dr is None:
                    tile = jnp.full((GRID_W, GRID_W), NEG_BIG, F32)
                else:
                    tile = t_ref[h * NA_BIAS_ROWS + dr]
                bias_ref[h, a * GRID_W:(a + 1) * GRID_W, b * GRID_W:(b + 1) * GRID_W] = tile
            return carry

        lax.fori_loop(0, HEADS, per_head, 0)

    pl.when(g == 0)(lambda: build("first"))
    pl.when(g == 1)(lambda: build("mid"))
    pl.when(g == NA_STEPS - 1)(lambda: build("last"))

    k_refs = (k0_ref, k1_ref, k2_ref)
    v_refs = (v0_ref, v1_ref, v2_ref)
    for h in range(HEADS):
        cols = slice(h * NA_HEAD_DIM, (h + 1) * NA_HEAD_DIM)
        s_ref, p_ref, l_ref = s_refs[h % 2], p_refs[h % 2], l_refs[h % 2]
        qh = q_ref[:, cols]
        for t in range(3):
            s_ref[:, t * NA_TQ:(t + 1) * NA_TQ] = _dot_nt(qh, k_refs[t][:, cols])
        for r in range(NA_TQ // NA_STRIP):
            rows = slice(r * NA_STRIP, (r + 1) * NA_STRIP)
            s = s_ref[rows, :] + bias_ref[h, rows, :]
            p = jnp.exp2(s - jnp.max(s, axis=-1, keepdims=True))
            l_ref[rows, :] = jnp.broadcast_to(jnp.sum(p, axis=-1, keepdims=True), (NA_STRIP, LANES))
            p_ref[rows, :] = p.astype(BF16)
        o = _dot(p_ref[:, 0:NA_TQ], v_refs[0][:, cols])
        for t in (1, 2):
            o = o + _dot(p_ref[:, t * NA_TQ:(t + 1) * NA_TQ], v_refs[t][:, cols])
        o_ref[:, cols] = o / l_ref[...]


def _na(rpb, nq, nk, nv):
    first_blk = lambda g: jnp.clip(g - 1, 0, NA_STEPS - 3)
    kv_specs = [pl.BlockSpec((NA_TQ, NA_WIDTH), functools.partial(lambda g, t: (first_blk(g) + t, 0), t=t))
                for t in range(3)]
    return pl.pallas_call(
        _na_kernel,
        grid=(NA_STEPS,),
        in_specs=[pl.BlockSpec(memory_space=pltpu.SMEM), pl.BlockSpec((NA_TQ, NA_WIDTH), lambda g: (g, 0))]
                 + kv_specs + kv_specs,
        out_specs=pl.BlockSpec((NA_TQ, NA_WIDTH), lambda g: (g, 0)),
        out_shape=jax.ShapeDtypeStruct((SEQ, NA_WIDTH), F32),
        scratch_shapes=[pltpu.VMEM((HEADS * NA_BIAS_ROWS, GRID_W, GRID_W), F32),
                        pltpu.VMEM((HEADS, NA_TQ, NA_TK), F32),
                        pltpu.VMEM((NA_TQ, NA_TK), F32), pltpu.VMEM((NA_TQ, NA_TK), F32),
                        pltpu.VMEM((NA_TQ, NA_TK), BF16), pltpu.VMEM((NA_TQ, NA_TK), BF16),
                        pltpu.VMEM((NA_TQ, LANES), F32), pltpu.VMEM((NA_TQ, LANES), F32)],
        compiler_params=pltpu.CompilerParams(dimension_semantics=("arbitrary",),
                                             vmem_limit_bytes=VMEM_LIMIT),
        name="na",
    )(rpb.astype(F32).reshape(-1), nq, nk, nk, nk, nv, nv, nv)


def _outproj_kernel(a_ref, n_ref, ga_ref, gn_ref, w_ref, x_ref, o_ref, wb_ref):
    @pl.when(pl.program_id(0) == 0)
    def _():
        wb_ref[...] = w_ref[...].astype(BF16)

    an = _rms(a_ref[...], ga_ref[...]).astype(BF16)
    nn = _rms(n_ref[...], gn_ref[...]).astype(BF16)
    o_ref[...] = x_ref[...] + _dot(an, wb_ref[0:MLA_WIDTH, :]) + _dot(nn, wb_ref[MLA_WIDTH:, :])


def _outproj(a, n, ga, gn, w, x, tm=512):
    row = lambda w_: pl.BlockSpec((tm, w_), lambda i: (i, 0))
    return pl.pallas_call(
        _outproj_kernel,
        grid=(SEQ // tm,),
        in_specs=[row(MLA_WIDTH), row(NA_WIDTH), _const_spec((1, MLA_WIDTH)), _const_spec((1, NA_WIDTH)),
                  _const_spec((MLA_WIDTH + NA_WIDTH, D_MODEL)), row(D_MODEL)],
        out_specs=row(D_MODEL),
        out_shape=jax.ShapeDtypeStruct((SEQ, D_MODEL), F32),
        scratch_shapes=[pltpu.VMEM((MLA_WIDTH + NA_WIDTH, D_MODEL), BF16)],
        compiler_params=pltpu.CompilerParams(dimension_semantics=("arbitrary",),
                                             vmem_limit_bytes=VMEM_LIMIT),
        name="outproj",
    )(a, n, ga, gn, w, x)


def _mlp_kernel(x_ref, g_ref, w1_ref, w2_ref, gf_ref, o_ref, h_ref, acc_ref):
    j = pl.program_id(1)

    @pl.when(j == 0)
    def _():
        x = x_ref[...]
        h_ref[...] = _rms(x, g_ref[...]).astype(BF16)
        acc_ref[...] = x

    hid = jnp.maximum(_dot(h_ref[...], w1_ref[...]), 0.0)
    acc_ref[...] += _dot((hid * hid).astype(BF16), w2_ref[...])

    @pl.when(j == pl.num_programs(1) - 1)
    def _():
        o_ref[...] = _rms(acc_ref[...], gf_ref[...])


def _mlp(x, g, w1, w2, gf, tm=512, tf=1024):
    return pl.pallas_call(
        _mlp_kernel,
        grid=(SEQ // tm, D_FF // tf),
        in_specs=[pl.BlockSpec((tm, D_MODEL), lambda i, j: (i, 0)),
                  _const_spec((1, D_MODEL)),
                  pl.BlockSpec((D_MODEL, tf), lambda i, j: (0, j)),
                  pl.BlockSpec((tf, D_MODEL), lambda i, j: (j, 0)),
                  _const_spec((1, D_MODEL))],
        out_specs=pl.BlockSpec((tm, D_MODEL), lambda i, j: (i, 0)),
        out_shape=jax.ShapeDtypeStruct((SEQ, D_MODEL), F32),
        scratch_shapes=[pltpu.VMEM((tm, D_MODEL), BF16), pltpu.VMEM((tm, D_MODEL), F32)],
        compiler_params=pltpu.CompilerParams(dimension_semantics=("arbitrary", "arbitrary"),
                                             vmem_limit_bytes=VMEM_LIMIT),
        name="mlp",
    )(x, g, w1, w2, gf)


def _rope_tables():
    half = MLA_ROPE // 2
    inv = ROPE_THETA ** (-jnp.arange(half, dtype=F32) / half)
    ang = jnp.arange(SEQ).astype(F32)[:, None] * inv[None, :]
    c, s = jnp.cos(ang), jnp.sin(ang)
    reps = LANES // MLA_ROPE
    return jnp.tile(jnp.concatenate([c, c], axis=1), (1, reps)), jnp.tile(jnp.concatenate([-s, s], axis=1), (1, reps))


def kernel(x, attn_norm_g, w_in, q_norm_g, w_uq, kv_norm_g, w_ukv, na_rpb, mla_out_norm_g, na_out_norm_g,
           w_out, mlp_norm_g, w_ff1, w_ff2, final_norm_g):
    assert x.shape == (1, SEQ, D_MODEL)
    assert w_in.shape[0] == 1, "the final norm is fused into the (single) layer's MLP kernel"
    cos, sin = _rope_tables()
    xs = x[0]
    for l in range(1):
        wi = w_in[l]
        kr_cols = wi[:, MLA_Q_RANK + MLA_KV_RANK: MLA_Q_RANK + MLA_KV_RANK + MLA_ROPE]
        win = jnp.concatenate([wi[:, :MLA_Q_RANK + MLA_KV_RANK], wi[:, MLA_Q_RANK + MLA_KV_RANK + MLA_ROPE:],
                               kr_cols, kr_cols], axis=1).astype(BF16)
        wq = w_uq[l].reshape(MLA_Q_RANK, HEADS, MLA_QK)
        wuq = jnp.concatenate([wq[:, :, :MLA_NOPE].reshape(MLA_Q_RANK, HEADS * MLA_NOPE),
                               wq[:, :, MLA_NOPE:].reshape(MLA_Q_RANK, HEADS * MLA_ROPE)], axis=1).astype(BF16)
        wk = w_ukv[l].reshape(MLA_KV_RANK, HEADS, MLA_NOPE + MLA_V)
        wukv = jnp.concatenate([wk[:, :, :MLA_NOPE].reshape(MLA_KV_RANK, HEADS * MLA_NOPE),
                                wk[:, :, MLA_NOPE:].reshape(MLA_KV_RANK, HEADS * MLA_V)], axis=1).astype(BF16)

        q, k, v, nq, nk, nv = _proj(xs, attn_norm_g[l][None], win, q_norm_g[l][None], wuq,
                                    kv_norm_g[l][None], wukv, cos, sin)
        a_out = _mla(q, k, v)
        n_out = _na(na_rpb[l], nq, nk, nv)
        xs = _outproj(a_out, n_out, mla_out_norm_g[l][None], na_out_norm_g[l][None],
                      w_out[l], xs)
        xs = _mlp(xs, mlp_norm_g[l][None], w_ff1[l].astype(BF16), w_ff2[l].astype(BF16), final_norm_g[None])
    return xs[None]
```

```python
import functools

import jax
import jax.numpy as jnp
from jax import lax
from jax.experimental import pallas as pl
from jax.experimental.pallas import tpu as pltpu

F32 = jnp.float32
BF16 = jnp.bfloat16

D_MODEL = 2048
SEQ = 8192
GRID_W = 64
GRID_ROWS = SEQ // GRID_W
HEADS = 8
MLA_NOPE = 128
MLA_ROPE = 64
MLA_V = 128
MLA_QK = MLA_NOPE + MLA_ROPE
MLA_Q_RANK = 512
MLA_KV_RANK = 256
ROPE_THETA = 10000.0
NA_HEAD_DIM = 128
NA_ROWS = 8
NA_COLS = 16
NA_WIDTH = HEADS * NA_HEAD_DIM
MLA_WIDTH = HEADS * MLA_V
D_FF = 4 * D_MODEL
NORM_EPS = 1e-6

LANES = 128
QK_PAD = 2 * LANES
V_PAD = 2 * LANES
LOG2_E = 1.4426950408889634
NEG_BIG = -1e30

_C_CQ = 0
_C_CKV = _C_CQ + MLA_Q_RANK
_C_NQ = _C_CKV + MLA_KV_RANK
_C_NK = _C_NQ + NA_WIDTH
_C_NV = _C_NK + NA_WIDTH
_C_KR = _C_NV + NA_WIDTH
IN_COLS_PAD = _C_KR + 2 * MLA_ROPE

VMEM_LIMIT = 56 * 1024 * 1024


def _rms(xf, g):
    y = xf * lax.rsqrt(jnp.mean(xf * xf, axis=-1, keepdims=True) + NORM_EPS)
    return y * g


def _dot(a, b):
    return jnp.dot(a, b, preferred_element_type=F32)


def _dot_nt(a, b):
    return lax.dot_general(a, b, (((1,), (1,)), ((), ())), preferred_element_type=F32)


def _const_spec(shape):
    nd = len(shape)
    return pl.BlockSpec(shape, lambda *_: (0,) * nd, pipeline_mode=pl.Buffered(1))


def _rope_lanes(x, c, s, lo_half):
    partner = jnp.where(lo_half, pltpu.roll(x, LANES - MLA_ROPE // 2, 1), pltpu.roll(x, MLA_ROPE // 2, 1))
    return x * c + partner * s


def _proj_kernel(x_ref, g_ref, win_ref, qg_ref, wuq_ref, kvg_ref, wukv_ref, cos_ref, sin_ref,
                 q_ref, k_ref, v_ref, nq_ref, nk_ref, nv_ref):
    tm = x_ref.shape[0]
    h = _rms(x_ref[...], g_ref[...]).astype(BF16)

    def proj(lo, hi):
        return _dot(h, win_ref[:, lo:hi])

    nq_ref[...] = (proj(_C_NQ, _C_NK) * (LOG2_E * NA_HEAD_DIM ** -0.5)).astype(BF16)
    nk_ref[...] = proj(_C_NK, _C_NV).astype(BF16)
    nv_ref[...] = proj(_C_NV, _C_KR).astype(BF16)

    cos = cos_ref[...]
    sin = sin_ref[...]
    lane = lax.broadcasted_iota(jnp.int32, (tm, LANES), 1)
    lo_half = (lane % MLA_ROPE) < (MLA_ROPE // 2)

    cqn = _rms(proj(_C_CQ, _C_CKV), qg_ref[...]).astype(BF16)
    q = _dot(cqn, wuq_ref[...]) * (LOG2_E * MLA_QK ** -0.5)
    rope0 = HEADS * MLA_NOPE
    for pair in range(HEADS // 2):
        qr = _rope_lanes(q[:, rope0 + pair * LANES: rope0 + (pair + 1) * LANES], cos, sin, lo_half)
        qr = qr.astype(BF16)
        for hh in (2 * pair, 2 * pair + 1):
            q_ref[hh, :, 0:LANES] = q[:, hh * MLA_NOPE:(hh + 1) * MLA_NOPE].astype(BF16)
            q_ref[hh, :, LANES:QK_PAD] = qr

    ckvn = _rms(proj(_C_CKV, _C_NQ), kvg_ref[...]).astype(BF16)
    kv = _dot(ckvn, wukv_ref[...])
    kr = _rope_lanes(proj(_C_KR, IN_COLS_PAD), cos, sin, lo_half)
    kr_even = jnp.where(lane < MLA_ROPE, kr, 0.0).astype(BF16)
    kr_odd = jnp.where(lane >= MLA_ROPE, kr, 0.0).astype(BF16)
    v0 = HEADS * MLA_NOPE
    ones_col = jnp.where(lane == 0, 1.0, 0.0).astype(BF16)
    for hh in range(HEADS):
        k_ref[hh, :, 0:LANES] = kv[:, hh * MLA_NOPE:(hh + 1) * MLA_NOPE].astype(BF16)
        k_ref[hh, :, LANES:QK_PAD] = kr_even if hh % 2 == 0 else kr_odd
        v_ref[hh, :, 0:MLA_V] = kv[:, v0 + hh * MLA_V: v0 + (hh + 1) * MLA_V].astype(BF16)
        v_ref[hh, :, MLA_V:V_PAD] = ones_col


def _proj(x, g, win, qg, wuq, kvg, wukv, cos, sin, tm=256):
    row = lambda w: pl.BlockSpec((tm, w), lambda i: (i, 0))
    head = lambda w: pl.BlockSpec((HEADS, tm, w), lambda i: (0, i, 0))
    return pl.pallas_call(
        _proj_kernel,
        grid=(SEQ // tm,),
        in_specs=[row(D_MODEL), _const_spec((1, D_MODEL)), _const_spec((D_MODEL, IN_COLS_PAD)),
                  _const_spec((1, MLA_Q_RANK)), _const_spec((MLA_Q_RANK, HEADS * MLA_QK)),
                  _const_spec((1, MLA_KV_RANK)), _const_spec((MLA_KV_RANK, HEADS * (MLA_NOPE + MLA_V))),
                  row(LANES), row(LANES)],
        out_specs=[head(QK_PAD), head(QK_PAD), head(V_PAD), row(NA_WIDTH), row(NA_WIDTH), row(NA_WIDTH)],
        out_shape=[jax.ShapeDtypeStruct((HEADS, SEQ, QK_PAD), BF16),
                   jax.ShapeDtypeStruct((HEADS, SEQ, QK_PAD), BF16),
                   jax.ShapeDtypeStruct((HEADS, SEQ, V_PAD), BF16),
                   jax.ShapeDtypeStruct((SEQ, NA_WIDTH), BF16),
                   jax.ShapeDtypeStruct((SEQ, NA_WIDTH), BF16),
                   jax.ShapeDtypeStruct((SEQ, NA_WIDTH), BF16)],
        compiler_params=pltpu.CompilerParams(dimension_semantics=("arbitrary",),
                                             vmem_limit_bytes=VMEM_LIMIT),
        name="proj",
    )(x, g, win, qg, wuq, kvg, wukv, cos, sin)


MLA_STRIP = 64


def _mla_kernel(q_ref, k_ref, v_ref, o_ref, s0_ref, s1_ref, p0_ref, p1_ref, a0_ref, a1_ref, acc_ref, m_ref,
                *, tk):
    tq = q_ref.shape[0]
    nk = SEQ // tk
    s_refs, p_refs, a_refs = (s0_ref, s1_ref), (p0_ref, p1_ref), (a0_ref, a1_ref)

    def chunk(c):
        return pl.ds(pl.multiple_of(c * tk, tk), tk)

    def scores(c, slot):
        s_refs[slot][...] = _dot_nt(q_ref[...], k_ref[chunk(c), :])

    def softmax(slot):
        for r in range(tq // MLA_STRIP):
            rows = slice(r * MLA_STRIP, (r + 1) * MLA_STRIP)
            tiles = [s_refs[slot][rows, c * LANES:(c + 1) * LANES] for c in range(tk // LANES)]
            tile_max = functools.reduce(jnp.maximum, tiles)
            m_old = m_ref[rows, :]
            m_new = jnp.maximum(m_old, jnp.max(tile_max, axis=-1, keepdims=True))
            m_ref[rows, :] = m_new
            a_refs[slot][rows, :] = jnp.exp2(m_old - m_new)
            for c, s in enumerate(tiles):
                p_refs[slot][rows, c * LANES:(c + 1) * LANES] = jnp.exp2(s - m_new).astype(BF16)

    def accumulate(c, slot):
        pv = _dot(p_refs[slot][...], v_ref[chunk(c), :])
        alpha = a_refs[slot][...]
        for c in range(V_PAD // LANES):
            cols = slice(c * LANES, (c + 1) * LANES)
            acc_ref[:, cols] = alpha * acc_ref[:, cols] + pv[:, cols]

    m_ref[...] = jnp.full(m_ref.shape, -jnp.inf, F32)
    acc_ref[...] = jnp.zeros(acc_ref.shape, F32)
    scores(0, 0)
    scores(1, 1)
    softmax(0)

    for t in range(2, nk):
        slot = t % 2
        softmax(1 - slot)
        scores(t, slot)
        accumulate(t - 2, slot)
    softmax(1)
    accumulate(nk - 2, 0)
    accumulate(nk - 1, 1)
    acc = acc_ref[...]
    o_ref[...] = acc[:, :MLA_V] / acc[:, MLA_V:MLA_V + 1]


def _mla(q, k, v, tq=512, tk=512):
    return pl.pallas_call(
        functools.partial(_mla_kernel, tk=tk),
        grid=(HEADS, SEQ // tq),
        in_specs=[pl.BlockSpec((None, tq, QK_PAD), lambda h, i: (h, i, 0)),
                  pl.BlockSpec((None, SEQ, QK_PAD), lambda h, i: (h, 0, 0)),
                  pl.BlockSpec((None, SEQ, V_PAD), lambda h, i: (h, 0, 0))],
        out_specs=pl.BlockSpec((tq, MLA_V), lambda h, i: (i, h)),
        out_shape=jax.ShapeDtypeStruct((SEQ, MLA_WIDTH), F32),
        scratch_shapes=[pltpu.VMEM((tq, tk), F32), pltpu.VMEM((tq, tk), F32),
                        pltpu.VMEM((tq, tk), BF16), pltpu.VMEM((tq, tk), BF16),
                        pltpu.VMEM((tq, LANES), F32), pltpu.VMEM((tq, LANES), F32),
                        pltpu.VMEM((tq, V_PAD), F32), pltpu.VMEM((tq, LANES), F32)],
        compiler_params=pltpu.CompilerParams(dimension_semantics=("arbitrary", "arbitrary"),
                                             vmem_limit_bytes=VMEM_LIMIT),
        name="mla",
    )(q, k, v)


NA_QROWS = 4
NA_KROWS = 12
NA_TQ = NA_QROWS * GRID_W
NA_TK = NA_KROWS * GRID_W
NA_STEPS = GRID_ROWS // NA_QROWS
NA_BIAS_ROWS = 2 * NA_ROWS - 1
NA_BIAS_COLS = 2 * NA_COLS - 1
NA_STRIP = 32


def _na_bias_plan(variant):
    plan = {}
    for a in range(NA_QROWS):
        for b in range(NA_KROWS):
            if variant == "first":
                dr, ok = b - a + NA_ROWS - 1, b < NA_ROWS
            elif variant == "last":
                dr, ok = b - a - 1, b >= NA_KROWS - NA_ROWS
            else:
                dr, ok = b - a + NA_ROWS // 2 - 1, a <= b < a + NA_ROWS
            plan[(a, b)] = dr if ok else None
    return plan


def _na_kernel(rpb_ref, q_ref, k0_ref, k1_ref, k2_ref, v0_ref, v1_ref, v2_ref, o_ref, t_ref, bias_ref,
               s0_ref, s1_ref, p0_ref, p1_ref, l0_ref, l1_ref):
    g = pl.program_id(0)
    s_refs, p_refs, l_refs = (s0_ref, s1_ref), (p0_ref, p1_ref), (l0_ref, l1_ref)

    @pl.when(g == 0)
    def _():
        c = lax.broadcasted_iota(jnp.int32, (GRID_W, GRID_W), 0)
        kc = lax.broadcasted_iota(jnp.int32, (GRID_W, GRID_W), 1)
        start = jnp.clip(c - NA_COLS // 2, 0, GRID_W - NA_COLS)
        dc = jnp.where((kc >= start) & (kc < start + NA_COLS), kc - c + NA_COLS - 1, -1)

        def per_row(i, carry):
            tile = jnp.full((GRID_W, GRID_W), NEG_BIG, F32)
            for d in range(NA_BIAS_COLS):
                tile = jnp.where(dc == d, rpb_ref[i * NA_BIAS_COLS + d] * LOG2_E, tile)
            t_ref[i] = tile
            return carry

        lax.fori_loop(0, HEADS * NA_BIAS_ROWS, per_row, 0)

    def build(variant):
        plan = _na_bias_plan(variant)

        def per_head(h, carry):
            for (a, b), dr in plan.items():
                if dr is None:
                    tile = jnp.full((GRID_W, GRID_W), NEG_BIG, F32)
                else:
                    tile = t_ref[h * NA_BIAS_ROWS + dr]
                bias_ref[h, a * GRID_W:(a + 1) * GRID_W, b * GRID_W:(b + 1) * GRID_W] = tile
            return carry

        lax.fori_loop(0, HEADS, per_head, 0)

    pl.when(g == 0)(lambda: build("first"))
    pl.when(g == 1)(lambda: build("mid"))
    pl.when(g == NA_STEPS - 1)(lambda: build("last"))

    k_refs = (k0_ref, k1_ref, k2_ref)
    v_refs = (v0_ref, v1_ref, v2_ref)
    for h in range(HEADS):
        cols = slice(h * NA_HEAD_DIM, (h + 1) * NA_HEAD_DIM)
        s_ref, p_ref, l_ref = s_refs[h % 2], p_refs[h % 2], l_refs[h % 2]
        qh = q_ref[:, cols]
        for t in range(3):
            s_ref[:, t * NA_TQ:(t + 1) * NA_TQ] = _dot_nt(qh, k_refs[t][:, cols])
        for r in range(NA_TQ // NA_STRIP):
            rows = slice(r * NA_STRIP, (r + 1) * NA_STRIP)
            s = s_ref[rows, :] + bias_ref[h, rows, :]
            p = jnp.exp2(s - jnp.max(s, axis=-1, keepdims=True))
            l_ref[rows, :] = jnp.broadcast_to(jnp.sum(p, axis=-1, keepdims=True), (NA_STRIP, LANES))
            p_ref[rows, :] = p.astype(BF16)
        o = _dot(p_ref[:, 0:NA_TQ], v_refs[0][:, cols])
        for t in (1, 2):
            o = o + _dot(p_ref[:, t * NA_TQ:(t + 1) * NA_TQ], v_refs[t][:, cols])
        o_ref[:, cols] = o / l_ref[...]


def _na(rpb, nq, nk, nv):
    first_blk = lambda g: jnp.clip(g - 1, 0, NA_STEPS - 3)
    kv_specs = [pl.BlockSpec((NA_TQ, NA_WIDTH), functools.partial(lambda g, t: (first_blk(g) + t, 0), t=t))
                for t in range(3)]
    return pl.pallas_call(
        _na_kernel,
        grid=(NA_STEPS,),
        in_specs=[pl.BlockSpec(memory_space=pltpu.SMEM), pl.BlockSpec((NA_TQ, NA_WIDTH), lambda g: (g, 0))]
                 + kv_specs + kv_specs,
        out_specs=pl.BlockSpec((NA_TQ, NA_WIDTH), lambda g: (g, 0)),
        out_shape=jax.ShapeDtypeStruct((SEQ, NA_WIDTH), F32),
        scratch_shapes=[pltpu.VMEM((HEADS * NA_BIAS_ROWS, GRID_W, GRID_W), F32),
                        pltpu.VMEM((HEADS, NA_TQ, NA_TK), F32),
                        pltpu.VMEM((NA_TQ, NA_TK), F32), pltpu.VMEM((NA_TQ, NA_TK), F32),
                        pltpu.VMEM((NA_TQ, NA_TK), BF16), pltpu.VMEM((NA_TQ, NA_TK), BF16),
                        pltpu.VMEM((NA_TQ, LANES), F32), pltpu.VMEM((NA_TQ, LANES), F32)],
        compiler_params=pltpu.CompilerParams(dimension_semantics=("arbitrary",),
                                             vmem_limit_bytes=VMEM_LIMIT),
        name="na",
    )(rpb.astype(F32).reshape(-1), nq, nk, nk, nk, nv, nv, nv)


def _outproj_kernel(a_ref, n_ref, ga_ref, gn_ref, w_ref, x_ref, o_ref, wb_ref):
    @pl.when(pl.program_id(0) == 0)
    def _():
        wb_ref[...] = w_ref[...].astype(BF16)

    an = _rms(a_ref[...], ga_ref[...]).astype(BF16)
    nn = _rms(n_ref[...], gn_ref[...]).astype(BF16)
    o_ref[...] = x_ref[...] + _dot(an, wb_ref[0:MLA_WIDTH, :]) + _dot(nn, wb_ref[MLA_WIDTH:, :])


def _outproj(a, n, ga, gn, w, x, tm=512):
    row = lambda w_: pl.BlockSpec((tm, w_), lambda i: (i, 0))
    return pl.pallas_call(
        _outproj_kernel,
        grid=(SEQ // tm,),
        in_specs=[row(MLA_WIDTH), row(NA_WIDTH), _const_spec((1, MLA_WIDTH)), _const_spec((1, NA_WIDTH)),
                  _const_spec((MLA_WIDTH + NA_WIDTH, D_MODEL)), row(D_MODEL)],
        out_specs=row(D_MODEL),
        out_shape=jax.ShapeDtypeStruct((SEQ, D_MODEL), F32),
        scratch_shapes=[pltpu.VMEM((MLA_WIDTH + NA_WIDTH, D_MODEL), BF16)],
        compiler_params=pltpu.CompilerParams(dimension_semantics=("arbitrary",),
                                             vmem_limit_bytes=VMEM_LIMIT),
        name="outproj",
    )(a, n, ga, gn, w, x)


def _mlp_kernel(x_ref, g_ref, w1_ref, w2_ref, gf_ref, o_ref, h_ref, acc_ref):
    j = pl.program_id(1)

    @pl.when(j == 0)
    def _():
        x = x_ref[...]
        h_ref[...] = _rms(x, g_ref[...]).astype(BF16)
        acc_ref[...] = x

    hid = jnp.maximum(_dot(h_ref[...], w1_ref[...]), 0.0)
    acc_ref[...] += _dot((hid * hid).astype(BF16), w2_ref[...])

    @pl.when(j == pl.num_programs(1) - 1)
    def _():
        o_ref[...] = _rms(acc_ref[...], gf_ref[...])


def _mlp(x, g, w1, w2, gf, tm=512, tf=1024):
    return pl.pallas_call(
        _mlp_kernel,
        grid=(SEQ // tm, D_FF // tf),
        in_specs=[pl.BlockSpec((tm, D_MODEL), lambda i, j: (i, 0)),
                  _const_spec((1, D_MODEL)),
                  pl.BlockSpec((D_MODEL, tf), lambda i, j: (0, j)),
                  pl.BlockSpec((tf, D_MODEL), lambda i, j: (j, 0)),
                  _const_spec((1, D_MODEL))],
        out_specs=pl.BlockSpec((tm, D_MODEL), lambda i, j: (i, 0)),
        out_shape=jax.ShapeDtypeStruct((SEQ, D_MODEL), F32),
        scratch_shapes=[pltpu.VMEM((tm, D_MODEL), BF16), pltpu.VMEM((tm, D_MODEL), F32)],
        compiler_params=pltpu.CompilerParams(dimension_semantics=("arbitrary", "arbitrary"),
                                             vmem_limit_bytes=VMEM_LIMIT),
        name="mlp",
    )(x, g, w1, w2, gf)


def _rope_tables():
    half = MLA_ROPE // 2
    inv = ROPE_THETA ** (-jnp.arange(half, dtype=F32) / half)
    ang = jnp.arange(SEQ).astype(F32)[:, None] * inv[None, :]
    c, s = jnp.cos(ang), jnp.sin(ang)
    reps = LANES // MLA_ROPE
    return jnp.tile(jnp.concatenate([c, c], axis=1), (1, reps)), jnp.tile(jnp.concatenate([-s, s], axis=1), (1, reps))


def kernel(x, attn_norm_g, w_in, q_norm_g, w_uq, kv_norm_g, w_ukv, na_rpb, mla_out_norm_g, na_out_norm_g,
           w_out, mlp_norm_g, w_ff1, w_ff2, final_norm_g):
    assert x.shape == (1, SEQ, D_MODEL)
    assert w_in.shape[0] == 1, "the final norm is fused into the (single) layer's MLP kernel"
    cos, sin = _rope_tables()
    xs = x[0]
    for l in range(1):
        wi = w_in[l].astype(BF16)
        kr_cols = wi[:, MLA_Q_RANK + MLA_KV_RANK: MLA_Q_RANK + MLA_KV_RANK + MLA_ROPE]
        win = jnp.concatenate([wi[:, :MLA_Q_RANK + MLA_KV_RANK], wi[:, MLA_Q_RANK + MLA_KV_RANK + MLA_ROPE:],
                               kr_cols, kr_cols], axis=1)
        wq = w_uq[l].reshape(MLA_Q_RANK, HEADS, MLA_QK)
        wuq = jnp.concatenate([wq[:, :, :MLA_NOPE].reshape(MLA_Q_RANK, HEADS * MLA_NOPE),
                               wq[:, :, MLA_NOPE:].reshape(MLA_Q_RANK, HEADS * MLA_ROPE)], axis=1).astype(BF16)
        wk = w_ukv[l].reshape(MLA_KV_RANK, HEADS, MLA_NOPE + MLA_V)
        wukv = jnp.concatenate([wk[:, :, :MLA_NOPE].reshape(MLA_KV_RANK, HEADS * MLA_NOPE),
                                wk[:, :, MLA_NOPE:].reshape(MLA_KV_RANK, HEADS * MLA_V)], axis=1).astype(BF16)

        q, k, v, nq, nk, nv = _proj(xs, attn_norm_g[l][None], win, q_norm_g[l][None], wuq,
                                    kv_norm_g[l][None], wukv, cos, sin)
        a_out = _mla(q, k, v)
        n_out = _na(na_rpb[l], nq, nk, nv)
        xs = _outproj(a_out, n_out, mla_out_norm_g[l][None], na_out_norm_g[l][None],
                      w_out[l], xs)
        xs = _mlp(xs, mlp_norm_g[l][None], w_ff1[l].astype(BF16), w_ff2[l].astype(BF16), final_norm_g[None])
    return xs[None]
```

```python
import functools

import jax
import jax.numpy as jnp
from jax import lax
from jax.experimental import pallas as pl
from jax.experimental.pallas import tpu as pltpu

F32 = jnp.float32
BF16 = jnp.bfloat16

D_MODEL = 2048
SEQ = 8192
GRID_W = 64
GRID_ROWS = SEQ // GRID_W
HEADS = 8
MLA_NOPE = 128
MLA_ROPE = 64
MLA_V = 128
MLA_QK = MLA_NOPE + MLA_ROPE
MLA_Q_RANK = 512
MLA_KV_RANK = 256
ROPE_THETA = 10000.0
NA_HEAD_DIM = 128
NA_ROWS = 8
NA_COLS = 16
NA_WIDTH = HEADS * NA_HEAD_DIM
MLA_WIDTH = HEADS * MLA_V
D_FF = 4 * D_MODEL
NORM_EPS = 1e-6

LANES = 128
QK_PAD = 2 * LANES
V_PAD = 2 * LANES
LOG2_E = 1.4426950408889634
NEG_BIG = -1e30

LAT_COLS = MLA_Q_RANK + MLA_KV_RANK
LOC_COLS = MLA_ROPE + 3 * NA_WIDTH
_C_NQ = 0
_C_NK = _C_NQ + NA_WIDTH
_C_NV = _C_NK + NA_WIDTH
_C_KR = _C_NV + NA_WIDTH
LOC_COLS_ALIGNED = _C_KR + 2 * MLA_ROPE
PROJ_ALIGN_ROWS = 256

VMEM_LIMIT = 56 * 1024 * 1024


def _rms(xf, g):
    y = xf * lax.rsqrt(jnp.mean(xf * xf, axis=-1, keepdims=True) + NORM_EPS)
    return y * g


def _dot(a, b):
    return jnp.dot(a, b, preferred_element_type=F32)


def _dot_nt(a, b):
    return lax.dot_general(a, b, (((1,), (1,)), ((), ())), preferred_element_type=F32)


def _const_spec(shape):
    nd = len(shape)
    return pl.BlockSpec(shape, lambda *_: (0,) * nd, pipeline_mode=pl.Buffered(1))


def _rope_lanes(x, c, s, lo_half):
    partner = jnp.where(lo_half, pltpu.roll(x, LANES - MLA_ROPE // 2, 1), pltpu.roll(x, MLA_ROPE // 2, 1))
    return x * c + partner * s


def _proj_kernel(x_ref, g_ref, wlat_ref, wloc_ref, qg_ref, wuq_ref, kvg_ref, wukv_ref, cos_ref, sin_ref,
                 q_ref, k_ref, v_ref, nq_ref, nk_ref, nv_ref, wal_ref):
    tm = x_ref.shape[0]

    @pl.when(pl.program_id(0) == 0)
    def _():
        def rows(i, carry):
            r = pl.ds(pl.multiple_of(i * PROJ_ALIGN_ROWS, PROJ_ALIGN_ROWS), PROJ_ALIGN_ROWS)
            w = wloc_ref[r, :]
            wal_ref[r, 0:_C_KR] = w[:, MLA_ROPE:]
            wal_ref[r, _C_KR:] = jnp.concatenate([w[:, :MLA_ROPE], w[:, :MLA_ROPE]], axis=1)
            return carry

        lax.fori_loop(0, D_MODEL // PROJ_ALIGN_ROWS, rows, 0)

    h = _rms(x_ref[...], g_ref[...]).astype(BF16)

    def proj(lo, hi):
        return _dot(h, wal_ref[:, lo:hi])

    nq_ref[...] = (proj(_C_NQ, _C_NK) * (LOG2_E * NA_HEAD_DIM ** -0.5)).astype(BF16)
    nk_ref[...] = proj(_C_NK, _C_NV).astype(BF16)
    nv_ref[...] = proj(_C_NV, _C_KR).astype(BF16)

    cos = cos_ref[...]
    sin = sin_ref[...]
    lane = lax.broadcasted_iota(jnp.int32, (tm, LANES), 1)
    lo_half = (lane % MLA_ROPE) < (MLA_ROPE // 2)

    cqn = _rms(_dot(h, wlat_ref[:, :MLA_Q_RANK]), qg_ref[...]).astype(BF16)
    q = _dot(cqn, wuq_ref[...]) * (LOG2_E * MLA_QK ** -0.5)
    rope0 = HEADS * MLA_NOPE
    for pair in range(HEADS // 2):
        qr = _rope_lanes(q[:, rope0 + pair * LANES: rope0 + (pair + 1) * LANES], cos, sin, lo_half)
        qr = qr.astype(BF16)
        for hh in (2 * pair, 2 * pair + 1):
            q_ref[hh, :, 0:LANES] = q[:, hh * MLA_NOPE:(hh + 1) * MLA_NOPE].astype(BF16)
            q_ref[hh, :, LANES:QK_PAD] = qr

    ckvn = _rms(_dot(h, wlat_ref[:, MLA_Q_RANK:]), kvg_ref[...]).astype(BF16)
    kv = _dot(ckvn, wukv_ref[...])
    kr = _rope_lanes(proj(_C_KR, LOC_COLS_ALIGNED), cos, sin, lo_half)
    kr_even = jnp.where(lane < MLA_ROPE, kr, 0.0).astype(BF16)
    kr_odd = jnp.where(lane >= MLA_ROPE, kr, 0.0).astype(BF16)
    v0 = HEADS * MLA_NOPE
    ones_col = jnp.where(lane == 0, 1.0, 0.0).astype(BF16)
    for hh in range(HEADS):
        k_ref[hh, :, 0:LANES] = kv[:, hh * MLA_NOPE:(hh + 1) * MLA_NOPE].astype(BF16)
        k_ref[hh, :, LANES:QK_PAD] = kr_even if hh % 2 == 0 else kr_odd
        v_ref[hh, :, 0:MLA_V] = kv[:, v0 + hh * MLA_V: v0 + (hh + 1) * MLA_V].astype(BF16)
        v_ref[hh, :, MLA_V:V_PAD] = ones_col


def _proj(x, g, wlat, wloc, qg, wuq, kvg, wukv, cos, sin, tm=256):
    row = lambda w: pl.BlockSpec((tm, w), lambda i: (i, 0))
    head = lambda w: pl.BlockSpec((HEADS, tm, w), lambda i: (0, i, 0))
    return pl.pallas_call(
        _proj_kernel,
        grid=(SEQ // tm,),
        in_specs=[row(D_MODEL), _const_spec((1, D_MODEL)), _const_spec((D_MODEL, LAT_COLS)),
                  _const_spec((D_MODEL, LOC_COLS)),
                  _const_spec((1, MLA_Q_RANK)), _const_spec((MLA_Q_RANK, HEADS * MLA_QK)),
                  _const_spec((1, MLA_KV_RANK)), _const_spec((MLA_KV_RANK, HEADS * (MLA_NOPE + MLA_V))),
                  row(LANES), row(LANES)],
        out_specs=[head(QK_PAD), head(QK_PAD), head(V_PAD), row(NA_WIDTH), row(NA_WIDTH), row(NA_WIDTH)],
        out_shape=[jax.ShapeDtypeStruct((HEADS, SEQ, QK_PAD), BF16),
                   jax.ShapeDtypeStruct((HEADS, SEQ, QK_PAD), BF16),
                   jax.ShapeDtypeStruct((HEADS, SEQ, V_PAD), BF16),
                   jax.ShapeDtypeStruct((SEQ, NA_WIDTH), BF16),
                   jax.ShapeDtypeStruct((SEQ, NA_WIDTH), BF16),
                   jax.ShapeDtypeStruct((SEQ, NA_WIDTH), BF16)],
        scratch_shapes=[pltpu.VMEM((D_MODEL, LOC_COLS_ALIGNED), BF16)],
        compiler_params=pltpu.CompilerParams(dimension_semantics=("arbitrary",),
                                             vmem_limit_bytes=VMEM_LIMIT),
        name="proj",
    )(x, g, wlat, wloc, qg, wuq, kvg, wukv, cos, sin)


MLA_STRIP = 64


def _mla_kernel(q_ref, k_ref, v_ref, o_ref, s0_ref, s1_ref, p0_ref, p1_ref, a0_ref, a1_ref, acc_ref, m_ref,
                *, tq, tk):
    nk = SEQ // tk
    n_sub = q_ref.shape[0] // tq
    total = n_sub * nk
    s_refs, p_refs, a_refs = (s0_ref, s1_ref), (p0_ref, p1_ref), (a0_ref, a1_ref)

    def sub_rows(u):
        return slice((u // nk) * tq, (u // nk + 1) * tq)

    def chunk(u):
        return slice((u % nk) * tk, (u % nk + 1) * tk)

    def scores(u):
        s_refs[u % 2][...] = _dot_nt(q_ref[sub_rows(u), :], k_ref[chunk(u), :])

    def softmax(u):
        s_ref, p_ref, a_ref = s_refs[u % 2], p_refs[u % 2], a_refs[u % 2]
        base = (u // nk) * tq
        for r in range(tq // MLA_STRIP):
            rows = slice(r * MLA_STRIP, (r + 1) * MLA_STRIP)
            mrows = slice(base + r * MLA_STRIP, base + (r + 1) * MLA_STRIP)
            tiles = [s_ref[rows, c * LANES:(c + 1) * LANES] for c in range(tk // LANES)]
            tile_max = functools.reduce(jnp.maximum, tiles)
            m_old = m_ref[mrows, :]
            m_new = jnp.maximum(m_old, jnp.max(tile_max, axis=-1, keepdims=True))
            m_ref[mrows, :] = m_new
            a_ref[rows, :] = jnp.exp2(m_old - m_new)
            for c, s in enumerate(tiles):
                p_ref[rows, c * LANES:(c + 1) * LANES] = jnp.exp2(s - m_new).astype(BF16)

    def accumulate(u):
        rows = sub_rows(u)
        pv = _dot(p_refs[u % 2][...], v_ref[chunk(u), :])
        alpha = a_refs[u % 2][...]
        acc = [alpha * acc_ref[rows, c * LANES:(c + 1) * LANES] + pv[:, c * LANES:(c + 1) * LANES]
               for c in range(V_PAD // LANES)]
        if u % nk == nk - 1:
            o_ref[rows, :] = acc[0] / acc[1][:, 0:1]
        else:
            for c in range(V_PAD // LANES):
                acc_ref[rows, c * LANES:(c + 1) * LANES] = acc[c]

    m_ref[...] = jnp.full(m_ref.shape, -jnp.inf, F32)
    acc_ref[...] = jnp.zeros(acc_ref.shape, F32)
    for t in range(total + 2):
        if 1 <= t <= total:
            softmax(t - 1)
        if t < total:
            scores(t)
        if t >= 2:
            accumulate(t - 2)


def _mla(q, k, v, tq=512, tk=512, n_sub=2):
    tb = n_sub * tq
    return pl.pallas_call(
        functools.partial(_mla_kernel, tq=tq, tk=tk),
        grid=(HEADS, SEQ // tb),
        in_specs=[pl.BlockSpec((None, tb, QK_PAD), lambda h, i: (h, i, 0)),
                  pl.BlockSpec((None, SEQ, QK_PAD), lambda h, i: (h, 0, 0)),
                  pl.BlockSpec((None, SEQ, V_PAD), lambda h, i: (h, 0, 0))],
        out_specs=pl.BlockSpec((tb, MLA_V), lambda h, i: (i, h)),
        out_shape=jax.ShapeDtypeStruct((SEQ, MLA_WIDTH), F32),
        scratch_shapes=[pltpu.VMEM((tq, tk), F32), pltpu.VMEM((tq, tk), F32),
                        pltpu.VMEM((tq, tk), BF16), pltpu.VMEM((tq, tk), BF16),
                        pltpu.VMEM((tq, LANES), F32), pltpu.VMEM((tq, LANES), F32),
                        pltpu.VMEM((tb, V_PAD), F32), pltpu.VMEM((tb, LANES), F32)],
        compiler_params=pltpu.CompilerParams(dimension_semantics=("arbitrary", "arbitrary"),
                                             vmem_limit_bytes=VMEM_LIMIT),
        name="mla",
    )(q, k, v)


NA_QROWS = 4
NA_KROWS = 12
NA_TQ = NA_QROWS * GRID_W
NA_TK = NA_KROWS * GRID_W
NA_STEPS = GRID_ROWS // NA_QROWS
NA_BIAS_ROWS = 2 * NA_ROWS - 1
NA_BIAS_COLS = 2 * NA_COLS - 1
NA_STRIP = 32


def _na_bias_plan(variant):
    plan = {}
    for a in range(NA_QROWS):
        for b in range(NA_KROWS):
            if variant == "first":
                dr, ok = b - a + NA_ROWS - 1, b < NA_ROWS
            elif variant == "last":
                dr, ok = b - a - 1, b >= NA_KROWS - NA_ROWS
            else:
                dr, ok = b - a + NA_ROWS // 2 - 1, a <= b < a + NA_ROWS
            plan[(a, b)] = dr if ok else None
    return plan


def _na_kernel(rpb_ref, q_ref, k0_ref, k1_ref, k2_ref, v0_ref, v1_ref, v2_ref, o_ref, t_ref, bias_ref,
               s0_ref, s1_ref, p0_ref, p1_ref, l0_ref, l1_ref):
    g = pl.program_id(0)
    s_refs, p_refs, l_refs = (s0_ref, s1_ref), (p0_ref, p1_ref), (l0_ref, l1_ref)

    @pl.when(g == 0)
    def _():
        c = lax.broadcasted_iota(jnp.int32, (GRID_W, GRID_W), 0)
        kc = lax.broadcasted_iota(jnp.int32, (GRID_W, GRID_W), 1)
        start = jnp.clip(c - NA_COLS // 2, 0, GRID_W - NA_COLS)
        dc = jnp.where((kc >= start) & (kc < start + NA_COLS), kc - c + NA_COLS - 1, -1)

        def per_row(i, carry):
            tile = jnp.full((GRID_W, GRID_W), NEG_BIG, F32)
            for d in range(NA_BIAS_COLS):
                tile = jnp.where(dc == d, rpb_ref[i * NA_BIAS_COLS + d] * LOG2_E, tile)
            t_ref[i] = tile
            return carry

        lax.fori_loop(0, HEADS * NA_BIAS_ROWS, per_row, 0)

    def build(variant):
        plan = _na_bias_plan(variant)

        def per_head(h, carry):
            for (a, b), dr in plan.items():
                if dr is None:
                    tile = jnp.full((GRID_W, GRID_W), NEG_BIG, F32)
                else:
                    tile = t_ref[h * NA_BIAS_ROWS + dr]
                bias_ref[h, a * GRID_W:(a + 1) * GRID_W, b * GRID_W:(b + 1) * GRID_W] = tile
            return carry

        lax.fori_loop(0, HEADS, per_head, 0)

    pl.when(g == 0)(lambda: build("first"))
    pl.when(g == 1)(lambda: build("mid"))
    pl.when(g == NA_STEPS - 1)(lambda: build("last"))

    k_refs = (k0_ref, k1_ref, k2_ref)
    v_refs = (v0_ref, v1_ref, v2_ref)
    for h in range(HEADS):
        cols = slice(h * NA_HEAD_DIM, (h + 1) * NA_HEAD_DIM)
        s_ref, p_ref, l_ref = s_refs[h % 2], p_refs[h % 2], l_refs[h % 2]
        qh = q_ref[:, cols]
        for t in range(3):
            s_ref[:, t * NA_TQ:(t + 1) * NA_TQ] = _dot_nt(qh, k_refs[t][:, cols])
        for r in range(NA_TQ // NA_STRIP):
            rows = slice(r * NA_STRIP, (r + 1) * NA_STRIP)
            s = s_ref[rows, :] + bias_ref[h, rows, :]
            p = jnp.exp2(s - jnp.max(s, axis=-1, keepdims=True))
            l_ref[rows, :] = jnp.broadcast_to(jnp.sum(p, axis=-1, keepdims=True), (NA_STRIP, LANES))
            p_ref[rows, :] = p.astype(BF16)
        o = _dot(p_ref[:, 0:NA_TQ], v_refs[0][:, cols])
        for t in (1, 2):
            o = o + _dot(p_ref[:, t * NA_TQ:(t + 1) * NA_TQ], v_refs[t][:, cols])
        o_ref[:, cols] = o / l_ref[...]


def _na(rpb, nq, nk, nv):
    first_blk = lambda g: jnp.clip(g - 1, 0, NA_STEPS - 3)
    kv_specs = [pl.BlockSpec((NA_TQ, NA_WIDTH), functools.partial(lambda g, t: (first_blk(g) + t, 0), t=t))
                for t in range(3)]
    return pl.pallas_call(
        _na_kernel,
        grid=(NA_STEPS,),
        in_specs=[pl.BlockSpec(memory_space=pltpu.SMEM), pl.BlockSpec((NA_TQ, NA_WIDTH), lambda g: (g, 0))]
                 + kv_specs + kv_specs,
        out_specs=pl.BlockSpec((NA_TQ, NA_WIDTH), lambda g: (g, 0)),
        out_shape=jax.ShapeDtypeStruct((SEQ, NA_WIDTH), F32),
        scratch_shapes=[pltpu.VMEM((HEADS * NA_BIAS_ROWS, GRID_W, GRID_W), F32),
                        pltpu.VMEM((HEADS, NA_TQ, NA_TK), F32),
                        pltpu.VMEM((NA_TQ, NA_TK), F32), pltpu.VMEM((NA_TQ, NA_TK), F32),
                        pltpu.VMEM((NA_TQ, NA_TK), BF16), pltpu.VMEM((NA_TQ, NA_TK), BF16),
                        pltpu.VMEM((NA_TQ, LANES), F32), pltpu.VMEM((NA_TQ, LANES), F32)],
        compiler_params=pltpu.CompilerParams(dimension_semantics=("arbitrary",),
                                             vmem_limit_bytes=VMEM_LIMIT),
        name="na",
    )(rpb.astype(F32).reshape(-1), nq, nk, nk, nk, nv, nv, nv)


def _outproj_kernel(a_ref, n_ref, ga_ref, gn_ref, w_ref, x_ref, o_ref, wb_ref):
    @pl.when(pl.program_id(0) == 0)
    def _():
        wb_ref[...] = w_ref[...].astype(BF16)

    an = _rms(a_ref[...], ga_ref[...]).astype(BF16)
    nn = _rms(n_ref[...], gn_ref[...]).astype(BF16)
    o_ref[...] = x_ref[...] + _dot(an, wb_ref[0:MLA_WIDTH, :]) + _dot(nn, wb_ref[MLA_WIDTH:, :])


def _outproj(a, n, ga, gn, w, x, tm=512):
    row = lambda w_: pl.BlockSpec((tm, w_), lambda i: (i, 0))
    return pl.pallas_call(
        _outproj_kernel,
        grid=(SEQ // tm,),
        in_specs=[row(MLA_WIDTH), row(NA_WIDTH), _const_spec((1, MLA_WIDTH)), _const_spec((1, NA_WIDTH)),
                  _const_spec((MLA_WIDTH + NA_WIDTH, D_MODEL)), row(D_MODEL)],
        out_specs=row(D_MODEL),
        out_shape=jax.ShapeDtypeStruct((SEQ, D_MODEL), F32),
        scratch_shapes=[pltpu.VMEM((MLA_WIDTH + NA_WIDTH, D_MODEL), BF16)],
        compiler_params=pltpu.CompilerParams(dimension_semantics=("arbitrary",),
                                             vmem_limit_bytes=VMEM_LIMIT),
        name="outproj",
    )(a, n, ga, gn, w, x)


def _mlp_kernel(x_ref, g_ref, w1_first_ref, w1_next_ref, w2_ref, gf_ref, o_ref, h_ref, acc_ref,
                hid0_ref, hid1_ref, *, n_ff):
    j = pl.program_id(1)
    hid_refs = (hid0_ref, hid1_ref)

    def up(w_ref, dst_ref):
        a = jnp.maximum(_dot(h_ref[...], w_ref[...]), 0.0)
        dst_ref[...] = (a * a).astype(BF16)

    @pl.when(j == 0)
    def _():
        x = x_ref[...]
        h_ref[...] = _rms(x, g_ref[...]).astype(BF16)
        acc_ref[...] = x
        up(w1_first_ref, hid0_ref)

    for slot in (0, 1):
        @pl.when((j % 2 == slot) & (j < n_ff - 1))
        def _():
            up(w1_next_ref, hid_refs[1 - slot])
            acc_ref[...] += _dot(hid_refs[slot][...], w2_ref[...])

    @pl.when(j == n_ff - 1)
    def _():
        y = acc_ref[...] + _dot(hid_refs[(n_ff - 1) % 2][...], w2_ref[...])
        o_ref[...] = _rms(y, gf_ref[...])


def _mlp(x, g, w1, w2, gf, tm=512, tf=1024):
    n_ff = D_FF // tf
    return pl.pallas_call(
        functools.partial(_mlp_kernel, n_ff=n_ff),
        grid=(SEQ // tm, n_ff),
        in_specs=[pl.BlockSpec((tm, D_MODEL), lambda i, j: (i, 0)),
                  _const_spec((1, D_MODEL)),
                  _const_spec((D_MODEL, tf)),
                  pl.BlockSpec((D_MODEL, tf), lambda i, j: (0, jnp.minimum(j + 1, n_ff - 1))),
                  pl.BlockSpec((tf, D_MODEL), lambda i, j: (j, 0)),
                  _const_spec((1, D_MODEL))],
        out_specs=pl.BlockSpec((tm, D_MODEL), lambda i, j: (i, 0)),
        out_shape=jax.ShapeDtypeStruct((SEQ, D_MODEL), F32),
        scratch_shapes=[pltpu.VMEM((tm, D_MODEL), BF16), pltpu.VMEM((tm, D_MODEL), F32),
                        pltpu.VMEM((tm, tf), BF16), pltpu.VMEM((tm, tf), BF16)],
        compiler_params=pltpu.CompilerParams(dimension_semantics=("arbitrary", "arbitrary"),
                                             vmem_limit_bytes=VMEM_LIMIT),
        name="mlp",
    )(x, g, w1, w1, w2, gf)


def _rope_tables():
    half = MLA_ROPE // 2
    inv = ROPE_THETA ** (-jnp.arange(half, dtype=F32) / half)
    ang = jnp.arange(SEQ).astype(F32)[:, None] * inv[None, :]
    c, s = jnp.cos(ang), jnp.sin(ang)
    reps = LANES // MLA_ROPE
    return jnp.tile(jnp.concatenate([c, c], axis=1), (1, reps)), jnp.tile(jnp.concatenate([-s, s], axis=1), (1, reps))


def kernel(x, attn_norm_g, w_in, q_norm_g, w_uq, kv_norm_g, w_ukv, na_rpb, mla_out_norm_g, na_out_norm_g,
           w_out, mlp_norm_g, w_ff1, w_ff2, final_norm_g):
    assert x.shape == (1, SEQ, D_MODEL)
    assert w_in.shape[0] == 1, "the final norm is fused into the (single) layer's MLP kernel"
    cos, sin = _rope_tables()
    xs = x[0]
    for l in range(1):
        wlat = w_in[l][:, :LAT_COLS].astype(BF16)
        wloc = w_in[l][:, LAT_COLS:].astype(BF16)
        wq = w_uq[l].reshape(MLA_Q_RANK, HEADS, MLA_QK)
        wuq = jnp.concatenate([wq[:, :, :MLA_NOPE].reshape(MLA_Q_RANK, HEADS * MLA_NOPE),
                               wq[:, :, MLA_NOPE:].reshape(MLA_Q_RANK, HEADS * MLA_ROPE)], axis=1).astype(BF16)
        wk = w_ukv[l].reshape(MLA_KV_RANK, HEADS, MLA_NOPE + MLA_V)
        wukv = jnp.concatenate([wk[:, :, :MLA_NOPE].reshape(MLA_KV_RANK, HEADS * MLA_NOPE),
                                wk[:, :, MLA_NOPE:].reshape(MLA_KV_RANK, HEADS * MLA_V)], axis=1).astype(BF16)

        q, k, v, nq, nk, nv = _proj(xs, attn_norm_g[l][None], wlat, wloc, q_norm_g[l][None], wuq,
                                    kv_norm_g[l][None], wukv, cos, sin)
        a_out = _mla(q, k, v)
        n_out = _na(na_rpb[l], nq, nk, nv)
        xs = _outproj(a_out, n_out, mla_out_norm_g[l][None], na_out_norm_g[l][None],
                      w_out[l], xs)
        xs = _mlp(xs, mlp_norm_g[l][None], w_ff1[l].astype(BF16), w_ff2[l].astype(BF16), final_norm_g[None])
    return xs[None]
```

```python
import functools

import jax
import jax.numpy as jnp
from jax import lax
from jax.experimental import pallas as pl
from jax.experimental.pallas import tpu as pltpu

F32 = jnp.float32
BF16 = jnp.bfloat16

D_MODEL = 2048
SEQ = 8192
GRID_W = 64
GRID_ROWS = SEQ // GRID_W
HEADS = 8
MLA_NOPE = 128
MLA_ROPE = 64
MLA_V = 128
MLA_QK = MLA_NOPE + MLA_ROPE
MLA_Q_RANK = 512
MLA_KV_RANK = 256
ROPE_THETA = 10000.0
NA_HEAD_DIM = 128
NA_ROWS = 8
NA_COLS = 16
NA_WIDTH = HEADS * NA_HEAD_DIM
MLA_WIDTH = HEADS * MLA_V
D_FF = 4 * D_MODEL
NORM_EPS = 1e-6

LANES = 128
QK_PAD = 2 * LANES
V_PAD = 2 * LANES
LOG2_E = 1.4426950408889634
NEG_BIG = -1e30

LAT_COLS = MLA_Q_RANK + MLA_KV_RANK
LOC_COLS = MLA_ROPE + 3 * NA_WIDTH
_C_NQ = 0
_C_NK = _C_NQ + NA_WIDTH
_C_NV = _C_NK + NA_WIDTH
_C_KR = _C_NV + NA_WIDTH
LOC_COLS_ALIGNED = _C_KR + 2 * MLA_ROPE
WIN_ROWS = 256

VMEM_LIMIT = 56 * 1024 * 1024


def _rms(xf, g):
    y = xf * lax.rsqrt(jnp.mean(xf * xf, axis=-1, keepdims=True) + NORM_EPS)
    return y * g


def _dot(a, b):
    return jnp.dot(a, b, preferred_element_type=F32)


def _dot_nt(a, b):
    return lax.dot_general(a, b, (((1,), (1,)), ((), ())), preferred_element_type=F32)


def _const_spec(shape):
    nd = len(shape)
    return pl.BlockSpec(shape, lambda *_: (0,) * nd, pipeline_mode=pl.Buffered(1))


def _rope_lanes(x, c, s, lo_half):
    partner = jnp.where(lo_half, pltpu.roll(x, LANES - MLA_ROPE // 2, 1), pltpu.roll(x, MLA_ROPE // 2, 1))
    return x * c + partner * s


def _win_kernel(w_ref, wlat_ref, wal_ref):
    w = w_ref[...]
    wlat_ref[...] = w[:, :LAT_COLS].astype(BF16)
    loc = w[:, LAT_COLS:].astype(BF16)
    wal_ref[:, 0:_C_KR] = loc[:, MLA_ROPE:]
    wal_ref[:, _C_KR:] = jnp.concatenate([loc[:, :MLA_ROPE], loc[:, :MLA_ROPE]], axis=1)


def _win(w):
    rows = lambda c: pl.BlockSpec((WIN_ROWS, c), lambda i: (i, 0))
    return pl.pallas_call(
        _win_kernel,
        grid=(D_MODEL // WIN_ROWS,),
        in_specs=[rows(LAT_COLS + LOC_COLS)],
        out_specs=[rows(LAT_COLS), rows(LOC_COLS_ALIGNED)],
        out_shape=[jax.ShapeDtypeStruct((D_MODEL, LAT_COLS), BF16),
                   jax.ShapeDtypeStruct((D_MODEL, LOC_COLS_ALIGNED), BF16)],
        compiler_params=pltpu.CompilerParams(dimension_semantics=("arbitrary",),
                                             vmem_limit_bytes=VMEM_LIMIT),
        name="win",
    )(w)


def _proj_kernel(x_ref, g_ref, wlat_ref, wal_ref, qg_ref, wuq_ref, kvg_ref, wukv_ref, cos_ref, sin_ref,
                 q_ref, k_ref, v_ref, nq_ref, nk_ref, nv_ref):
    tm = x_ref.shape[0]
    h = _rms(x_ref[...], g_ref[...]).astype(BF16)

    def proj(lo, hi):
        return _dot(h, wal_ref[:, lo:hi])

    nq_ref[...] = (proj(_C_NQ, _C_NK) * (LOG2_E * NA_HEAD_DIM ** -0.5)).astype(BF16)
    nk_ref[...] = proj(_C_NK, _C_NV).astype(BF16)
    nv_ref[...] = proj(_C_NV, _C_KR).astype(BF16)

    cos = cos_ref[...]
    sin = sin_ref[...]
    lane = lax.broadcasted_iota(jnp.int32, (tm, LANES), 1)
    lo_half = (lane % MLA_ROPE) < (MLA_ROPE // 2)

    cqn = _rms(_dot(h, wlat_ref[:, :MLA_Q_RANK]), qg_ref[...]).astype(BF16)
    q = _dot(cqn, wuq_ref[...]) * (LOG2_E * MLA_QK ** -0.5)
    rope0 = HEADS * MLA_NOPE
    for pair in range(HEADS // 2):
        qr = _rope_lanes(q[:, rope0 + pair * LANES: rope0 + (pair + 1) * LANES], cos, sin, lo_half)
        qr = qr.astype(BF16)
        for hh in (2 * pair, 2 * pair + 1):
            q_ref[hh, :, 0:LANES] = q[:, hh * MLA_NOPE:(hh + 1) * MLA_NOPE].astype(BF16)
            q_ref[hh, :, LANES:QK_PAD] = qr

    ckvn = _rms(_dot(h, wlat_ref[:, MLA_Q_RANK:]), kvg_ref[...]).astype(BF16)
    kv = _dot(ckvn, wukv_ref[...])
    kr = _rope_lanes(proj(_C_KR, LOC_COLS_ALIGNED), cos, sin, lo_half)
    kr_even = jnp.where(lane < MLA_ROPE, kr, 0.0).astype(BF16)
    kr_odd = jnp.where(lane >= MLA_ROPE, kr, 0.0).astype(BF16)
    v0 = HEADS * MLA_NOPE
    ones_col = jnp.where(lane == 0, 1.0, 0.0).astype(BF16)
    for hh in range(HEADS):
        k_ref[hh, :, 0:LANES] = kv[:, hh * MLA_NOPE:(hh + 1) * MLA_NOPE].astype(BF16)
        k_ref[hh, :, LANES:QK_PAD] = kr_even if hh % 2 == 0 else kr_odd
        v_ref[hh, :, 0:MLA_V] = kv[:, v0 + hh * MLA_V: v0 + (hh + 1) * MLA_V].astype(BF16)
        v_ref[hh, :, MLA_V:V_PAD] = ones_col


def _proj(x, g, wlat, wal, qg, wuq, kvg, wukv, cos, sin, tm=512):
    row = lambda w: pl.BlockSpec((tm, w), lambda i: (i, 0))
    head = lambda w: pl.BlockSpec((HEADS, tm, w), lambda i: (0, i, 0))
    return pl.pallas_call(
        _proj_kernel,
        grid=(SEQ // tm,),
        in_specs=[row(D_MODEL), _const_spec((1, D_MODEL)), _const_spec((D_MODEL, LAT_COLS)),
                  _const_spec((D_MODEL, LOC_COLS_ALIGNED)),
                  _const_spec((1, MLA_Q_RANK)), _const_spec((MLA_Q_RANK, HEADS * MLA_QK)),
                  _const_spec((1, MLA_KV_RANK)), _const_spec((MLA_KV_RANK, HEADS * (MLA_NOPE + MLA_V))),
                  row(LANES), row(LANES)],
        out_specs=[head(QK_PAD), head(QK_PAD), head(V_PAD), row(NA_WIDTH), row(NA_WIDTH), row(NA_WIDTH)],
        out_shape=[jax.ShapeDtypeStruct((HEADS, SEQ, QK_PAD), BF16),
                   jax.ShapeDtypeStruct((HEADS, SEQ, QK_PAD), BF16),
                   jax.ShapeDtypeStruct((HEADS, SEQ, V_PAD), BF16),
                   jax.ShapeDtypeStruct((SEQ, NA_WIDTH), BF16),
                   jax.ShapeDtypeStruct((SEQ, NA_WIDTH), BF16),
                   jax.ShapeDtypeStruct((SEQ, NA_WIDTH), BF16)],
        compiler_params=pltpu.CompilerParams(dimension_semantics=("arbitrary",),
                                             vmem_limit_bytes=VMEM_LIMIT),
        name="proj",
    )(x, g, wlat, wal, qg, wuq, kvg, wukv, cos, sin)


MLA_STRIP = 64


def _mla_kernel(q_ref, k_ref, v_ref, w1_ref, w2_ref, o_ref, w1b_ref, w2b_ref,
                s0_ref, s1_ref, p0_ref, p1_ref, a0_ref, a1_ref, acc_ref, m_ref, *, tq, tk):
    w1b_ref[...] = w1_ref[...].astype(BF16)
    w2b_ref[...] = w2_ref[...].astype(BF16)
    nk = SEQ // tk
    n_sub = q_ref.shape[0] // tq
    total = n_sub * nk
    s_refs, p_refs, a_refs = (s0_ref, s1_ref), (p0_ref, p1_ref), (a0_ref, a1_ref)

    def sub_rows(u):
        return slice((u // nk) * tq, (u // nk + 1) * tq)

    def chunk(u):
        return slice((u % nk) * tk, (u % nk + 1) * tk)

    def scores(u):
        s_refs[u % 2][...] = _dot_nt(q_ref[sub_rows(u), :], k_ref[chunk(u), :])

    def softmax(u):
        s_ref, p_ref, a_ref = s_refs[u % 2], p_refs[u % 2], a_refs[u % 2]
        base = (u // nk) * tq
        for r in range(tq // MLA_STRIP):
            rows = slice(r * MLA_STRIP, (r + 1) * MLA_STRIP)
            mrows = slice(base + r * MLA_STRIP, base + (r + 1) * MLA_STRIP)
            tiles = [s_ref[rows, c * LANES:(c + 1) * LANES] for c in range(tk // LANES)]
            tile_max = functools.reduce(jnp.maximum, tiles)
            m_old = m_ref[mrows, :]
            m_new = jnp.maximum(m_old, jnp.max(tile_max, axis=-1, keepdims=True))
            m_ref[mrows, :] = m_new
            a_ref[rows, :] = jnp.exp2(m_old - m_new)
            for c, s in enumerate(tiles):
                p_ref[rows, c * LANES:(c + 1) * LANES] = jnp.exp2(s - m_new).astype(BF16)

    def accumulate(u):
        rows = sub_rows(u)
        pv = _dot(p_refs[u % 2][...], v_ref[chunk(u), :])
        alpha = a_refs[u % 2][...]
        acc = [alpha * acc_ref[rows, c * LANES:(c + 1) * LANES] + pv[:, c * LANES:(c + 1) * LANES]
               for c in range(V_PAD // LANES)]
        if u % nk == nk - 1:
            o_ref[rows, :] = acc[0] / acc[1][:, 0:1]
        else:
            for c in range(V_PAD // LANES):
                acc_ref[rows, c * LANES:(c + 1) * LANES] = acc[c]

    m_ref[...] = jnp.full(m_ref.shape, -jnp.inf, F32)
    acc_ref[...] = jnp.zeros(acc_ref.shape, F32)
    for t in range(total + 2):
        if 1 <= t <= total:
            softmax(t - 1)
        if t < total:
            scores(t)
        if t >= 2:
            accumulate(t - 2)


def _mla(q, k, v, w1, w2, tq=512, tk=512, n_sub=2):
    tb = n_sub * tq
    nb = SEQ // tb
    ff_slice = D_FF // (HEADS * nb)
    assert ff_slice % LANES == 0
    return pl.pallas_call(
        functools.partial(_mla_kernel, tq=tq, tk=tk),
        grid=(HEADS, nb),
        in_specs=[pl.BlockSpec((None, tb, QK_PAD), lambda h, i: (h, i, 0)),
                  pl.BlockSpec((None, SEQ, QK_PAD), lambda h, i: (h, 0, 0)),
                  pl.BlockSpec((None, SEQ, V_PAD), lambda h, i: (h, 0, 0)),
                  pl.BlockSpec((D_MODEL, ff_slice), lambda h, i: (0, h * nb + i)),
                  pl.BlockSpec((ff_slice, D_MODEL), lambda h, i: (h * nb + i, 0))],
        out_specs=[pl.BlockSpec((tb, MLA_V), lambda h, i: (i, h)),
                   pl.BlockSpec((D_MODEL, ff_slice), lambda h, i: (0, h * nb + i)),
                   pl.BlockSpec((ff_slice, D_MODEL), lambda h, i: (h * nb + i, 0))],
        out_shape=[jax.ShapeDtypeStruct((SEQ, MLA_WIDTH), F32),
                   jax.ShapeDtypeStruct((D_MODEL, D_FF), BF16),
                   jax.ShapeDtypeStruct((D_FF, D_MODEL), BF16)],
        scratch_shapes=[pltpu.VMEM((tq, tk), F32), pltpu.VMEM((tq, tk), F32),
                        pltpu.VMEM((tq, tk), BF16), pltpu.VMEM((tq, tk), BF16),
                        pltpu.VMEM((tq, LANES), F32), pltpu.VMEM((tq, LANES), F32),
                        pltpu.VMEM((tb, V_PAD), F32), pltpu.VMEM((tb, LANES), F32)],
        compiler_params=pltpu.CompilerParams(dimension_semantics=("arbitrary", "arbitrary"),
                                             vmem_limit_bytes=VMEM_LIMIT),
        name="mla",
    )(q, k, v, w1, w2)


NA_QROWS = 4
NA_KROWS = 12
NA_TQ = NA_QROWS * GRID_W
NA_TK = NA_KROWS * GRID_W
NA_STEPS = GRID_ROWS // NA_QROWS
NA_BIAS_ROWS = 2 * NA_ROWS - 1
NA_BIAS_COLS = 2 * NA_COLS - 1
NA_STRIP = 32


def _na_bias_plan(variant):
    plan = {}
    for a in range(NA_QROWS):
        for b in range(NA_KROWS):
            if variant == "first":
                dr, ok = b - a + NA_ROWS - 1, b < NA_ROWS
            elif variant == "last":
                dr, ok = b - a - 1, b >= NA_KROWS - NA_ROWS
            else:
                dr, ok = b - a + NA_ROWS // 2 - 1, a <= b < a + NA_ROWS
            plan[(a, b)] = dr if ok else None
    return plan


def _na_kernel(rpb_ref, q_ref, k0_ref, k1_ref, k2_ref, v0_ref, v1_ref, v2_ref, o_ref, t_ref, bias_ref,
               s0_ref, s1_ref, p0_ref, p1_ref, l0_ref, l1_ref):
    g = pl.program_id(0)
    s_refs, p_refs, l_refs = (s0_ref, s1_ref), (p0_ref, p1_ref), (l0_ref, l1_ref)

    @pl.when(g == 0)
    def _():
        c = lax.broadcasted_iota(jnp.int32, (GRID_W, GRID_W), 0)
        kc = lax.broadcasted_iota(jnp.int32, (GRID_W, GRID_W), 1)
        start = jnp.clip(c - NA_COLS // 2, 0, GRID_W - NA_COLS)
        dc = jnp.where((kc >= start) & (kc < start + NA_COLS), kc - c + NA_COLS - 1, -1)

        def per_row(i, carry):
            tile = jnp.full((GRID_W, GRID_W), NEG_BIG, F32)
            for d in range(NA_BIAS_COLS):
                tile = jnp.where(dc == d, rpb_ref[i * NA_BIAS_COLS + d] * LOG2_E, tile)
            t_ref[i] = tile
            return carry

        lax.fori_loop(0, HEADS * NA_BIAS_ROWS, per_row, 0)

    def build(variant):
        plan = _na_bias_plan(variant)

        def per_head(h, carry):
            for (a, b), dr in plan.items():
                if dr is None:
                    tile = jnp.full((GRID_W, GRID_W), NEG_BIG, F32)
                else:
                    tile = t_ref[h * NA_BIAS_ROWS + dr]
                bias_ref[h, a * GRID_W:(a + 1) * GRID_W, b * GRID_W:(b + 1) * GRID_W] = tile
            return carry

        lax.fori_loop(0, HEADS, per_head, 0)

    pl.when(g == 0)(lambda: build("first"))
    pl.when(g == 1)(lambda: build("mid"))
    pl.when(g == NA_STEPS - 1)(lambda: build("last"))

    k_refs = (k0_ref, k1_ref, k2_ref)
    v_refs = (v0_ref, v1_ref, v2_ref)
    for h in range(HEADS):
        cols = slice(h * NA_HEAD_DIM, (h + 1) * NA_HEAD_DIM)
        s_ref, p_ref, l_ref = s_refs[h % 2], p_refs[h % 2], l_refs[h % 2]
        qh = q_ref[:, cols]
        for t in range(3):
            s_ref[:, t * NA_TQ:(t + 1) * NA_TQ] = _dot_nt(qh, k_refs[t][:, cols])
        for r in range(NA_TQ // NA_STRIP):
            rows = slice(r * NA_STRIP, (r + 1) * NA_STRIP)
            s = s_ref[rows, :] + bias_ref[h, rows, :]
            p = jnp.exp2(s - jnp.max(s, axis=-1, keepdims=True))
            l_ref[rows, :] = jnp.broadcast_to(jnp.sum(p, axis=-1, keepdims=True), (NA_STRIP, LANES))
            p_ref[rows, :] = p.astype(BF16)
        o = _dot(p_ref[:, 0:NA_TQ], v_refs[0][:, cols])
        for t in (1, 2):
            o = o + _dot(p_ref[:, t * NA_TQ:(t + 1) * NA_TQ], v_refs[t][:, cols])
        o_ref[:, cols] = o / l_ref[...]


def _na(rpb, nq, nk, nv):
    first_blk = lambda g: jnp.clip(g - 1, 0, NA_STEPS - 3)
    kv_specs = [pl.BlockSpec((NA_TQ, NA_WIDTH), functools.partial(lambda g, t: (first_blk(g) + t, 0), t=t))
                for t in range(3)]
    return pl.pallas_call(
        _na_kernel,
        grid=(NA_STEPS,),
        in_specs=[pl.BlockSpec(memory_space=pltpu.SMEM), pl.BlockSpec((NA_TQ, NA_WIDTH), lambda g: (g, 0))]
                 + kv_specs + kv_specs,
        out_specs=pl.BlockSpec((NA_TQ, NA_WIDTH), lambda g: (g, 0)),
        out_shape=jax.ShapeDtypeStruct((SEQ, NA_WIDTH), F32),
        scratch_shapes=[pltpu.VMEM((HEADS * NA_BIAS_ROWS, GRID_W, GRID_W), F32),
                        pltpu.VMEM((HEADS, NA_TQ, NA_TK), F32),
                        pltpu.VMEM((NA_TQ, NA_TK), F32), pltpu.VMEM((NA_TQ, NA_TK), F32),
                        pltpu.VMEM((NA_TQ, NA_TK), BF16), pltpu.VMEM((NA_TQ, NA_TK), BF16),
                        pltpu.VMEM((NA_TQ, LANES), F32), pltpu.VMEM((NA_TQ, LANES), F32)],
        compiler_params=pltpu.CompilerParams(dimension_semantics=("arbitrary",),
                                             vmem_limit_bytes=VMEM_LIMIT),
        name="na",
    )(rpb.astype(F32).reshape(-1), nq, nk, nk, nk, nv, nv, nv)


def _outproj_kernel(a_ref, n_ref, ga_ref, gn_ref, w_ref, x_ref, o_ref, wb_ref):
    @pl.when(pl.program_id(0) == 0)
    def _():
        wb_ref[...] = w_ref[...].astype(BF16)

    an = _rms(a_ref[...], ga_ref[...]).astype(BF16)
    nn = _rms(n_ref[...], gn_ref[...]).astype(BF16)
    o_ref[...] = x_ref[...] + _dot(an, wb_ref[0:MLA_WIDTH, :]) + _dot(nn, wb_ref[MLA_WIDTH:, :])


def _outproj(a, n, ga, gn, w, x, tm=512):
    row = lambda w_: pl.BlockSpec((tm, w_), lambda i: (i, 0))
    return pl.pallas_call(
        _outproj_kernel,
        grid=(SEQ // tm,),
        in_specs=[row(MLA_WIDTH), row(NA_WIDTH), _const_spec((1, MLA_WIDTH)), _const_spec((1, NA_WIDTH)),
                  _const_spec((MLA_WIDTH + NA_WIDTH, D_MODEL)), row(D_MODEL)],
        out_specs=row(D_MODEL),
        out_shape=jax.ShapeDtypeStruct((SEQ, D_MODEL), F32),
        scratch_shapes=[pltpu.VMEM((MLA_WIDTH + NA_WIDTH, D_MODEL), BF16)],
        compiler_params=pltpu.CompilerParams(dimension_semantics=("arbitrary",),
                                             vmem_limit_bytes=VMEM_LIMIT),
        name="outproj",
    )(a, n, ga, gn, w, x)


def _mlp_kernel(x_ref, g_ref, w1_ref, w2_ref, gf_ref, o_ref, h_ref, acc_ref):
    j = pl.program_id(1)

    @pl.when(j == 0)
    def _():
        x = x_ref[...]
        h_ref[...] = _rms(x, g_ref[...]).astype(BF16)
        acc_ref[...] = x

    hid = jnp.maximum(_dot(h_ref[...], w1_ref[...]), 0.0)
    acc_ref[...] += _dot((hid * hid).astype(BF16), w2_ref[...])

    @pl.when(j == pl.num_programs(1) - 1)
    def _():
        o_ref[...] = _rms(acc_ref[...], gf_ref[...])


def _mlp(x, g, w1, w2, gf, tm=512, tf=1024):
    return pl.pallas_call(
        _mlp_kernel,
        grid=(SEQ // tm, D_FF // tf),
        in_specs=[pl.BlockSpec((tm, D_MODEL), lambda i, j: (i, 0)),
                  _const_spec((1, D_MODEL)),
                  pl.BlockSpec((D_MODEL, tf), lambda i, j: (0, j)),
                  pl.BlockSpec((tf, D_MODEL), lambda i, j: (j, 0)),
                  _const_spec((1, D_MODEL))],
        out_specs=pl.BlockSpec((tm, D_MODEL), lambda i, j: (i, 0)),
        out_shape=jax.ShapeDtypeStruct((SEQ, D_MODEL), F32),
        scratch_shapes=[pltpu.VMEM((tm, D_MODEL), BF16), pltpu.VMEM((tm, D_MODEL), F32)],
        compiler_params=pltpu.CompilerParams(dimension_semantics=("arbitrary", "arbitrary"),
                                             vmem_limit_bytes=VMEM_LIMIT),
        name="mlp",
    )(x, g, w1, w2, gf)


def _rope_tables():
    half = MLA_ROPE // 2
    inv = ROPE_THETA ** (-jnp.arange(half, dtype=F32) / half)
    ang = jnp.arange(SEQ).astype(F32)[:, None] * inv[None, :]
    c, s = jnp.cos(ang), jnp.sin(ang)
    reps = LANES // MLA_ROPE
    return jnp.tile(jnp.concatenate([c, c], axis=1), (1, reps)), jnp.tile(jnp.concatenate([-s, s], axis=1), (1, reps))


def kernel(x, attn_norm_g, w_in, q_norm_g, w_uq, kv_norm_g, w_ukv, na_rpb, mla_out_norm_g, na_out_norm_g,
           w_out, mlp_norm_g, w_ff1, w_ff2, final_norm_g):
    assert x.shape == (1, SEQ, D_MODEL)
    assert w_in.shape[0] == 1, "the final norm is fused into the (single) layer's MLP kernel"
    cos, sin = _rope_tables()
    xs = x[0]
    for l in range(1):
        wlat, wal = _win(w_in[l])
        wq = w_uq[l].reshape(MLA_Q_RANK, HEADS, MLA_QK)
        wuq = jnp.concatenate([wq[:, :, :MLA_NOPE].reshape(MLA_Q_RANK, HEADS * MLA_NOPE),
                               wq[:, :, MLA_NOPE:].reshape(MLA_Q_RANK, HEADS * MLA_ROPE)], axis=1).astype(BF16)
        wk = w_ukv[l].reshape(MLA_KV_RANK, HEADS, MLA_NOPE + MLA_V)
        wukv = jnp.concatenate([wk[:, :, :MLA_NOPE].reshape(MLA_KV_RANK, HEADS * MLA_NOPE),
                                wk[:, :, MLA_NOPE:].reshape(MLA_KV_RANK, HEADS * MLA_V)], axis=1).astype(BF16)

        q, k, v, nq, nk, nv = _proj(xs, attn_norm_g[l][None], wlat, wal, q_norm_g[l][None], wuq,
                                    kv_norm_g[l][None], wukv, cos, sin)
        a_out, w1b, w2b = _mla(q, k, v, w_ff1[l], w_ff2[l])
        n_out = _na(na_rpb[l], nq, nk, nv)
        xs = _outproj(a_out, n_out, mla_out_norm_g[l][None], na_out_norm_g[l][None],
                      w_out[l], xs)
        xs = _mlp(xs, mlp_norm_g[l][None], w1b, w2b, final_norm_g[None])
    return xs[None]
```

```python
import functools

import jax
import jax.numpy as jnp
from jax import lax
from jax.experimental import pallas as pl
from jax.experimental.pallas import tpu as pltpu

F32 = jnp.float32
BF16 = jnp.bfloat16

D_MODEL = 2048
SEQ = 8192
GRID_W = 64
GRID_ROWS = SEQ // GRID_W
HEADS = 8
MLA_NOPE = 128
MLA_ROPE = 64
MLA_V = 128
MLA_QK = MLA_NOPE + MLA_ROPE
MLA_Q_RANK = 512
MLA_KV_RANK = 256
ROPE_THETA = 10000.0
NA_HEAD_DIM = 128
NA_ROWS = 8
NA_COLS = 16
NA_WIDTH = HEADS * NA_HEAD_DIM
MLA_WIDTH = HEADS * MLA_V
D_FF = 4 * D_MODEL
NORM_EPS = 1e-6

LANES = 128
QK_PAD = 2 * LANES
V_PAD = 2 * LANES
LOG2_E = 1.4426950408889634
NEG_BIG = -1e30

_R_CQ = 0
_R_CKV = _R_CQ + MLA_Q_RANK
_R_KR = _R_CKV + MLA_KV_RANK
_R_NQ = _R_KR + MLA_ROPE
_R_NK = _R_NQ + NA_WIDTH
_R_NV = _R_NK + NA_WIDTH
IN_COLS = _R_NV + NA_WIDTH
WIN_STEPS = 4

VMEM_LIMIT = 56 * 1024 * 1024


def _rms(xf, g):
    y = xf * lax.rsqrt(jnp.mean(xf * xf, axis=-1, keepdims=True) + NORM_EPS)
    return y * g


def _dot(a, b):
    return jnp.dot(a, b, preferred_element_type=F32)


def _dot_nt(a, b):
    return lax.dot_general(a, b, (((1,), (1,)), ((), ())), preferred_element_type=F32)


def _const_spec(shape):
    nd = len(shape)
    return pl.BlockSpec(shape, lambda *_: (0,) * nd, pipeline_mode=pl.Buffered(1))


def _rope_lanes(x, c, s, lo_half):
    partner = jnp.where(lo_half, pltpu.roll(x, LANES - MLA_ROPE // 2, 1), pltpu.roll(x, MLA_ROPE // 2, 1))
    return x * c + partner * s


def _win_kernel(w_ref, wb_ref):
    wb_ref[...] = w_ref[...].astype(BF16)


def _win(wt):
    rows = pl.BlockSpec((IN_COLS // WIN_STEPS, D_MODEL), lambda i: (i, 0))
    return pl.pallas_call(
        _win_kernel,
        grid=(WIN_STEPS,),
        in_specs=[rows],
        out_specs=rows,
        out_shape=jax.ShapeDtypeStruct((IN_COLS, D_MODEL), BF16),
        compiler_params=pltpu.CompilerParams(dimension_semantics=("arbitrary",),
                                             vmem_limit_bytes=VMEM_LIMIT),
        name="win",
    )(wt)


def _proj_kernel(x_ref, g_ref, wt_ref, qg_ref, wuq_ref, kvg_ref, wukv_ref, cos_ref, sin_ref,
                 q_ref, k_ref, v_ref, nq_ref, nk_ref, nv_ref):
    tm = x_ref.shape[0]
    h = _rms(x_ref[...], g_ref[...]).astype(BF16)

    def proj(lo, hi):
        return _dot_nt(h, wt_ref[lo:hi, :])

    nq_ref[...] = (proj(_R_NQ, _R_NK) * (LOG2_E * NA_HEAD_DIM ** -0.5)).astype(BF16)
    nk_ref[...] = proj(_R_NK, _R_NV).astype(BF16)
    nv_ref[...] = proj(_R_NV, IN_COLS).astype(BF16)

    cos = cos_ref[...]
    sin = sin_ref[...]
    lane = lax.broadcasted_iota(jnp.int32, (tm, LANES), 1)
    lo_half = (lane % MLA_ROPE) < (MLA_ROPE // 2)

    cqn = _rms(proj(_R_CQ, _R_CKV), qg_ref[...]).astype(BF16)
    q = _dot(cqn, wuq_ref[...]) * (LOG2_E * MLA_QK ** -0.5)
    rope0 = HEADS * MLA_NOPE
    for pair in range(HEADS // 2):
        qr = _rope_lanes(q[:, rope0 + pair * LANES: rope0 + (pair + 1) * LANES], cos, sin, lo_half)
        qr = qr.astype(BF16)
        for hh in (2 * pair, 2 * pair + 1):
            q_ref[hh, :, 0:LANES] = q[:, hh * MLA_NOPE:(hh + 1) * MLA_NOPE].astype(BF16)
            q_ref[hh, :, LANES:QK_PAD] = qr

    ckvn = _rms(proj(_R_CKV, _R_KR), kvg_ref[...]).astype(BF16)
    kv = _dot(ckvn, wukv_ref[...])
    wkr = wt_ref[_R_KR:_R_NQ, :]
    kr = _rope_lanes(_dot_nt(h, jnp.concatenate([wkr, wkr], axis=0)), cos, sin, lo_half)
    kr_even = jnp.where(lane < MLA_ROPE, kr, 0.0).astype(BF16)
    kr_odd = jnp.where(lane >= MLA_ROPE, kr, 0.0).astype(BF16)
    v0 = HEADS * MLA_NOPE
    ones_col = jnp.where(lane == 0, 1.0, 0.0).astype(BF16)
    for hh in range(HEADS):
        k_ref[hh, :, 0:LANES] = kv[:, hh * MLA_NOPE:(hh + 1) * MLA_NOPE].astype(BF16)
        k_ref[hh, :, LANES:QK_PAD] = kr_even if hh % 2 == 0 else kr_odd
        v_ref[hh, :, 0:MLA_V] = kv[:, v0 + hh * MLA_V: v0 + (hh + 1) * MLA_V].astype(BF16)
        v_ref[hh, :, MLA_V:V_PAD] = ones_col


def _proj(x, g, wt, qg, wuq, kvg, wukv, cos, sin, tm=512):
    row = lambda w: pl.BlockSpec((tm, w), lambda i: (i, 0))
    head = lambda w: pl.BlockSpec((HEADS, tm, w), lambda i: (0, i, 0))
    return pl.pallas_call(
        _proj_kernel,
        grid=(SEQ // tm,),
        in_specs=[row(D_MODEL), _const_spec((1, D_MODEL)), _const_spec((IN_COLS, D_MODEL)),
                  _const_spec((1, MLA_Q_RANK)), _const_spec((MLA_Q_RANK, HEADS * MLA_QK)),
                  _const_spec((1, MLA_KV_RANK)), _const_spec((MLA_KV_RANK, HEADS * (MLA_NOPE + MLA_V))),
                  row(LANES), row(LANES)],
        out_specs=[head(QK_PAD), head(QK_PAD), head(V_PAD), row(NA_WIDTH), row(NA_WIDTH), row(NA_WIDTH)],
        out_shape=[jax.ShapeDtypeStruct((HEADS, SEQ, QK_PAD), BF16),
                   jax.ShapeDtypeStruct((HEADS, SEQ, QK_PAD), BF16),
                   jax.ShapeDtypeStruct((HEADS, SEQ, V_PAD), BF16),
                   jax.ShapeDtypeStruct((SEQ, NA_WIDTH), BF16),
                   jax.ShapeDtypeStruct((SEQ, NA_WIDTH), BF16),
                   jax.ShapeDtypeStruct((SEQ, NA_WIDTH), BF16)],
        compiler_params=pltpu.CompilerParams(dimension_semantics=("arbitrary",),
                                             vmem_limit_bytes=VMEM_LIMIT),
        name="proj",
    )(x, g, wt, qg, wuq, kvg, wukv, cos, sin)


MLA_STRIP = 64


def _mla_kernel(q_ref, k_ref, v_ref, w1_ref, w2_ref, o_ref, w1b_ref, w2b_ref,
                s0_ref, s1_ref, p0_ref, p1_ref, a0_ref, a1_ref, acc_ref, m_ref, *, tq, tk):
    w1b_ref[...] = w1_ref[...].astype(BF16)
    w2b_ref[...] = w2_ref[...].astype(BF16)
    nk = SEQ // tk
    n_sub = q_ref.shape[0] // tq
    total = n_sub * nk
    s_refs, p_refs, a_refs = (s0_ref, s1_ref), (p0_ref, p1_ref), (a0_ref, a1_ref)

    def sub_rows(u):
        return slice((u // nk) * tq, (u // nk + 1) * tq)

    def chunk(u):
        return slice((u % nk) * tk, (u % nk + 1) * tk)

    def scores(u):
        s_refs[u % 2][...] = _dot_nt(q_ref[sub_rows(u), :], k_ref[chunk(u), :])

    def softmax(u):
        s_ref, p_ref, a_ref = s_refs[u % 2], p_refs[u % 2], a_refs[u % 2]
        base = (u // nk) * tq
        for r in range(tq // MLA_STRIP):
            rows = slice(r * MLA_STRIP, (r + 1) * MLA_STRIP)
            mrows = slice(base + r * MLA_STRIP, base + (r + 1) * MLA_STRIP)
            tiles = [s_ref[rows, c * LANES:(c + 1) * LANES] for c in range(tk // LANES)]
            tile_max = functools.reduce(jnp.maximum, tiles)
            m_old = m_ref[mrows, :]
            m_new = jnp.maximum(m_old, jnp.max(tile_max, axis=-1, keepdims=True))
            m_ref[mrows, :] = m_new
            a_ref[rows, :] = jnp.exp2(m_old - m_new)
            for c, s in enumerate(tiles):
                p_ref[rows, c * LANES:(c + 1) * LANES] = jnp.exp2(s - m_new).astype(BF16)

    def accumulate(u):
        rows = sub_rows(u)
        pv = _dot(p_refs[u % 2][...], v_ref[chunk(u), :])
        alpha = a_refs[u % 2][...]
        acc = [alpha * acc_ref[rows, c * LANES:(c + 1) * LANES] + pv[:, c * LANES:(c + 1) * LANES]
               for c in range(V_PAD // LANES)]
        if u % nk == nk - 1:
            o_ref[rows, :] = acc[0] / acc[1][:, 0:1]
        else:
            for c in range(V_PAD // LANES):
                acc_ref[rows, c * LANES:(c + 1) * LANES] = acc[c]

    m_ref[...] = jnp.full(m_ref.shape, -jnp.inf, F32)
    acc_ref[...] = jnp.zeros(acc_ref.shape, F32)
    for t in range(total + 2):
        if 1 <= t <= total:
            softmax(t - 1)
        if t < total:
            scores(t)
        if t >= 2:
            accumulate(t - 2)


def _mla(q, k, v, w1, w2, tq=512, tk=512, n_sub=2):
    tb = n_sub * tq
    nb = SEQ // tb
    ff_slice = D_FF // (HEADS * nb)
    assert ff_slice % LANES == 0
    return pl.pallas_call(
        functools.partial(_mla_kernel, tq=tq, tk=tk),
        grid=(HEADS, nb),
        in_specs=[pl.BlockSpec((None, tb, QK_PAD), lambda h, i: (h, i, 0)),
                  pl.BlockSpec((None, SEQ, QK_PAD), lambda h, i: (h, 0, 0)),
                  pl.BlockSpec((None, SEQ, V_PAD), lambda h, i: (h, 0, 0)),
                  pl.BlockSpec((D_MODEL, ff_slice), lambda h, i: (0, h * nb + i)),
                  pl.BlockSpec((ff_slice, D_MODEL), lambda h, i: (h * nb + i, 0))],
        out_specs=[pl.BlockSpec((tb, MLA_V), lambda h, i: (i, h)),
                   pl.BlockSpec((D_MODEL, ff_slice), lambda h, i: (0, h * nb + i)),
                   pl.BlockSpec((ff_slice, D_MODEL), lambda h, i: (h * nb + i, 0))],
        out_shape=[jax.ShapeDtypeStruct((SEQ, MLA_WIDTH), F32),
                   jax.ShapeDtypeStruct((D_MODEL, D_FF), BF16),
                   jax.ShapeDtypeStruct((D_FF, D_MODEL), BF16)],
        scratch_shapes=[pltpu.VMEM((tq, tk), F32), pltpu.VMEM((tq, tk), F32),
                        pltpu.VMEM((tq, tk), BF16), pltpu.VMEM((tq, tk), BF16),
                        pltpu.VMEM((tq, LANES), F32), pltpu.VMEM((tq, LANES), F32),
                        pltpu.VMEM((tb, V_PAD), F32), pltpu.VMEM((tb, LANES), F32)],
        compiler_params=pltpu.CompilerParams(dimension_semantics=("arbitrary", "arbitrary"),
                                             vmem_limit_bytes=VMEM_LIMIT),
        name="mla",
    )(q, k, v, w1, w2)


NA_QROWS = 4
NA_KROWS = 12
NA_TQ = NA_QROWS * GRID_W
NA_TK = NA_KROWS * GRID_W
NA_STEPS = GRID_ROWS // NA_QROWS
NA_BIAS_ROWS = 2 * NA_ROWS - 1
NA_BIAS_COLS = 2 * NA_COLS - 1
NA_STRIP = 32


def _na_bias_plan(variant):
    plan = {}
    for a in range(NA_QROWS):
        for b in range(NA_KROWS):
            if variant == "first":
                dr, ok = b - a + NA_ROWS - 1, b < NA_ROWS
            elif variant == "last":
                dr, ok = b - a - 1, b >= NA_KROWS - NA_ROWS
            else:
                dr, ok = b - a + NA_ROWS // 2 - 1, a <= b < a + NA_ROWS
            plan[(a, b)] = dr if ok else None
    return plan


def _na_kernel(rpb_ref, q_ref, k0_ref, k1_ref, k2_ref, v0_ref, v1_ref, v2_ref, o_ref, t_ref, bias_ref,
               s0_ref, s1_ref, p0_ref, p1_ref, l0_ref, l1_ref):
    g = pl.program_id(0)
    s_refs, p_refs, l_refs = (s0_ref, s1_ref), (p0_ref, p1_ref), (l0_ref, l1_ref)

    @pl.when(g == 0)
    def _():
        c = lax.broadcasted_iota(jnp.int32, (GRID_W, GRID_W), 0)
        kc = lax.broadcasted_iota(jnp.int32, (GRID_W, GRID_W), 1)
        start = jnp.clip(c - NA_COLS // 2, 0, GRID_W - NA_COLS)
        dc = jnp.where((kc >= start) & (kc < start + NA_COLS), kc - c + NA_COLS - 1, -1)

        def per_row(i, carry):
            tile = jnp.full((GRID_W, GRID_W), NEG_BIG, F32)
            for d in range(NA_BIAS_COLS):
                tile = jnp.where(dc == d, rpb_ref[i * NA_BIAS_COLS + d] * LOG2_E, tile)
            t_ref[i] = tile
            return carry

        lax.fori_loop(0, HEADS * NA_BIAS_ROWS, per_row, 0)

    def build(variant):
        plan = _na_bias_plan(variant)

        def per_head(h, carry):
            for (a, b), dr in plan.items():
                if dr is None:
                    tile = jnp.full((GRID_W, GRID_W), NEG_BIG, F32)
                else:
                    tile = t_ref[h * NA_BIAS_ROWS + dr]
                bias_ref[h, a * GRID_W:(a + 1) * GRID_W, b * GRID_W:(b + 1) * GRID_W] = tile
            return carry

        lax.fori_loop(0, HEADS, per_head, 0)

    pl.when(g == 0)(lambda: build("first"))
    pl.when(g == 1)(lambda: build("mid"))
    pl.when(g == NA_STEPS - 1)(lambda: build("last"))

    k_refs = (k0_ref, k1_ref, k2_ref)
    v_refs = (v0_ref, v1_ref, v2_ref)
    for h in range(HEADS):
        cols = slice(h * NA_HEAD_DIM, (h + 1) * NA_HEAD_DIM)
        s_ref, p_ref, l_ref = s_refs[h % 2], p_refs[h % 2], l_refs[h % 2]
        qh = q_ref[:, cols]
        for t in range(3):
            s_ref[:, t * NA_TQ:(t + 1) * NA_TQ] = _dot_nt(qh, k_refs[t][:, cols])
        for r in range(NA_TQ // NA_STRIP):
            rows = slice(r * NA_STRIP, (r + 1) * NA_STRIP)
            s = s_ref[rows, :] + bias_ref[h, rows, :]
            p = jnp.exp2(s - jnp.max(s, axis=-1, keepdims=True))
            l_ref[rows, :] = jnp.broadcast_to(jnp.sum(p, axis=-1, keepdims=True), (NA_STRIP, LANES))
            p_ref[rows, :] = p.astype(BF16)
        o = _dot(p_ref[:, 0:NA_TQ], v_refs[0][:, cols])
        for t in (1, 2):
            o = o + _dot(p_ref[:, t * NA_TQ:(t + 1) * NA_TQ], v_refs[t][:, cols])
        o_ref[:, cols] = o / l_ref[...]


def _na(rpb, nq, nk, nv):
    first_blk = lambda g: jnp.clip(g - 1, 0, NA_STEPS - 3)
    kv_specs = [pl.BlockSpec((NA_TQ, NA_WIDTH), functools.partial(lambda g, t: (first_blk(g) + t, 0), t=t))
                for t in range(3)]
    return pl.pallas_call(
        _na_kernel,
        grid=(NA_STEPS,),
        in_specs=[pl.BlockSpec(memory_space=pltpu.SMEM), pl.BlockSpec((NA_TQ, NA_WIDTH), lambda g: (g, 0))]
                 + kv_specs + kv_specs,
        out_specs=pl.BlockSpec((NA_TQ, NA_WIDTH), lambda g: (g, 0)),
        out_shape=jax.ShapeDtypeStruct((SEQ, NA_WIDTH), F32),
        scratch_shapes=[pltpu.VMEM((HEADS * NA_BIAS_ROWS, GRID_W, GRID_W), F32),
                        pltpu.VMEM((HEADS, NA_TQ, NA_TK), F32),
                        pltpu.VMEM((NA_TQ, NA_TK), F32), pltpu.VMEM((NA_TQ, NA_TK), F32),
                        pltpu.VMEM((NA_TQ, NA_TK), BF16), pltpu.VMEM((NA_TQ, NA_TK), BF16),
                        pltpu.VMEM((NA_TQ, LANES), F32), pltpu.VMEM((NA_TQ, LANES), F32)],
        compiler_params=pltpu.CompilerParams(dimension_semantics=("arbitrary",),
                                             vmem_limit_bytes=VMEM_LIMIT),
        name="na",
    )(rpb.astype(F32).reshape(-1), nq, nk, nk, nk, nv, nv, nv)


def _outproj_kernel(a_ref, n_ref, ga_ref, gn_ref, w_ref, x_ref, o_ref, wb_ref):
    @pl.when(pl.program_id(0) == 0)
    def _():
        wb_ref[...] = w_ref[...].astype(BF16)

    an = _rms(a_ref[...], ga_ref[...]).astype(BF16)
    nn = _rms(n_ref[...], gn_ref[...]).astype(BF16)
    o_ref[...] = x_ref[...] + _dot(an, wb_ref[0:MLA_WIDTH, :]) + _dot(nn, wb_ref[MLA_WIDTH:, :])


def _outproj(a, n, ga, gn, w, x, tm=512):
    row = lambda w_: pl.BlockSpec((tm, w_), lambda i: (i, 0))
    return pl.pallas_call(
        _outproj_kernel,
        grid=(SEQ // tm,),
        in_specs=[row(MLA_WIDTH), row(NA_WIDTH), _const_spec((1, MLA_WIDTH)), _const_spec((1, NA_WIDTH)),
                  _const_spec((MLA_WIDTH + NA_WIDTH, D_MODEL)), row(D_MODEL)],
        out_specs=row(D_MODEL),
        out_shape=jax.ShapeDtypeStruct((SEQ, D_MODEL), F32),
        scratch_shapes=[pltpu.VMEM((MLA_WIDTH + NA_WIDTH, D_MODEL), BF16)],
        compiler_params=pltpu.CompilerParams(dimension_semantics=("arbitrary",),
                                             vmem_limit_bytes=VMEM_LIMIT),
        name="outproj",
    )(a, n, ga, gn, w, x)


def _mlp_kernel(x_ref, g_ref, w1_ref, w2_ref, gf_ref, o_ref, h_ref, acc_ref):
    j = pl.program_id(1)

    @pl.when(j == 0)
    def _():
        x = x_ref[...]
        h_ref[...] = _rms(x, g_ref[...]).astype(BF16)
        acc_ref[...] = x

    hid = jnp.maximum(_dot(h_ref[...], w1_ref[...]), 0.0)
    acc_ref[...] += _dot((hid * hid).astype(BF16), w2_ref[...])

    @pl.when(j == pl.num_programs(1) - 1)
    def _():
        o_ref[...] = _rms(acc_ref[...], gf_ref[...])


def _mlp(x, g, w1, w2, gf, tm=512, tf=1024):
    return pl.pallas_call(
        _mlp_kernel,
        grid=(SEQ // tm, D_FF // tf),
        in_specs=[pl.BlockSpec((tm, D_MODEL), lambda i, j: (i, 0)),
                  _const_spec((1, D_MODEL)),
                  pl.BlockSpec((D_MODEL, tf), lambda i, j: (0, j)),
                  pl.BlockSpec((tf, D_MODEL), lambda i, j: (j, 0)),
                  _const_spec((1, D_MODEL))],
        out_specs=pl.BlockSpec((tm, D_MODEL), lambda i, j: (i, 0)),
        out_shape=jax.ShapeDtypeStruct((SEQ, D_MODEL), F32),
        scratch_shapes=[pltpu.VMEM((tm, D_MODEL), BF16), pltpu.VMEM((tm, D_MODEL), F32)],
        compiler_params=pltpu.CompilerParams(dimension_semantics=("arbitrary", "arbitrary"),
                                             vmem_limit_bytes=VMEM_LIMIT),
        name="mlp",
    )(x, g, w1, w2, gf)


def _rope_tables():
    half = MLA_ROPE // 2
    lane = jnp.arange(LANES)
    inv = ROPE_THETA ** (-(lane % half).astype(F32) / half)
    ang = jnp.arange(SEQ).astype(F32)[:, None] * inv[None, :]
    sign = jnp.where(lane % MLA_ROPE < half, -1.0, 1.0).astype(F32)
    return jnp.cos(ang), jnp.sin(ang) * sign[None, :]


def kernel(x, attn_norm_g, w_in, q_norm_g, w_uq, kv_norm_g, w_ukv, na_rpb, mla_out_norm_g, na_out_norm_g,
           w_out, mlp_norm_g, w_ff1, w_ff2, final_norm_g):
    assert x.shape == (1, SEQ, D_MODEL)
    assert w_in.shape[0] == 1, "the final norm is fused into the (single) layer's MLP kernel"
    cos, sin = _rope_tables()
    xs = x[0]
    for l in range(1):
        wt = _win(jnp.swapaxes(w_in[l], 0, 1))
        wq = w_uq[l].reshape(MLA_Q_RANK, HEADS, MLA_QK)
        wuq = jnp.concatenate([wq[:, :, :MLA_NOPE].reshape(MLA_Q_RANK, HEADS * MLA_NOPE),
                               wq[:, :, MLA_NOPE:].reshape(MLA_Q_RANK, HEADS * MLA_ROPE)], axis=1).astype(BF16)
        wk = w_ukv[l].reshape(MLA_KV_RANK, HEADS, MLA_NOPE + MLA_V)
        wukv = jnp.concatenate([wk[:, :, :MLA_NOPE].reshape(MLA_KV_RANK, HEADS * MLA_NOPE),
                                wk[:, :, MLA_NOPE:].reshape(MLA_KV_RANK, HEADS * MLA_V)], axis=1).astype(BF16)

        q, k, v, nq, nk, nv = _proj(xs, attn_norm_g[l][None], wt, q_norm_g[l][None], wuq,
                                    kv_norm_g[l][None], wukv, cos, sin)
        a_out, w1b, w2b = _mla(q, k, v, w_ff1[l], w_ff2[l])
        n_out = _na(na_rpb[l], nq, nk, nv)
        xs = _outproj(a_out, n_out, mla_out_norm_g[l][None], na_out_norm_g[l][None],
                      w_out[l], xs)
        xs = _mlp(xs, mlp_norm_g[l][None], w1b, w2b, final_norm_g[None])
    return xs[None]
```

```python
import functools

import jax
import jax.numpy as jnp
from jax import lax
from jax.experimental import pallas as pl
from jax.experimental.pallas import tpu as pltpu

F32 = jnp.float32
BF16 = jnp.bfloat16

D_MODEL = 2048
SEQ = 8192
GRID_W = 64
GRID_ROWS = SEQ // GRID_W
HEADS = 8
MLA_NOPE = 128
MLA_ROPE = 64
MLA_V = 128
MLA_QK = MLA_NOPE + MLA_ROPE
MLA_Q_RANK = 512
MLA_KV_RANK = 256
ROPE_THETA = 10000.0
NA_HEAD_DIM = 128
NA_ROWS = 8
NA_COLS = 16
NA_WIDTH = HEADS * NA_HEAD_DIM
MLA_WIDTH = HEADS * MLA_V
D_FF = 4 * D_MODEL
NORM_EPS = 1e-6

LANES = 128
QK_PAD = 2 * LANES
V_PAD = 2 * LANES
LOG2_E = 1.4426950408889634
NEG_BIG = -1e30

_R_CQ = 0
_R_CKV = _R_CQ + MLA_Q_RANK
_R_KR = _R_CKV + MLA_KV_RANK
_R_NQ = _R_KR + MLA_ROPE
_R_NK = _R_NQ + NA_WIDTH
_R_NV = _R_NK + NA_WIDTH
IN_COLS = _R_NV + NA_WIDTH
WIN_STEPS = 4

VMEM_LIMIT = 56 * 1024 * 1024


def _rms(xf, g):
    y = xf * lax.rsqrt(jnp.mean(xf * xf, axis=-1, keepdims=True) + NORM_EPS)
    return y * g


def _dot(a, b):
    return jnp.dot(a, b, preferred_element_type=F32)


def _dot_nt(a, b):
    return lax.dot_general(a, b, (((1,), (1,)), ((), ())), preferred_element_type=F32)


def _const_spec(shape):
    nd = len(shape)
    return pl.BlockSpec(shape, lambda *_: (0,) * nd, pipeline_mode=pl.Buffered(1))


def _rope_lanes(x, c, s, lo_half):
    partner = jnp.where(lo_half, pltpu.roll(x, LANES - MLA_ROPE // 2, 1), pltpu.roll(x, MLA_ROPE // 2, 1))
    return x * c + partner * s


def _win_kernel(w_ref, wb_ref):
    wb_ref[...] = w_ref[...].astype(BF16)


def _win(wt):
    rows = pl.BlockSpec((IN_COLS // WIN_STEPS, D_MODEL), lambda i: (i, 0))
    return pl.pallas_call(
        _win_kernel,
        grid=(WIN_STEPS,),
        in_specs=[rows],
        out_specs=rows,
        out_shape=jax.ShapeDtypeStruct((IN_COLS, D_MODEL), BF16),
        compiler_params=pltpu.CompilerParams(dimension_semantics=("arbitrary",),
                                             vmem_limit_bytes=VMEM_LIMIT),
        name="win",
    )(wt)


def _proj_kernel(x_ref, g_ref, wt_ref, qg_ref, wuq_ref, kvg_ref, wukv_ref, cos_ref, sin_ref,
                 q_ref, k_ref, v_ref, nq_ref, nk_ref, nv_ref, cq_ref, ckv_ref, kr_ref, *, n_tiles):
    tm = x_ref.shape[0]
    i = pl.program_id(0)

    def stage1():
        h = _rms(x_ref[...], g_ref[...]).astype(BF16)

        def proj(lo, hi):
            return _dot_nt(h, wt_ref[lo:hi, :])

        cq_ref[...] = proj(_R_CQ, _R_CKV)
        ckv_ref[...] = proj(_R_CKV, _R_KR)
        wkr = wt_ref[_R_KR:_R_NQ, :]
        kr_ref[...] = _dot_nt(h, jnp.concatenate([wkr, wkr], axis=0))
        nq_ref[...] = (proj(_R_NQ, _R_NK) * (LOG2_E * NA_HEAD_DIM ** -0.5)).astype(BF16)
        nk_ref[...] = proj(_R_NK, _R_NV).astype(BF16)
        nv_ref[...] = proj(_R_NV, IN_COLS).astype(BF16)

    def stage2():
        c32, s32 = cos_ref[...], sin_ref[...]
        cos = jnp.concatenate([c32, c32] * (LANES // MLA_ROPE), axis=1)
        sin = jnp.concatenate([-s32, s32] * (LANES // MLA_ROPE), axis=1)
        lane = lax.broadcasted_iota(jnp.int32, (tm, LANES), 1)
        lo_half = (lane % MLA_ROPE) < (MLA_ROPE // 2)

        cqn = _rms(cq_ref[...], qg_ref[...]).astype(BF16)
        q = _dot(cqn, wuq_ref[...]) * (LOG2_E * MLA_QK ** -0.5)
        rope0 = HEADS * MLA_NOPE
        for pair in range(HEADS // 2):
            qr = _rope_lanes(q[:, rope0 + pair * LANES: rope0 + (pair + 1) * LANES], cos, sin, lo_half)
            qr = qr.astype(BF16)
            for hh in (2 * pair, 2 * pair + 1):
                q_ref[hh, :, 0:LANES] = q[:, hh * MLA_NOPE:(hh + 1) * MLA_NOPE].astype(BF16)
                q_ref[hh, :, LANES:QK_PAD] = qr

        ckvn = _rms(ckv_ref[...], kvg_ref[...]).astype(BF16)
        kv = _dot(ckvn, wukv_ref[...])
        kr = _rope_lanes(kr_ref[...], cos, sin, lo_half)
        kr_even = jnp.where(lane < MLA_ROPE, kr, 0.0).astype(BF16)
        kr_odd = jnp.where(lane >= MLA_ROPE, kr, 0.0).astype(BF16)
        v0 = HEADS * MLA_NOPE
        ones_col = jnp.where(lane == 0, 1.0, 0.0).astype(BF16)
        for hh in range(HEADS):
            k_ref[hh, :, 0:LANES] = kv[:, hh * MLA_NOPE:(hh + 1) * MLA_NOPE].astype(BF16)
            k_ref[hh, :, LANES:QK_PAD] = kr_even if hh % 2 == 0 else kr_odd
            v_ref[hh, :, 0:MLA_V] = kv[:, v0 + hh * MLA_V: v0 + (hh + 1) * MLA_V].astype(BF16)
            v_ref[hh, :, MLA_V:V_PAD] = ones_col

    pl.when(i == 0)(stage1)

    @pl.when((i > 0) & (i < n_tiles))
    def _():
        stage2()
        stage1()

    pl.when(i == n_tiles)(stage2)


def _proj(x, g, wt, qg, wuq, kvg, wukv, cos, sin, tm=512):
    n_tiles = SEQ // tm
    cur = lambda i: jnp.minimum(i, n_tiles - 1)
    prev = lambda i: jnp.maximum(i - 1, 0)
    row = lambda w, tile: pl.BlockSpec((tm, w), lambda i: (tile(i), 0))
    head = lambda w: pl.BlockSpec((HEADS, tm, w), lambda i: (0, prev(i), 0))
    return pl.pallas_call(
        functools.partial(_proj_kernel, n_tiles=n_tiles),
        grid=(n_tiles + 1,),
        in_specs=[row(D_MODEL, cur), _const_spec((1, D_MODEL)), _const_spec((IN_COLS, D_MODEL)),
                  _const_spec((1, MLA_Q_RANK)), _const_spec((MLA_Q_RANK, HEADS * MLA_QK)),
                  _const_spec((1, MLA_KV_RANK)), _const_spec((MLA_KV_RANK, HEADS * (MLA_NOPE + MLA_V))),
                  row(MLA_ROPE // 2, prev), row(MLA_ROPE // 2, prev)],
        out_specs=[head(QK_PAD), head(QK_PAD), head(V_PAD),
                   row(NA_WIDTH, cur), row(NA_WIDTH, cur), row(NA_WIDTH, cur)],
        out_shape=[jax.ShapeDtypeStruct((HEADS, SEQ, QK_PAD), BF16),
                   jax.ShapeDtypeStruct((HEADS, SEQ, QK_PAD), BF16),
                   jax.ShapeDtypeStruct((HEADS, SEQ, V_PAD), BF16),
                   jax.ShapeDtypeStruct((SEQ, NA_WIDTH), BF16),
                   jax.ShapeDtypeStruct((SEQ, NA_WIDTH), BF16),
                   jax.ShapeDtypeStruct((SEQ, NA_WIDTH), BF16)],
        scratch_shapes=[pltpu.VMEM((tm, MLA_Q_RANK), F32), pltpu.VMEM((tm, MLA_KV_RANK), F32),
                        pltpu.VMEM((tm, LANES), F32)],
        compiler_params=pltpu.CompilerParams(dimension_semantics=("arbitrary",),
                                             vmem_limit_bytes=VMEM_LIMIT),
        name="proj",
    )(x, g, wt, qg, wuq, kvg, wukv, cos, sin)


MLA_STRIP = 64


def _mla_kernel(q_ref, k_ref, v_ref, w1_ref, w2_ref, wo_ref, o_ref, w1b_ref, w2b_ref, wob_ref,
                s0_ref, s1_ref, p0_ref, p1_ref, a0_ref, a1_ref, acc_ref, m_ref, *, tq, tk):
    w1b_ref[...] = w1_ref[...].astype(BF16)
    w2b_ref[...] = w2_ref[...].astype(BF16)
    wob_ref[...] = wo_ref[...].astype(BF16)
    nk = SEQ // tk
    n_sub = q_ref.shape[0] // tq
    total = n_sub * nk
    s_refs, p_refs, a_refs = (s0_ref, s1_ref), (p0_ref, p1_ref), (a0_ref, a1_ref)

    def sub_rows(u):
        return slice((u // nk) * tq, (u // nk + 1) * tq)

    def chunk(u):
        return slice((u % nk) * tk, (u % nk + 1) * tk)

    def scores(u):
        s_refs[u % 2][...] = _dot_nt(q_ref[sub_rows(u), :], k_ref[chunk(u), :])

    def softmax(u):
        s_ref, p_ref, a_ref = s_refs[u % 2], p_refs[u % 2], a_refs[u % 2]
        base = (u // nk) * tq
        for r in range(tq // MLA_STRIP):
            rows = slice(r * MLA_STRIP, (r + 1) * MLA_STRIP)
            mrows = slice(base + r * MLA_STRIP, base + (r + 1) * MLA_STRIP)
            tiles = [s_ref[rows, c * LANES:(c + 1) * LANES] for c in range(tk // LANES)]
            tile_max = functools.reduce(jnp.maximum, tiles)
            m_old = m_ref[mrows, :]
            m_new = jnp.maximum(m_old, jnp.max(tile_max, axis=-1, keepdims=True))
            m_ref[mrows, :] = m_new
            a_ref[rows, :] = jnp.exp2(m_old - m_new)
            for c, s in enumerate(tiles):
                p_ref[rows, c * LANES:(c + 1) * LANES] = jnp.exp2(s - m_new).astype(BF16)

    def accumulate(u):
        rows = sub_rows(u)
        pv = _dot(p_refs[u % 2][...], v_ref[chunk(u), :])
        alpha = a_refs[u % 2][...]
        acc = [alpha * acc_ref[rows, c * LANES:(c + 1) * LANES] + pv[:, c * LANES:(c + 1) * LANES]
               for c in range(V_PAD // LANES)]
        if u % nk == nk - 1:
            o_ref[rows, :] = acc[0] / acc[1][:, 0:1]
        else:
            for c in range(V_PAD // LANES):
                acc_ref[rows, c * LANES:(c + 1) * LANES] = acc[c]

    m_ref[...] = jnp.full(m_ref.shape, -jnp.inf, F32)
    acc_ref[...] = jnp.zeros(acc_ref.shape, F32)
    for t in range(total + 2):
        if 1 <= t <= total:
            softmax(t - 1)
        if t < total:
            scores(t)
        if t >= 2:
            accumulate(t - 2)


def _mla(q, k, v, w1, w2, wo, tq=512, tk=512, n_sub=2):
    tb = n_sub * tq
    nb = SEQ // tb
    ff_slice = D_FF // (HEADS * nb)
    wo_slice = (MLA_WIDTH + NA_WIDTH) // (HEADS * nb)
    assert ff_slice % LANES == 0 and wo_slice % 16 == 0
    step = lambda h, i: h * nb + i
    return pl.pallas_call(
        functools.partial(_mla_kernel, tq=tq, tk=tk),
        grid=(HEADS, nb),
        in_specs=[pl.BlockSpec((None, tb, QK_PAD), lambda h, i: (h, i, 0)),
                  pl.BlockSpec((None, SEQ, QK_PAD), lambda h, i: (h, 0, 0)),
                  pl.BlockSpec((None, SEQ, V_PAD), lambda h, i: (h, 0, 0)),
                  pl.BlockSpec((D_MODEL, ff_slice), lambda h, i: (0, step(h, i))),
                  pl.BlockSpec((ff_slice, D_MODEL), lambda h, i: (step(h, i), 0)),
                  pl.BlockSpec((wo_slice, D_MODEL), lambda h, i: (step(h, i), 0))],
        out_specs=[pl.BlockSpec((tb, MLA_V), lambda h, i: (i, h)),
                   pl.BlockSpec((D_MODEL, ff_slice), lambda h, i: (0, step(h, i))),
                   pl.BlockSpec((ff_slice, D_MODEL), lambda h, i: (step(h, i), 0)),
                   pl.BlockSpec((wo_slice, D_MODEL), lambda h, i: (step(h, i), 0))],
        out_shape=[jax.ShapeDtypeStruct((SEQ, MLA_WIDTH), F32),
                   jax.ShapeDtypeStruct((D_MODEL, D_FF), BF16),
                   jax.ShapeDtypeStruct((D_FF, D_MODEL), BF16),
                   jax.ShapeDtypeStruct((MLA_WIDTH + NA_WIDTH, D_MODEL), BF16)],
        scratch_shapes=[pltpu.VMEM((tq, tk), F32), pltpu.VMEM((tq, tk), F32),
                        pltpu.VMEM((tq, tk), BF16), pltpu.VMEM((tq, tk), BF16),
                        pltpu.VMEM((tq, LANES), F32), pltpu.VMEM((tq, LANES), F32),
                        pltpu.VMEM((tb, V_PAD), F32), pltpu.VMEM((tb, LANES), F32)],
        compiler_params=pltpu.CompilerParams(dimension_semantics=("arbitrary", "arbitrary"),
                                             vmem_limit_bytes=VMEM_LIMIT),
        name="mla",
    )(q, k, v, w1, w2, wo)


NA_QROWS = 4
NA_KROWS = 12
NA_TQ = NA_QROWS * GRID_W
NA_TK = NA_KROWS * GRID_W
NA_STEPS = GRID_ROWS // NA_QROWS
NA_BIAS_ROWS = 2 * NA_ROWS - 1
NA_BIAS_COLS = 2 * NA_COLS - 1
NA_STRIP = 32


def _na_bias_plan(variant):
    plan = {}
    for a in range(NA_QROWS):
        for b in range(NA_KROWS):
            if variant == "first":
                dr, ok = b - a + NA_ROWS - 1, b < NA_ROWS
            elif variant == "last":
                dr, ok = b - a - 1, b >= NA_KROWS - NA_ROWS
            else:
                dr, ok = b - a + NA_ROWS // 2 - 1, a <= b < a + NA_ROWS
            plan[(a, b)] = dr if ok else None
    return plan


def _na_kernel(rpb_ref, q_ref, k0_ref, k1_ref, k2_ref, v0_ref, v1_ref, v2_ref, o_ref, t_ref, bias_ref,
               s0_ref, s1_ref, p0_ref, p1_ref, l0_ref, l1_ref):
    g = pl.program_id(0)
    s_refs, p_refs, l_refs = (s0_ref, s1_ref), (p0_ref, p1_ref), (l0_ref, l1_ref)

    @pl.when(g == 0)
    def _():
        c = lax.broadcasted_iota(jnp.int32, (GRID_W, GRID_W), 0)
        kc = lax.broadcasted_iota(jnp.int32, (GRID_W, GRID_W), 1)
        start = jnp.clip(c - NA_COLS // 2, 0, GRID_W - NA_COLS)
        dc = jnp.where((kc >= start) & (kc < start + NA_COLS), kc - c + NA_COLS - 1, -1)

        def per_row(i, carry):
            tile = jnp.full((GRID_W, GRID_W), NEG_BIG, F32)
            for d in range(NA_BIAS_COLS):
                tile = jnp.where(dc == d, rpb_ref[i * NA_BIAS_COLS + d] * LOG2_E, tile)
            t_ref[i] = tile
            return carry

        lax.fori_loop(0, HEADS * NA_BIAS_ROWS, per_row, 0)

    def build(variant):
        plan = _na_bias_plan(variant)

        def per_head(h, carry):
            for (a, b), dr in plan.items():
                if dr is None:
                    tile = jnp.full((GRID_W, GRID_W), NEG_BIG, F32)
                else:
                    tile = t_ref[h * NA_BIAS_ROWS + dr]
                bias_ref[h, a * GRID_W:(a + 1) * GRID_W, b * GRID_W:(b + 1) * GRID_W] = tile
            return carry

        lax.fori_loop(0, HEADS, per_head, 0)

    pl.when(g == 0)(lambda: build("first"))
    pl.when(g == 1)(lambda: build("mid"))
    pl.when(g == NA_STEPS - 1)(lambda: build("last"))

    k_refs = (k0_ref, k1_ref, k2_ref)
    v_refs = (v0_ref, v1_ref, v2_ref)
    for h in range(HEADS):
        cols = slice(h * NA_HEAD_DIM, (h + 1) * NA_HEAD_DIM)
        s_ref, p_ref, l_ref = s_refs[h % 2], p_refs[h % 2], l_refs[h % 2]
        qh = q_ref[:, cols]
        for t in range(3):
            s_ref[:, t * NA_TQ:(t + 1) * NA_TQ] = _dot_nt(qh, k_refs[t][:, cols])
        for r in range(NA_TQ // NA_STRIP):
            rows = slice(r * NA_STRIP, (r + 1) * NA_STRIP)
            s = s_ref[rows, :] + bias_ref[h, rows, :]
            p = jnp.exp2(s - jnp.max(s, axis=-1, keepdims=True))
            l_ref[rows, :] = jnp.broadcast_to(jnp.sum(p, axis=-1, keepdims=True), (NA_STRIP, LANES))
            p_ref[rows, :] = p.astype(BF16)
        o = _dot(p_ref[:, 0:NA_TQ], v_refs[0][:, cols])
        for t in (1, 2):
            o = o + _dot(p_ref[:, t * NA_TQ:(t + 1) * NA_TQ], v_refs[t][:, cols])
        o_ref[:, cols] = o / l_ref[...]


def _na(rpb, nq, nk, nv):
    first_blk = lambda g: jnp.clip(g - 1, 0, NA_STEPS - 3)
    kv_specs = [pl.BlockSpec((NA_TQ, NA_WIDTH), functools.partial(lambda g, t: (first_blk(g) + t, 0), t=t))
                for t in range(3)]
    return pl.pallas_call(
        _na_kernel,
        grid=(NA_STEPS,),
        in_specs=[pl.BlockSpec(memory_space=pltpu.SMEM), pl.BlockSpec((NA_TQ, NA_WIDTH), lambda g: (g, 0))]
                 + kv_specs + kv_specs,
        out_specs=pl.BlockSpec((NA_TQ, NA_WIDTH), lambda g: (g, 0)),
        out_shape=jax.ShapeDtypeStruct((SEQ, NA_WIDTH), F32),
        scratch_shapes=[pltpu.VMEM((HEADS * NA_BIAS_ROWS, GRID_W, GRID_W), F32),
                        pltpu.VMEM((HEADS, NA_TQ, NA_TK), F32),
                        pltpu.VMEM((NA_TQ, NA_TK), F32), pltpu.VMEM((NA_TQ, NA_TK), F32),
                        pltpu.VMEM((NA_TQ, NA_TK), BF16), pltpu.VMEM((NA_TQ, NA_TK), BF16),
                        pltpu.VMEM((NA_TQ, LANES), F32), pltpu.VMEM((NA_TQ, LANES), F32)],
        compiler_params=pltpu.CompilerParams(dimension_semantics=("arbitrary",),
                                             vmem_limit_bytes=VMEM_LIMIT),
        name="na",
    )(rpb.astype(F32).reshape(-1), nq, nk, nk, nk, nv, nv, nv)


def _outproj_kernel(a_ref, n_ref, ga_ref, gn_ref, wb_ref, x_ref, o_ref):
    an = _rms(a_ref[...], ga_ref[...]).astype(BF16)
    nn = _rms(n_ref[...], gn_ref[...]).astype(BF16)
    o_ref[...] = x_ref[...] + _dot(an, wb_ref[0:MLA_WIDTH, :]) + _dot(nn, wb_ref[MLA_WIDTH:, :])


def _outproj(a, n, ga, gn, w, x, tm=512):
    row = lambda w_: pl.BlockSpec((tm, w_), lambda i: (i, 0))
    return pl.pallas_call(
        _outproj_kernel,
        grid=(SEQ // tm,),
        in_specs=[row(MLA_WIDTH), row(NA_WIDTH), _const_spec((1, MLA_WIDTH)), _const_spec((1, NA_WIDTH)),
                  _const_spec((MLA_WIDTH + NA_WIDTH, D_MODEL)), row(D_MODEL)],
        out_specs=row(D_MODEL),
        out_shape=jax.ShapeDtypeStruct((SEQ, D_MODEL), F32),
        compiler_params=pltpu.CompilerParams(dimension_semantics=("arbitrary",),
                                             vmem_limit_bytes=VMEM_LIMIT),
        name="outproj",
    )(a, n, ga, gn, w, x)


def _mlp_kernel(x_ref, g_ref, w1_ref, w2_ref, gf_ref, o_ref, h_ref, acc_ref):
    j = pl.program_id(1)

    @pl.when(j == 0)
    def _():
        x = x_ref[...]
        h_ref[...] = _rms(x, g_ref[...]).astype(BF16)
        acc_ref[...] = x

    hid = jnp.maximum(_dot(h_ref[...], w1_ref[...]), 0.0)
    acc_ref[...] += _dot((hid * hid).astype(BF16), w2_ref[...])

    @pl.when(j == pl.num_programs(1) - 1)
    def _():
        o_ref[...] = _rms(acc_ref[...], gf_ref[...])


def _mlp(x, g, w1, w2, gf, tm=512, tf=1024):
    return pl.pallas_call(
        _mlp_kernel,
        grid=(SEQ // tm, D_FF // tf),
        in_specs=[pl.BlockSpec((tm, D_MODEL), lambda i, j: (i, 0)),
                  _const_spec((1, D_MODEL)),
                  pl.BlockSpec((D_MODEL, tf), lambda i, j: (0, j)),
                  pl.BlockSpec((tf, D_MODEL), lambda i, j: (j, 0)),
                  _const_spec((1, D_MODEL))],
        out_specs=pl.BlockSpec((tm, D_MODEL), lambda i, j: (i, 0)),
        out_shape=jax.ShapeDtypeStruct((SEQ, D_MODEL), F32),
        scratch_shapes=[pltpu.VMEM((tm, D_MODEL), BF16), pltpu.VMEM((tm, D_MODEL), F32)],
        compiler_params=pltpu.CompilerParams(dimension_semantics=("arbitrary", "arbitrary"),
                                             vmem_limit_bytes=VMEM_LIMIT),
        name="mlp",
    )(x, g, w1, w2, gf)


def _rope_tables():
    half = MLA_ROPE // 2
    inv = ROPE_THETA ** (-jnp.arange(half, dtype=F32) / half)
    ang = jnp.arange(SEQ).astype(F32)[:, None] * inv[None, :]
    return jnp.cos(ang), jnp.sin(ang)


def kernel(x, attn_norm_g, w_in, q_norm_g, w_uq, kv_norm_g, w_ukv, na_rpb, mla_out_norm_g, na_out_norm_g,
           w_out, mlp_norm_g, w_ff1, w_ff2, final_norm_g):
    assert x.shape == (1, SEQ, D_MODEL)
    assert w_in.shape[0] == 1, "the final norm is fused into the (single) layer's MLP kernel"
    cos, sin = _rope_tables()
    xs = x[0]
    for l in range(1):
        wt = _win(jnp.swapaxes(w_in[l], 0, 1))
        wq = w_uq[l].reshape(MLA_Q_RANK, HEADS, MLA_QK)
        wuq = jnp.concatenate([wq[:, :, :MLA_NOPE].reshape(MLA_Q_RANK, HEADS * MLA_NOPE),
                               wq[:, :, MLA_NOPE:].reshape(MLA_Q_RANK, HEADS * MLA_ROPE)], axis=1).astype(BF16)
        wk = w_ukv[l].reshape(MLA_KV_RANK, HEADS, MLA_NOPE + MLA_V)
        wukv = jnp.concatenate([wk[:, :, :MLA_NOPE].reshape(MLA_KV_RANK, HEADS * MLA_NOPE),
                                wk[:, :, MLA_NOPE:].reshape(MLA_KV_RANK, HEADS * MLA_V)], axis=1).astype(BF16)

        q, k, v, nq, nk, nv = _proj(xs, attn_norm_g[l][None], wt, q_norm_g[l][None], wuq,
                                    kv_norm_g[l][None], wukv, cos, sin)
        a_out, w1b, w2b, wob = _mla(q, k, v, w_ff1[l], w_ff2[l], w_out[l])
        n_out = _na(na_rpb[l], nq, nk, nv)
        xs = _outproj(a_out, n_out, mla_out_norm_g[l][None], na_out_norm_g[l][None],
                      wob, xs)
        xs = _mlp(xs, mlp_norm_g[l][None], w1b, w2b, final_norm_g[None])
    return xs[None]
```

```python
import functools

import jax
import jax.numpy as jnp
from jax import lax
from jax.experimental import pallas as pl
from jax.experimental.pallas import tpu as pltpu

F32 = jnp.float32
BF16 = jnp.bfloat16

D_MODEL = 2048
SEQ = 8192
GRID_W = 64
GRID_ROWS = SEQ // GRID_W
HEADS = 8
MLA_NOPE = 128
MLA_ROPE = 64
MLA_V = 128
MLA_QK = MLA_NOPE + MLA_ROPE
MLA_Q_RANK = 512
MLA_KV_RANK = 256
ROPE_THETA = 10000.0
NA_HEAD_DIM = 128
NA_ROWS = 8
NA_COLS = 16
NA_WIDTH = HEADS * NA_HEAD_DIM
MLA_WIDTH = HEADS * MLA_V
D_FF = 4 * D_MODEL
NORM_EPS = 1e-6

LANES = 128
QK_PAD = 2 * LANES
V_PAD = 2 * LANES
LOG2_E = 1.4426950408889634
NEG_BIG = -1e30

_R_CQ = 0
_R_CKV = _R_CQ + MLA_Q_RANK
_R_KR = _R_CKV + MLA_KV_RANK
_R_NQ = _R_KR + MLA_ROPE
_R_NK = _R_NQ + NA_WIDTH
_R_NV = _R_NK + NA_WIDTH
IN_COLS = _R_NV + NA_WIDTH
WIN_STEPS = 4

VMEM_LIMIT = 56 * 1024 * 1024


def _rms(xf, g):
    y = xf * lax.rsqrt(jnp.mean(xf * xf, axis=-1, keepdims=True) + NORM_EPS)
    return y * g


def _dot(a, b):
    return jnp.dot(a, b, preferred_element_type=F32)


def _dot_nt(a, b):
    return lax.dot_general(a, b, (((1,), (1,)), ((), ())), preferred_element_type=F32)


def _const_spec(shape):
    nd = len(shape)
    return pl.BlockSpec(shape, lambda *_: (0,) * nd, pipeline_mode=pl.Buffered(1))


def _rope_lanes(x, c, s, lo_half):
    partner = jnp.where(lo_half, pltpu.roll(x, LANES - MLA_ROPE // 2, 1), pltpu.roll(x, MLA_ROPE // 2, 1))
    return x * c + partner * s


def _win_kernel(w_ref, wb_ref):
    wb_ref[...] = w_ref[...].astype(BF16)


def _win(wt):
    rows = pl.BlockSpec((IN_COLS // WIN_STEPS, D_MODEL), lambda i: (i, 0))
    return pl.pallas_call(
        _win_kernel,
        grid=(WIN_STEPS,),
        in_specs=[rows],
        out_specs=rows,
        out_shape=jax.ShapeDtypeStruct((IN_COLS, D_MODEL), BF16),
        compiler_params=pltpu.CompilerParams(dimension_semantics=("arbitrary",),
                                             vmem_limit_bytes=VMEM_LIMIT),
        name="win",
    )(wt)


def _proj_kernel(x_ref, g_ref, wt_ref, qg_ref, wuq_ref, kvg_ref, wukv_ref, cos_ref, sin_ref,
                 q_ref, k_ref, v_ref, nq_ref, nk_ref, nv_ref, cq_ref, ckv_ref, kr_ref, *, n_tiles):
    tm = x_ref.shape[0]
    i = pl.program_id(0)

    def stage1():
        h = _rms(x_ref[...], g_ref[...]).astype(BF16)

        def proj(lo, hi):
            return _dot_nt(h, wt_ref[lo:hi, :])

        cq_ref[...] = proj(_R_CQ, _R_CKV)
        ckv_ref[...] = proj(_R_CKV, _R_KR)
        wkr = wt_ref[_R_KR:_R_NQ, :]
        kr_ref[...] = _dot_nt(h, jnp.concatenate([wkr, wkr], axis=0))
        nq_ref[...] = (proj(_R_NQ, _R_NK) * (LOG2_E * NA_HEAD_DIM ** -0.5)).astype(BF16)
        nk_ref[...] = proj(_R_NK, _R_NV).astype(BF16)
        nv_ref[...] = proj(_R_NV, IN_COLS).astype(BF16)

    def stage2():
        c32, s32 = cos_ref[...], sin_ref[...]
        cos = jnp.concatenate([c32, c32] * (LANES // MLA_ROPE), axis=1)
        sin = jnp.concatenate([-s32, s32] * (LANES // MLA_ROPE), axis=1)
        lane = lax.broadcasted_iota(jnp.int32, (tm, LANES), 1)
        lo_half = (lane % MLA_ROPE) < (MLA_ROPE // 2)

        cqn = _rms(cq_ref[...], qg_ref[...]).astype(BF16)
        q = _dot(cqn, wuq_ref[...]) * (LOG2_E * MLA_QK ** -0.5)
        rope0 = HEADS * MLA_NOPE
        for pair in range(HEADS // 2):
            qr = _rope_lanes(q[:, rope0 + pair * LANES: rope0 + (pair + 1) * LANES], cos, sin, lo_half)
            qr = qr.astype(BF16)
            for hh in (2 * pair, 2 * pair + 1):
                q_ref[hh, :, 0:LANES] = q[:, hh * MLA_NOPE:(hh + 1) * MLA_NOPE].astype(BF16)
                q_ref[hh, :, LANES:QK_PAD] = qr

        ckvn = _rms(ckv_ref[...], kvg_ref[...]).astype(BF16)
        kv = _dot(ckvn, wukv_ref[...])
        kr = _rope_lanes(kr_ref[...], cos, sin, lo_half)
        kr_even = jnp.where(lane < MLA_ROPE, kr, 0.0).astype(BF16)
        kr_odd = jnp.where(lane >= MLA_ROPE, kr, 0.0).astype(BF16)
        v0 = HEADS * MLA_NOPE
        ones_col = jnp.where(lane == 0, 1.0, 0.0).astype(BF16)
        for hh in range(HEADS):
            k_ref[hh, :, 0:LANES] = kv[:, hh * MLA_NOPE:(hh + 1) * MLA_NOPE].astype(BF16)
            k_ref[hh, :, LANES:QK_PAD] = kr_even if hh % 2 == 0 else kr_odd
            v_ref[hh, :, 0:MLA_V] = kv[:, v0 + hh * MLA_V: v0 + (hh + 1) * MLA_V].astype(BF16)
            v_ref[hh, :, MLA_V:V_PAD] = ones_col

    pl.when(i == 0)(stage1)

    @pl.when((i > 0) & (i < n_tiles))
    def _():
        stage2()
        stage1()

    pl.when(i == n_tiles)(stage2)


def _proj(x, g, wt, qg, wuq, kvg, wukv, cos, sin, tm=512):
    n_tiles = SEQ // tm
    cur = lambda i: jnp.minimum(i, n_tiles - 1)
    prev = lambda i: jnp.maximum(i - 1, 0)
    row = lambda w, tile: pl.BlockSpec((tm, w), lambda i: (tile(i), 0))
    head = lambda w: pl.BlockSpec((HEADS, tm, w), lambda i: (0, prev(i), 0))
    return pl.pallas_call(
        functools.partial(_proj_kernel, n_tiles=n_tiles),
        grid=(n_tiles + 1,),
        in_specs=[row(D_MODEL, cur), _const_spec((1, D_MODEL)), _const_spec((IN_COLS, D_MODEL)),
                  _const_spec((1, MLA_Q_RANK)), _const_spec((MLA_Q_RANK, HEADS * MLA_QK)),
                  _const_spec((1, MLA_KV_RANK)), _const_spec((MLA_KV_RANK, HEADS * (MLA_NOPE + MLA_V))),
                  row(MLA_ROPE // 2, prev), row(MLA_ROPE // 2, prev)],
        out_specs=[head(QK_PAD), head(QK_PAD), head(V_PAD),
                   row(NA_WIDTH, cur), row(NA_WIDTH, cur), row(NA_WIDTH, cur)],
        out_shape=[jax.ShapeDtypeStruct((HEADS, SEQ, QK_PAD), BF16),
                   jax.ShapeDtypeStruct((HEADS, SEQ, QK_PAD), BF16),
                   jax.ShapeDtypeStruct((HEADS, SEQ, V_PAD), BF16),
                   jax.ShapeDtypeStruct((SEQ, NA_WIDTH), BF16),
                   jax.ShapeDtypeStruct((SEQ, NA_WIDTH), BF16),
                   jax.ShapeDtypeStruct((SEQ, NA_WIDTH), BF16)],
        scratch_shapes=[pltpu.VMEM((tm, MLA_Q_RANK), F32), pltpu.VMEM((tm, MLA_KV_RANK), F32),
                        pltpu.VMEM((tm, LANES), F32)],
        compiler_params=pltpu.CompilerParams(dimension_semantics=("arbitrary",),
                                             vmem_limit_bytes=VMEM_LIMIT),
        name="proj",
    )(x, g, wt, qg, wuq, kvg, wukv, cos, sin)


MLA_STRIP = 64


def _mla_stream(n_sub, tk):
    nk = SEQ // tk
    return [(sub, c * tk, tk, c == nk - 1) for sub in range(n_sub) for c in range(nk)]


def _mla_kernel(q_ref, k_ref, v_ref, w1_ref, w2_ref, wo_ref, o_ref, w1b_ref, w2b_ref, wob_ref,
                s0_ref, s1_ref, p0_ref, p1_ref, a0_ref, a1_ref, acc_ref, m_ref, *, tq, tk):
    w1b_ref[...] = w1_ref[...].astype(BF16)
    w2b_ref[...] = w2_ref[...].astype(BF16)
    wob_ref[...] = wo_ref[...].astype(BF16)
    stream = _mla_stream(q_ref.shape[0] // tq, tk)
    total = len(stream)
    s_refs, p_refs, a_refs = (s0_ref, s1_ref), (p0_ref, p1_ref), (a0_ref, a1_ref)

    def scores(u):
        sub, start, size, _ = stream[u]
        s_refs[u % 2][:, :size] = _dot_nt(q_ref[sub * tq:(sub + 1) * tq, :], k_ref[start:start + size, :])

    def softmax(u):
        sub, _, size, _ = stream[u]
        s_ref, p_ref, a_ref = s_refs[u % 2], p_refs[u % 2], a_refs[u % 2]
        for r in range(tq // MLA_STRIP):
            rows = slice(r * MLA_STRIP, (r + 1) * MLA_STRIP)
            mrows = slice(sub * tq + r * MLA_STRIP, sub * tq + (r + 1) * MLA_STRIP)
            tiles = [s_ref[rows, c * LANES:(c + 1) * LANES] for c in range(size // LANES)]
            tile_max = functools.reduce(jnp.maximum, tiles)
            m_old = m_ref[mrows, :]
            m_new = jnp.maximum(m_old, jnp.max(tile_max, axis=-1, keepdims=True))
            m_ref[mrows, :] = m_new
            a_ref[rows, :] = jnp.exp2(m_old - m_new)
            for c, s in enumerate(tiles):
                p_ref[rows, c * LANES:(c + 1) * LANES] = jnp.exp2(s - m_new).astype(BF16)

    def accumulate(u):
        sub, start, size, last = stream[u]
        rows = slice(sub * tq, (sub + 1) * tq)
        pv = _dot(p_refs[u % 2][:, :size], v_ref[start:start + size, :])
        alpha = a_refs[u % 2][...]
        acc = [alpha * acc_ref[rows, c * LANES:(c + 1) * LANES] + pv[:, c * LANES:(c + 1) * LANES]
               for c in range(V_PAD // LANES)]
        if last:
            o_ref[rows, :] = acc[0] / acc[1][:, 0:1]
        else:
            for c in range(V_PAD // LANES):
                acc_ref[rows, c * LANES:(c + 1) * LANES] = acc[c]

    m_ref[...] = jnp.full(m_ref.shape, -jnp.inf, F32)
    acc_ref[...] = jnp.zeros(acc_ref.shape, F32)
    for t in range(total + 2):
        if 1 <= t <= total:
            softmax(t - 1)
        if t < total:
            scores(t)
        if t >= 2:
            accumulate(t - 2)


def _mla(q, k, v, w1, w2, wo, tq=512, tk=1024, n_sub=2):
    tb = n_sub * tq
    nb = SEQ // tb
    ff_slice = D_FF // (HEADS * nb)
    wo_slice = (MLA_WIDTH + NA_WIDTH) // (HEADS * nb)
    assert ff_slice % LANES == 0 and wo_slice % 16 == 0
    step = lambda h, i: h * nb + i
    return pl.pallas_call(
        functools.partial(_mla_kernel, tq=tq, tk=tk),
        grid=(HEADS, nb),
        in_specs=[pl.BlockSpec((None, tb, QK_PAD), lambda h, i: (h, i, 0)),
                  pl.BlockSpec((None, SEQ, QK_PAD), lambda h, i: (h, 0, 0)),
                  pl.BlockSpec((None, SEQ, V_PAD), lambda h, i: (h, 0, 0)),
                  pl.BlockSpec((D_MODEL, ff_slice), lambda h, i: (0, step(h, i))),
                  pl.BlockSpec((ff_slice, D_MODEL), lambda h, i: (step(h, i), 0)),
                  pl.BlockSpec((wo_slice, D_MODEL), lambda h, i: (step(h, i), 0))],
        out_specs=[pl.BlockSpec((tb, MLA_V), lambda h, i: (i, h)),
                   pl.BlockSpec((D_MODEL, ff_slice), lambda h, i: (0, step(h, i))),
                   pl.BlockSpec((ff_slice, D_MODEL), lambda h, i: (step(h, i), 0)),
                   pl.BlockSpec((wo_slice, D_MODEL), lambda h, i: (step(h, i), 0))],
        out_shape=[jax.ShapeDtypeStruct((SEQ, MLA_WIDTH), F32),
                   jax.ShapeDtypeStruct((D_MODEL, D_FF), BF16),
                   jax.ShapeDtypeStruct((D_FF, D_MODEL), BF16),
                   jax.ShapeDtypeStruct((MLA_WIDTH + NA_WIDTH, D_MODEL), BF16)],
        scratch_shapes=[pltpu.VMEM((tq, tk), F32), pltpu.VMEM((tq, tk), F32),
                        pltpu.VMEM((tq, tk), BF16), pltpu.VMEM((tq, tk), BF16),
                        pltpu.VMEM((tq, LANES), F32), pltpu.VMEM((tq, LANES), F32),
                        pltpu.VMEM((tb, V_PAD), F32), pltpu.VMEM((tb, LANES), F32)],
        compiler_params=pltpu.CompilerParams(dimension_semantics=("arbitrary", "arbitrary"),
                                             vmem_limit_bytes=VMEM_LIMIT),
        name="mla",
    )(q, k, v, w1, w2, wo)


NA_QROWS = 4
NA_KROWS = 12
NA_TQ = NA_QROWS * GRID_W
NA_TK = NA_KROWS * GRID_W
NA_STEPS = GRID_ROWS // NA_QROWS
NA_BIAS_ROWS = 2 * NA_ROWS - 1
NA_BIAS_COLS = 2 * NA_COLS - 1
NA_STRIP = 32


def _na_bias_plan(variant):
    plan = {}
    for a in range(NA_QROWS):
        for b in range(NA_KROWS):
            if variant == "first":
                dr, ok = b - a + NA_ROWS - 1, b < NA_ROWS
            elif variant == "last":
                dr, ok = b - a - 1, b >= NA_KROWS - NA_ROWS
            else:
                dr, ok = b - a + NA_ROWS // 2 - 1, a <= b < a + NA_ROWS
            plan[(a, b)] = dr if ok else None
    return plan


def _na_kernel(rpb_ref, q_ref, k0_ref, k1_ref, k2_ref, v0_ref, v1_ref, v2_ref, o_ref, t_ref, bias_ref,
               s0_ref, s1_ref, p0_ref, p1_ref, l0_ref, l1_ref):
    g = pl.program_id(0)
    s_refs, p_refs, l_refs = (s0_ref, s1_ref), (p0_ref, p1_ref), (l0_ref, l1_ref)

    @pl.when(g == 0)
    def _():
        c = lax.broadcasted_iota(jnp.int32, (GRID_W, GRID_W), 0)
        kc = lax.broadcasted_iota(jnp.int32, (GRID_W, GRID_W), 1)
        start = jnp.clip(c - NA_COLS // 2, 0, GRID_W - NA_COLS)
        dc = jnp.where((kc >= start) & (kc < start + NA_COLS), kc - c + NA_COLS - 1, -1)

        def per_row(i, carry):
            tile = jnp.full((GRID_W, GRID_W), NEG_BIG, F32)
            for d in range(NA_BIAS_COLS):
                tile = jnp.where(dc == d, rpb_ref[i * NA_BIAS_COLS + d] * LOG2_E, tile)
            t_ref[i] = tile
            return carry

        lax.fori_loop(0, HEADS * NA_BIAS_ROWS, per_row, 0)

    def build(variant):
        plan = _na_bias_plan(variant)

        def per_head(h, carry):
            for (a, b), dr in plan.items():
                if dr is None:
                    tile = jnp.full((GRID_W, GRID_W), NEG_BIG, F32)
                else:
                    tile = t_ref[h * NA_BIAS_ROWS + dr]
                bias_ref[h, a * GRID_W:(a + 1) * GRID_W, b * GRID_W:(b + 1) * GRID_W] = tile
            return carry

        lax.fori_loop(0, HEADS, per_head, 0)

    pl.when(g == 0)(lambda: build("first"))
    pl.when(g == 1)(lambda: build("mid"))
    pl.when(g == NA_STEPS - 1)(lambda: build("last"))

    k_refs = (k0_ref, k1_ref, k2_ref)
    v_refs = (v0_ref, v1_ref, v2_ref)
    for h in range(HEADS):
        cols = slice(h * NA_HEAD_DIM, (h + 1) * NA_HEAD_DIM)
        s_ref, p_ref, l_ref = s_refs[h % 2], p_refs[h % 2], l_refs[h % 2]
        qh = q_ref[:, cols]
        for t in range(3):
            s_ref[:, t * NA_TQ:(t + 1) * NA_TQ] = _dot_nt(qh, k_refs[t][:, cols])
        for r in range(NA_TQ // NA_STRIP):
            rows = slice(r * NA_STRIP, (r + 1) * NA_STRIP)
            s = s_ref[rows, :] + bias_ref[h, rows, :]
            p = jnp.exp2(s - jnp.max(s, axis=-1, keepdims=True))
            l_ref[rows, :] = jnp.broadcast_to(jnp.sum(p, axis=-1, keepdims=True), (NA_STRIP, LANES))
            p_ref[rows, :] = p.astype(BF16)
        o = _dot(p_ref[:, 0:NA_TQ], v_refs[0][:, cols])
        for t in (1, 2):
            o = o + _dot(p_ref[:, t * NA_TQ:(t + 1) * NA_TQ], v_refs[t][:, cols])
        o_ref[:, cols] = o / l_ref[...]


def _na(rpb, nq, nk, nv):
    first_blk = lambda g: jnp.clip(g - 1, 0, NA_STEPS - 3)
    kv_specs = [pl.BlockSpec((NA_TQ, NA_WIDTH), functools.partial(lambda g, t: (first_blk(g) + t, 0), t=t))
                for t in range(3)]
    return pl.pallas_call(
        _na_kernel,
        grid=(NA_STEPS,),
        in_specs=[pl.BlockSpec(memory_space=pltpu.SMEM), pl.BlockSpec((NA_TQ, NA_WIDTH), lambda g: (g, 0))]
                 + kv_specs + kv_specs,
        out_specs=pl.BlockSpec((NA_TQ, NA_WIDTH), lambda g: (g, 0)),
        out_shape=jax.ShapeDtypeStruct((SEQ, NA_WIDTH), F32),
        scratch_shapes=[pltpu.VMEM((HEADS * NA_BIAS_ROWS, GRID_W, GRID_W), F32),
                        pltpu.VMEM((HEADS, NA_TQ, NA_TK), F32),
                        pltpu.VMEM((NA_TQ, NA_TK), F32), pltpu.VMEM((NA_TQ, NA_TK), F32),
                        pltpu.VMEM((NA_TQ, NA_TK), BF16), pltpu.VMEM((NA_TQ, NA_TK), BF16),
                        pltpu.VMEM((NA_TQ, LANES), F32), pltpu.VMEM((NA_TQ, LANES), F32)],
        compiler_params=pltpu.CompilerParams(dimension_semantics=("arbitrary",),
                                             vmem_limit_bytes=VMEM_LIMIT),
        name="na",
    )(rpb.astype(F32).reshape(-1), nq, nk, nk, nk, nv, nv, nv)


def _outproj_kernel(a_ref, n_ref, ga_ref, gn_ref, wb_ref, x_ref, o_ref):
    an = _rms(a_ref[...], ga_ref[...]).astype(BF16)
    nn = _rms(n_ref[...], gn_ref[...]).astype(BF16)
    o_ref[...] = x_ref[...] + _dot(an, wb_ref[0:MLA_WIDTH, :]) + _dot(nn, wb_ref[MLA_WIDTH:, :])


def _outproj(a, n, ga, gn, w, x, tm=512):
    row = lambda w_: pl.BlockSpec((tm, w_), lambda i: (i, 0))
    return pl.pallas_call(
        _outproj_kernel,
        grid=(SEQ // tm,),
        in_specs=[row(MLA_WIDTH), row(NA_WIDTH), _const_spec((1, MLA_WIDTH)), _const_spec((1, NA_WIDTH)),
                  _const_spec((MLA_WIDTH + NA_WIDTH, D_MODEL)), row(D_MODEL)],
        out_specs=row(D_MODEL),
        out_shape=jax.ShapeDtypeStruct((SEQ, D_MODEL), F32),
        compiler_params=pltpu.CompilerParams(dimension_semantics=("arbitrary",),
                                             vmem_limit_bytes=VMEM_LIMIT),
        name="outproj",
    )(a, n, ga, gn, w, x)


def _mlp_kernel(x_ref, g_ref, w1_ref, w2_ref, gf_ref, o_ref, h_ref, acc_ref):
    j = pl.program_id(1)

    @pl.when(j == 0)
    def _():
        x = x_ref[...]
        h_ref[...] = _rms(x, g_ref[...]).astype(BF16)
        acc_ref[...] = x

    hid = jnp.maximum(_dot(h_ref[...], w1_ref[...]), 0.0)
    acc_ref[...] += _dot((hid * hid).astype(BF16), w2_ref[...])

    @pl.when(j == pl.num_programs(1) - 1)
    def _():
        o_ref[...] = _rms(acc_ref[...], gf_ref[...])


def _mlp(x, g, w1, w2, gf, tm=512, tf=1024):
    return pl.pallas_call(
        _mlp_kernel,
        grid=(SEQ // tm, D_FF // tf),
        in_specs=[pl.BlockSpec((tm, D_MODEL), lambda i, j: (i, 0)),
                  _const_spec((1, D_MODEL)),
                  pl.BlockSpec((D_MODEL, tf), lambda i, j: (0, j)),
                  pl.BlockSpec((tf, D_MODEL), lambda i, j: (j, 0)),
                  _const_spec((1, D_MODEL))],
        out_specs=pl.BlockSpec((tm, D_MODEL), lambda i, j: (i, 0)),
        out_shape=jax.ShapeDtypeStruct((SEQ, D_MODEL), F32),
        scratch_shapes=[pltpu.VMEM((tm, D_MODEL), BF16), pltpu.VMEM((tm, D_MODEL), F32)],
        compiler_params=pltpu.CompilerParams(dimension_semantics=("arbitrary", "arbitrary"),
                                             vmem_limit_bytes=VMEM_LIMIT),
        name="mlp",
    )(x, g, w1, w2, gf)


def _rope_tables():
    half = MLA_ROPE // 2
    per_row = LANES // half
    row = jnp.arange(SEQ // per_row)[:, None]
    lane = jnp.arange(LANES)[None, :]
    pos = (row * per_row + lane // half).astype(F32)
    inv = ROPE_THETA ** (-(lane % half).astype(F32) / half)
    ang = pos * inv
    return jnp.cos(ang).reshape(SEQ, half), jnp.sin(ang).reshape(SEQ, half)


def kernel(x, attn_norm_g, w_in, q_norm_g, w_uq, kv_norm_g, w_ukv, na_rpb, mla_out_norm_g, na_out_norm_g,
           w_out, mlp_norm_g, w_ff1, w_ff2, final_norm_g):
    assert x.shape == (1, SEQ, D_MODEL)
    assert w_in.shape[0] == 1, "the final norm is fused into the (single) layer's MLP kernel"
    cos, sin = _rope_tables()
    xs = x[0]
    for l in range(1):
        wt = _win(jnp.swapaxes(w_in[l], 0, 1))
        wq = w_uq[l].reshape(MLA_Q_RANK, HEADS, MLA_QK)
        wuq = jnp.concatenate([wq[:, :, :MLA_NOPE].reshape(MLA_Q_RANK, HEADS * MLA_NOPE),
                               wq[:, :, MLA_NOPE:].reshape(MLA_Q_RANK, HEADS * MLA_ROPE)], axis=1).astype(BF16)
        wk = w_ukv[l].reshape(MLA_KV_RANK, HEADS, MLA_NOPE + MLA_V)
        wukv = jnp.concatenate([wk[:, :, :MLA_NOPE].reshape(MLA_KV_RANK, HEADS * MLA_NOPE),
                                wk[:, :, MLA_NOPE:].reshape(MLA_KV_RANK, HEADS * MLA_V)], axis=1).astype(BF16)

        q, k, v, nq, nk, nv = _proj(xs, attn_norm_g[l][None], wt, q_norm_g[l][None], wuq,
                                    kv_norm_g[l][None], wukv, cos, sin)
        a_out, w1b, w2b, wob = _mla(q, k, v, w_ff1[l], w_ff2[l], w_out[l])
        n_out = _na(na_rpb[l], nq, nk, nv)
        xs = _outproj(a_out, n_out, mla_out_norm_g[l][None], na_out_norm_g[l][None],
                      wob, xs)
        xs = _mlp(xs, mlp_norm_g[l][None], w1b, w2b, final_norm_g[None])
    return xs[None]
```

```python
import functools

import jax
import jax.numpy as jnp
from jax import lax
from jax.experimental import pallas as pl
from jax.experimental.pallas import tpu as pltpu

F32 = jnp.float32
BF16 = jnp.bfloat16

D_MODEL = 2048
SEQ = 8192
GRID_W = 64
GRID_ROWS = SEQ // GRID_W
HEADS = 8
MLA_NOPE = 128
MLA_ROPE = 64
MLA_V = 128
MLA_QK = MLA_NOPE + MLA_ROPE
MLA_Q_RANK = 512
MLA_KV_RANK = 256
ROPE_THETA = 10000.0
NA_HEAD_DIM = 128
NA_ROWS = 8
NA_COLS = 16
NA_WIDTH = HEADS * NA_HEAD_DIM
MLA_WIDTH = HEADS * MLA_V
D_FF = 4 * D_MODEL
NORM_EPS = 1e-6

LANES = 128
QK_PAD = 2 * LANES
V_PAD = 2 * LANES
LOG2_E = 1.4426950408889634
ROPE_SPLIT = 64
NEG_BIG = -1e30

_R_CQ = 0
_R_CKV = _R_CQ + MLA_Q_RANK
_R_KR = _R_CKV + MLA_KV_RANK
_R_NQ = _R_KR + MLA_ROPE
_R_NK = _R_NQ + NA_WIDTH
_R_NV = _R_NK + NA_WIDTH
IN_COLS = _R_NV + NA_WIDTH
WIN_STEPS = 4

VMEM_LIMIT = 56 * 1024 * 1024


def _rms(xf, g):
    y = xf * lax.rsqrt(jnp.mean(xf * xf, axis=-1, keepdims=True) + NORM_EPS)
    return y * g


def _dot(a, b):
    return jnp.dot(a, b, preferred_element_type=F32)


def _dot_nt(a, b):
    return lax.dot_general(a, b, (((1,), (1,)), ((), ())), preferred_element_type=F32)


def _const_spec(shape):
    nd = len(shape)
    return pl.BlockSpec(shape, lambda *_: (0,) * nd, pipeline_mode=pl.Buffered(1))


def _rope_lanes(x, c, s, lo_half):
    partner = jnp.where(lo_half, pltpu.roll(x, LANES - MLA_ROPE // 2, 1), pltpu.roll(x, MLA_ROPE // 2, 1))
    return x * c + partner * s


def _win_kernel(w_ref, wb_ref):
    wb_ref[...] = w_ref[...].astype(BF16)


def _win(wt):
    rows = pl.BlockSpec((IN_COLS // WIN_STEPS, D_MODEL), lambda i: (i, 0))
    return pl.pallas_call(
        _win_kernel,
        grid=(WIN_STEPS,),
        in_specs=[rows],
        out_specs=rows,
        out_shape=jax.ShapeDtypeStruct((IN_COLS, D_MODEL), BF16),
        compiler_params=pltpu.CompilerParams(dimension_semantics=("arbitrary",),
                                             vmem_limit_bytes=VMEM_LIMIT),
        name="win",
    )(wt)


def _proj_kernel(x_ref, g_ref, wt_ref, qg_ref, wuq_ref, kvg_ref, wukv_ref, cos_ref, sin_ref,
                 q_ref, k_ref, v_ref, nq_ref, nk_ref, nv_ref, cq_ref, ckv_ref, kr_ref, *, n_tiles):
    tm = x_ref.shape[0]
    i = pl.program_id(0)

    def stage1():
        h = _rms(x_ref[...], g_ref[...]).astype(BF16)

        def proj(lo, hi):
            return _dot_nt(h, wt_ref[lo:hi, :])

        cq_ref[...] = proj(_R_CQ, _R_CKV)
        ckv_ref[...] = proj(_R_CKV, _R_KR)
        wkr = wt_ref[_R_KR:_R_NQ, :]
        kr_ref[...] = _dot_nt(h, jnp.concatenate([wkr, wkr], axis=0))
        nq_ref[...] = (proj(_R_NQ, _R_NK) * (LOG2_E * NA_HEAD_DIM ** -0.5)).astype(BF16)
        nk_ref[...] = proj(_R_NK, _R_NV).astype(BF16)
        nv_ref[...] = proj(_R_NV, IN_COLS).astype(BF16)

    def stage2():
        c32, s32 = cos_ref[...], sin_ref[...]
        cos = jnp.concatenate([c32, c32] * (LANES // MLA_ROPE), axis=1)
        sin = jnp.concatenate([-s32, s32] * (LANES // MLA_ROPE), axis=1)
        lane = lax.broadcasted_iota(jnp.int32, (tm, LANES), 1)
        lo_half = (lane % MLA_ROPE) < (MLA_ROPE // 2)

        cqn = _rms(cq_ref[...], qg_ref[...]).astype(BF16)
        q = _dot(cqn, wuq_ref[...]) * (LOG2_E * MLA_QK ** -0.5)
        rope0 = HEADS * MLA_NOPE
        for pair in range(HEADS // 2):
            qr = _rope_lanes(q[:, rope0 + pair * LANES: rope0 + (pair + 1) * LANES], cos, sin, lo_half)
            qr = qr.astype(BF16)
            for hh in (2 * pair, 2 * pair + 1):
                q_ref[hh, :, 0:LANES] = q[:, hh * MLA_NOPE:(hh + 1) * MLA_NOPE].astype(BF16)
                q_ref[hh, :, LANES:QK_PAD] = qr

        ckvn = _rms(ckv_ref[...], kvg_ref[...]).astype(BF16)
        kv = _dot(ckvn, wukv_ref[...])
        kr = _rope_lanes(kr_ref[...], cos, sin, lo_half)
        kr_even = jnp.where(lane < MLA_ROPE, kr, 0.0).astype(BF16)
        kr_odd = jnp.where(lane >= MLA_ROPE, kr, 0.0).astype(BF16)
        v0 = HEADS * MLA_NOPE
        ones_col = jnp.where(lane == 0, 1.0, 0.0).astype(BF16)
        for hh in range(HEADS):
            k_ref[hh, :, 0:LANES] = kv[:, hh * MLA_NOPE:(hh + 1) * MLA_NOPE].astype(BF16)
            k_ref[hh, :, LANES:QK_PAD] = kr_even if hh % 2 == 0 else kr_odd
            v_ref[hh, :, 0:MLA_V] = kv[:, v0 + hh * MLA_V: v0 + (hh + 1) * MLA_V].astype(BF16)
            v_ref[hh, :, MLA_V:V_PAD] = ones_col

    pl.when(i == 0)(stage1)

    @pl.when((i > 0) & (i < n_tiles))
    def _():
        stage2()
        stage1()

    pl.when(i == n_tiles)(stage2)


def _proj(x, g, wt, qg, wuq, kvg, wukv, cos, sin, tm=512):
    n_tiles = SEQ // tm
    cur = lambda i: jnp.minimum(i, n_tiles - 1)
    prev = lambda i: jnp.maximum(i - 1, 0)
    row = lambda w, tile: pl.BlockSpec((tm, w), lambda i: (tile(i), 0))
    head = lambda w: pl.BlockSpec((HEADS, tm, w), lambda i: (0, prev(i), 0))
    return pl.pallas_call(
        functools.partial(_proj_kernel, n_tiles=n_tiles),
        grid=(n_tiles + 1,),
        in_specs=[row(D_MODEL, cur), _const_spec((1, D_MODEL)), _const_spec((IN_COLS, D_MODEL)),
                  _const_spec((1, MLA_Q_RANK)), _const_spec((MLA_Q_RANK, HEADS * MLA_QK)),
                  _const_spec((1, MLA_KV_RANK)), _const_spec((MLA_KV_RANK, HEADS * (MLA_NOPE + MLA_V))),
                  row(MLA_ROPE // 2, prev), row(MLA_ROPE // 2, prev)],
        out_specs=[head(QK_PAD), head(QK_PAD), head(V_PAD),
                   row(NA_WIDTH, cur), row(NA_WIDTH, cur), row(NA_WIDTH, cur)],
        out_shape=[jax.ShapeDtypeStruct((HEADS, SEQ, QK_PAD), BF16),
                   jax.ShapeDtypeStruct((HEADS, SEQ, QK_PAD), BF16),
                   jax.ShapeDtypeStruct((HEADS, SEQ, V_PAD), BF16),
                   jax.ShapeDtypeStruct((SEQ, NA_WIDTH), BF16),
                   jax.ShapeDtypeStruct((SEQ, NA_WIDTH), BF16),
                   jax.ShapeDtypeStruct((SEQ, NA_WIDTH), BF16)],
        scratch_shapes=[pltpu.VMEM((tm, MLA_Q_RANK), F32), pltpu.VMEM((tm, MLA_KV_RANK), F32),
                        pltpu.VMEM((tm, LANES), F32)],
        compiler_params=pltpu.CompilerParams(dimension_semantics=("arbitrary",),
                                             vmem_limit_bytes=VMEM_LIMIT),
        name="proj",
    )(x, g, wt, qg, wuq, kvg, wukv, cos, sin)


MLA_STRIP = 64


def _mla_stream(n_sub, tk):
    nk = SEQ // tk
    return [(sub, c * tk, tk, c == nk - 1) for sub in range(n_sub) for c in range(nk)]


def _mla_kernel(q_ref, k_ref, v_ref, w1_ref, w2_ref, wo_ref, o_ref, w1b_ref, w2b_ref, wob_ref,
                s0_ref, s1_ref, p0_ref, p1_ref, a0_ref, a1_ref, acc_ref, m_ref, *, tq, tk):
    w1b_ref[...] = w1_ref[...].astype(BF16)
    w2b_ref[...] = w2_ref[...].astype(BF16)
    wob_ref[...] = wo_ref[...].astype(BF16)
    stream = _mla_stream(q_ref.shape[0] // tq, tk)
    total = len(stream)
    s_refs, p_refs, a_refs = (s0_ref, s1_ref), (p0_ref, p1_ref), (a0_ref, a1_ref)

    def scores(u):
        sub, start, size, _ = stream[u]
        s_refs[u % 2][:, :size] = _dot_nt(q_ref[sub * tq:(sub + 1) * tq, :], k_ref[start:start + size, :])

    def softmax(u):
        sub, _, size, _ = stream[u]
        s_ref, p_ref, a_ref = s_refs[u % 2], p_refs[u % 2], a_refs[u % 2]
        for r in range(tq // MLA_STRIP):
            rows = slice(r * MLA_STRIP, (r + 1) * MLA_STRIP)
            mrows = slice(sub * tq + r * MLA_STRIP, sub * tq + (r + 1) * MLA_STRIP)
            tiles = [s_ref[rows, c * LANES:(c + 1) * LANES] for c in range(size // LANES)]
            tile_max = functools.reduce(jnp.maximum, tiles)
            m_old = m_ref[mrows, :]
            m_new = jnp.maximum(m_old, jnp.max(tile_max, axis=-1, keepdims=True))
            m_ref[mrows, :] = m_new
            a_ref[rows, :] = jnp.exp2(m_old - m_new)
            for c, s in enumerate(tiles):
                p_ref[rows, c * LANES:(c + 1) * LANES] = jnp.exp2(s - m_new).astype(BF16)

    def accumulate(u):
        sub, start, size, last = stream[u]
        rows = slice(sub * tq, (sub + 1) * tq)
        pv = _dot(p_refs[u % 2][:, :size], v_ref[start:start + size, :])
        alpha = a_refs[u % 2][...]
        acc = [alpha * acc_ref[rows, c * LANES:(c + 1) * LANES] + pv[:, c * LANES:(c + 1) * LANES]
               for c in range(V_PAD // LANES)]
        if last:
            o_ref[rows, :] = acc[0] / acc[1][:, 0:1]
        else:
            for c in range(V_PAD // LANES):
                acc_ref[rows, c * LANES:(c + 1) * LANES] = acc[c]

    m_ref[...] = jnp.full(m_ref.shape, -jnp.inf, F32)
    acc_ref[...] = jnp.zeros(acc_ref.shape, F32)
    for t in range(total + 2):
        if 1 <= t <= total:
            softmax(t - 1)
        if t < total:
            scores(t)
        if t >= 2:
            accumulate(t - 2)


def _mla(q, k, v, w1, w2, wo, tq=512, tk=512, n_sub=2):
    tb = n_sub * tq
    nb = SEQ // tb
    ff_slice = D_FF // (HEADS * nb)
    wo_slice = (MLA_WIDTH + NA_WIDTH) // (HEADS * nb)
    assert ff_slice % LANES == 0 and wo_slice % 16 == 0
    step = lambda h, i: h * nb + i
    return pl.pallas_call(
        functools.partial(_mla_kernel, tq=tq, tk=tk),
        grid=(HEADS, nb),
        in_specs=[pl.BlockSpec((None, tb, QK_PAD), lambda h, i: (h, i, 0)),
                  pl.BlockSpec((None, SEQ, QK_PAD), lambda h, i: (h, 0, 0)),
                  pl.BlockSpec((None, SEQ, V_PAD), lambda h, i: (h, 0, 0)),
                  pl.BlockSpec((D_MODEL, ff_slice), lambda h, i: (0, step(h, i))),
                  pl.BlockSpec((ff_slice, D_MODEL), lambda h, i: (step(h, i), 0)),
                  pl.BlockSpec((wo_slice, D_MODEL), lambda h, i: (step(h, i), 0))],
        out_specs=[pl.BlockSpec((tb, MLA_V), lambda h, i: (i, h)),
                   pl.BlockSpec((D_MODEL, ff_slice), lambda h, i: (0, step(h, i))),
                   pl.BlockSpec((ff_slice, D_MODEL), lambda h, i: (step(h, i), 0)),
                   pl.BlockSpec((wo_slice, D_MODEL), lambda h, i: (step(h, i), 0))],
        out_shape=[jax.ShapeDtypeStruct((SEQ, MLA_WIDTH), F32),
                   jax.ShapeDtypeStruct((D_MODEL, D_FF), BF16),
                   jax.ShapeDtypeStruct((D_FF, D_MODEL), BF16),
                   jax.ShapeDtypeStruct((MLA_WIDTH + NA_WIDTH, D_MODEL), BF16)],
        scratch_shapes=[pltpu.VMEM((tq, tk), F32), pltpu.VMEM((tq, tk), F32),
                        pltpu.VMEM((tq, tk), BF16), pltpu.VMEM((tq, tk), BF16),
                        pltpu.VMEM((tq, LANES), F32), pltpu.VMEM((tq, LANES), F32),
                        pltpu.VMEM((tb, V_PAD), F32), pltpu.VMEM((tb, LANES), F32)],
        compiler_params=pltpu.CompilerParams(dimension_semantics=("arbitrary", "arbitrary"),
                                             vmem_limit_bytes=VMEM_LIMIT),
        name="mla",
    )(q, k, v, w1, w2, wo)


NA_QROWS = 4
NA_KROWS = 12
NA_TQ = NA_QROWS * GRID_W
NA_TK = NA_KROWS * GRID_W
NA_STEPS = GRID_ROWS // NA_QROWS
NA_BIAS_ROWS = 2 * NA_ROWS - 1
NA_BIAS_COLS = 2 * NA_COLS - 1
NA_STRIP = 32


def _na_bias_plan(variant):
    plan = {}
    for a in range(NA_QROWS):
        for b in range(NA_KROWS):
            if variant == "first":
                dr, ok = b - a + NA_ROWS - 1, b < NA_ROWS
            elif variant == "last":
                dr, ok = b - a - 1, b >= NA_KROWS - NA_ROWS
            else:
                dr, ok = b - a + NA_ROWS // 2 - 1, a <= b < a + NA_ROWS
            plan[(a, b)] = dr if ok else None
    return plan


def _na_kernel(rpb_ref, q_ref, k0_ref, k1_ref, k2_ref, v0_ref, v1_ref, v2_ref, o_ref, t_ref, bias_ref,
               s0_ref, s1_ref, p0_ref, p1_ref, l0_ref, l1_ref):
    g = pl.program_id(0)
    s_refs, p_refs, l_refs = (s0_ref, s1_ref), (p0_ref, p1_ref), (l0_ref, l1_ref)

    @pl.when(g == 0)
    def _():
        c = lax.broadcasted_iota(jnp.int32, (GRID_W, GRID_W), 0)
        kc = lax.broadcasted_iota(jnp.int32, (GRID_W, GRID_W), 1)
        start = jnp.clip(c - NA_COLS // 2, 0, GRID_W - NA_COLS)
        dc = jnp.where((kc >= start) & (kc < start + NA_COLS), kc - c + NA_COLS - 1, -1)

        def per_row(i, carry):
            tile = jnp.full((GRID_W, GRID_W), NEG_BIG, F32)
            for d in range(NA_BIAS_COLS):
                tile = jnp.where(dc == d, rpb_ref[i * NA_BIAS_COLS + d] * LOG2_E, tile)
            t_ref[i] = tile
            return carry

        lax.fori_loop(0, HEADS * NA_BIAS_ROWS, per_row, 0)

    def build(variant):
        plan = _na_bias_plan(variant)

        def per_head(h, carry):
            for (a, b), dr in plan.items():
                if dr is None:
                    tile = jnp.full((GRID_W, GRID_W), NEG_BIG, F32)
                else:
                    tile = t_ref[h * NA_BIAS_ROWS + dr]
                bias_ref[h, a * GRID_W:(a + 1) * GRID_W, b * GRID_W:(b + 1) * GRID_W] = tile
            return carry

        lax.fori_loop(0, HEADS, per_head, 0)

    pl.when(g == 0)(lambda: build("first"))
    pl.when(g == 1)(lambda: build("mid"))
    pl.when(g == NA_STEPS - 1)(lambda: build("last"))

    k_refs = (k0_ref, k1_ref, k2_ref)
    v_refs = (v0_ref, v1_ref, v2_ref)
    for h in range(HEADS):
        cols = slice(h * NA_HEAD_DIM, (h + 1) * NA_HEAD_DIM)
        s_ref, p_ref, l_ref = s_refs[h % 2], p_refs[h % 2], l_refs[h % 2]
        qh = q_ref[:, cols]
        for t in range(3):
            s_ref[:, t * NA_TQ:(t + 1) * NA_TQ] = _dot_nt(qh, k_refs[t][:, cols])
        for r in range(NA_TQ // NA_STRIP):
            rows = slice(r * NA_STRIP, (r + 1) * NA_STRIP)
            s = s_ref[rows, :] + bias_ref[h, rows, :]
            p = jnp.exp2(s - jnp.max(s, axis=-1, keepdims=True))
            l_ref[rows, :] = jnp.broadcast_to(jnp.sum(p, axis=-1, keepdims=True), (NA_STRIP, LANES))
            p_ref[rows, :] = p.astype(BF16)
        o = _dot(p_ref[:, 0:NA_TQ], v_refs[0][:, cols])
        for t in (1, 2):
            o = o + _dot(p_ref[:, t * NA_TQ:(t + 1) * NA_TQ], v_refs[t][:, cols])
        o_ref[:, cols] = o / l_ref[...]


def _na(rpb, nq, nk, nv):
    first_blk = lambda g: jnp.clip(g - 1, 0, NA_STEPS - 3)
    kv_specs = [pl.BlockSpec((NA_TQ, NA_WIDTH), functools.partial(lambda g, t: (first_blk(g) + t, 0), t=t))
                for t in range(3)]
    return pl.pallas_call(
        _na_kernel,
        grid=(NA_STEPS,),
        in_specs=[pl.BlockSpec(memory_space=pltpu.SMEM), pl.BlockSpec((NA_TQ, NA_WIDTH), lambda g: (g, 0))]
                 + kv_specs + kv_specs,
        out_specs=pl.BlockSpec((NA_TQ, NA_WIDTH), lambda g: (g, 0)),
        out_shape=jax.ShapeDtypeStruct((SEQ, NA_WIDTH), F32),
        scratch_shapes=[pltpu.VMEM((HEADS * NA_BIAS_ROWS, GRID_W, GRID_W), F32),
                        pltpu.VMEM((HEADS, NA_TQ, NA_TK), F32),
                        pltpu.VMEM((NA_TQ, NA_TK), F32), pltpu.VMEM((NA_TQ, NA_TK), F32),
                        pltpu.VMEM((NA_TQ, NA_TK), BF16), pltpu.VMEM((NA_TQ, NA_TK), BF16),
                        pltpu.VMEM((NA_TQ, LANES), F32), pltpu.VMEM((NA_TQ, LANES), F32)],
        compiler_params=pltpu.CompilerParams(dimension_semantics=("arbitrary",),
                                             vmem_limit_bytes=VMEM_LIMIT),
        name="na",
    )(rpb.astype(F32).reshape(-1), nq, nk, nk, nk, nv, nv, nv)


def _outproj_kernel(a_ref, n_ref, ga_ref, gn_ref, wb_ref, x_ref, gm_ref, o_ref, h_ref):
    an = _rms(a_ref[...], ga_ref[...]).astype(BF16)
    nn = _rms(n_ref[...], gn_ref[...]).astype(BF16)
    y = x_ref[...] + _dot(an, wb_ref[0:MLA_WIDTH, :]) + _dot(nn, wb_ref[MLA_WIDTH:, :])
    o_ref[...] = y
    h_ref[...] = _rms(y, gm_ref[...]).astype(BF16)


def _outproj(a, n, ga, gn, w, x, gm, tm=512):
    row = lambda w_: pl.BlockSpec((tm, w_), lambda i: (i, 0))
    return pl.pallas_call(
        _outproj_kernel,
        grid=(SEQ // tm,),
        in_specs=[row(MLA_WIDTH), row(NA_WIDTH), _const_spec((1, MLA_WIDTH)), _const_spec((1, NA_WIDTH)),
                  _const_spec((MLA_WIDTH + NA_WIDTH, D_MODEL)), row(D_MODEL), _const_spec((1, D_MODEL))],
        out_specs=[row(D_MODEL), row(D_MODEL)],
        out_shape=[jax.ShapeDtypeStruct((SEQ, D_MODEL), F32), jax.ShapeDtypeStruct((SEQ, D_MODEL), BF16)],
        compiler_params=pltpu.CompilerParams(dimension_semantics=("arbitrary",),
                                             vmem_limit_bytes=VMEM_LIMIT),
        name="outproj",
    )(a, n, ga, gn, w, x, gm)


def _mlp_kernel(x_ref, h_ref, w1_ref, w2_ref, gf_ref, o_ref, acc_ref):
    j = pl.program_id(1)
    last = pl.num_programs(1) - 1

    def chunk():
        hid = jnp.maximum(_dot(h_ref[...], w1_ref[...]), 0.0)
        return _dot((hid * hid).astype(BF16), w2_ref[...])

    @pl.when(j == 0)
    def _():
        acc_ref[...] = x_ref[...] + chunk()

    @pl.when((j > 0) & (j < last))
    def _():
        acc_ref[...] += chunk()

    @pl.when(j == last)
    def _():
        o_ref[...] = _rms(acc_ref[...] + chunk(), gf_ref[...])


def _mlp(x, h, w1, w2, gf, tm=512, tf=1024):
    assert D_FF // tf >= 2
    tile = lambda: pl.BlockSpec((tm, D_MODEL), lambda i, j: (i, 0))
    return pl.pallas_call(
        _mlp_kernel,
        grid=(SEQ // tm, D_FF // tf),
        in_specs=[tile(), tile(),
                  pl.BlockSpec((D_MODEL, tf), lambda i, j: (0, j)),
                  pl.BlockSpec((tf, D_MODEL), lambda i, j: (j, 0)),
                  _const_spec((1, D_MODEL))],
        out_specs=tile(),
        out_shape=jax.ShapeDtypeStruct((SEQ, D_MODEL), F32),
        scratch_shapes=[pltpu.VMEM((tm, D_MODEL), F32)],
        compiler_params=pltpu.CompilerParams(dimension_semantics=("arbitrary", "arbitrary"),
                                             vmem_limit_bytes=VMEM_LIMIT),
        name="mlp",
    )(x, h, w1, w2, gf)


def _rope_tables():
    half = MLA_ROPE // 2
    inv = ROPE_THETA ** (-jnp.arange(half, dtype=F32) / half)
    ang_a = (jnp.arange(SEQ // ROPE_SPLIT) * ROPE_SPLIT).astype(F32)[:, None] * inv[None, :]
    ang_b = jnp.arange(ROPE_SPLIT).astype(F32)[:, None] * inv[None, :]
    ca, sa = jnp.cos(ang_a)[:, None, :], jnp.sin(ang_a)[:, None, :]
    cb, sb = jnp.cos(ang_b)[None, :, :], jnp.sin(ang_b)[None, :, :]
    return (ca * cb - sa * sb).reshape(SEQ, half), (sa * cb + ca * sb).reshape(SEQ, half)


def kernel(x, attn_norm_g, w_in, q_norm_g, w_uq, kv_norm_g, w_ukv, na_rpb, mla_out_norm_g, na_out_norm_g,
           w_out, mlp_norm_g, w_ff1, w_ff2, final_norm_g):
    assert x.shape == (1, SEQ, D_MODEL)
    assert w_in.shape[0] == 1, "the final norm is fused into the (single) layer's MLP kernel"
    cos, sin = _rope_tables()
    xs = x[0]
    for l in range(1):
        wt = _win(jnp.swapaxes(w_in[l], 0, 1))
        wq = w_uq[l].reshape(MLA_Q_RANK, HEADS, MLA_QK)
        wuq = jnp.concatenate([wq[:, :, :MLA_NOPE].reshape(MLA_Q_RANK, HEADS * MLA_NOPE),
                               wq[:, :, MLA_NOPE:].reshape(MLA_Q_RANK, HEADS * MLA_ROPE)], axis=1).astype(BF16)
        wk = w_ukv[l].reshape(MLA_KV_RANK, HEADS, MLA_NOPE + MLA_V)
        wukv = jnp.concatenate([wk[:, :, :MLA_NOPE].reshape(MLA_KV_RANK, HEADS * MLA_NOPE),
                                wk[:, :, MLA_NOPE:].reshape(MLA_KV_RANK, HEADS * MLA_V)], axis=1).astype(BF16)

        q, k, v, nq, nk, nv = _proj(xs, attn_norm_g[l][None], wt, q_norm_g[l][None], wuq,
                                    kv_norm_g[l][None], wukv, cos, sin)
        a_out, w1b, w2b, wob = _mla(q, k, v, w_ff1[l], w_ff2[l], w_out[l])
        n_out = _na(na_rpb[l], nq, nk, nv)
        xs, hs = _outproj(a_out, n_out, mla_out_norm_g[l][None], na_out_norm_g[l][None],
                          wob, xs, mlp_norm_g[l][None])
        xs = _mlp(xs, hs, w1b, w2b, final_norm_g[None])
    return xs[None]
```

```python
import functools

import jax
import jax.numpy as jnp
from jax import lax
from jax.experimental import pallas as pl
from jax.experimental.pallas import tpu as pltpu

F32 = jnp.float32
BF16 = jnp.bfloat16

D_MODEL = 2048
SEQ = 8192
GRID_W = 64
GRID_ROWS = SEQ // GRID_W
HEADS = 8
MLA_NOPE = 128
MLA_ROPE = 64
MLA_V = 128
MLA_QK = MLA_NOPE + MLA_ROPE
MLA_Q_RANK = 512
MLA_KV_RANK = 256
ROPE_THETA = 10000.0
NA_HEAD_DIM = 128
NA_ROWS = 8
NA_COLS = 16
NA_WIDTH = HEADS * NA_HEAD_DIM
MLA_WIDTH = HEADS * MLA_V
D_FF = 4 * D_MODEL
NORM_EPS = 1e-6

LANES = 128
QK_PAD = 2 * LANES
V_PAD = 2 * LANES
LOG2_E = 1.4426950408889634
ROPE_SPLIT = 64
NEG_BIG = -1e30

_R_CQ = 0
_R_CKV = _R_CQ + MLA_Q_RANK
_R_KR = _R_CKV + MLA_KV_RANK
_R_NQ = _R_KR + MLA_ROPE
_R_NK = _R_NQ + NA_WIDTH
_R_NV = _R_NK + NA_WIDTH
IN_COLS = _R_NV + NA_WIDTH
WIN_STEPS = 4

VMEM_LIMIT = 56 * 1024 * 1024


def _rms(xf, g):
    y = xf * lax.rsqrt(jnp.mean(xf * xf, axis=-1, keepdims=True) + NORM_EPS)
    return y * g


def _dot(a, b):
    return jnp.dot(a, b, preferred_element_type=F32)


def _dot_nt(a, b):
    return lax.dot_general(a, b, (((1,), (1,)), ((), ())), preferred_element_type=F32)


def _const_spec(shape):
    nd = len(shape)
    return pl.BlockSpec(shape, lambda *_: (0,) * nd, pipeline_mode=pl.Buffered(1))


def _rope_lanes(x, c, s, lo_half):
    partner = jnp.where(lo_half, pltpu.roll(x, LANES - MLA_ROPE // 2, 1), pltpu.roll(x, MLA_ROPE // 2, 1))
    return x * c + partner * s


def _win_kernel(w_ref, wb_ref):
    wb_ref[...] = w_ref[...].astype(BF16)


def _win(wt):
    rows = pl.BlockSpec((IN_COLS // WIN_STEPS, D_MODEL), lambda i: (i, 0))
    return pl.pallas_call(
        _win_kernel,
        grid=(WIN_STEPS,),
        in_specs=[rows],
        out_specs=rows,
        out_shape=jax.ShapeDtypeStruct((IN_COLS, D_MODEL), BF16),
        compiler_params=pltpu.CompilerParams(dimension_semantics=("arbitrary",),
                                             vmem_limit_bytes=VMEM_LIMIT),
        name="win",
    )(wt)


def _proj_kernel(x_ref, g_ref, wt_ref, qg_ref, wuq_ref, kvg_ref, wukv_ref, cos_ref, sin_ref,
                 q_ref, k_ref, v_ref, nq_ref, nk_ref, nv_ref, cq_ref, ckv_ref, kr_ref, *, n_tiles):
    tm = x_ref.shape[0]
    i = pl.program_id(0)

    def stage1():
        h = _rms(x_ref[...], g_ref[...]).astype(BF16)

        def proj(lo, hi):
            return _dot_nt(h, wt_ref[lo:hi, :])

        cq_ref[...] = proj(_R_CQ, _R_CKV)
        ckv_ref[...] = proj(_R_CKV, _R_KR)
        wkr = wt_ref[_R_KR:_R_NQ, :]
        kr_ref[...] = _dot_nt(h, jnp.concatenate([wkr, wkr], axis=0))
        nq_ref[...] = (proj(_R_NQ, _R_NK) * (LOG2_E * NA_HEAD_DIM ** -0.5)).astype(BF16)
        nk_ref[...] = proj(_R_NK, _R_NV).astype(BF16)
        nv_ref[...] = proj(_R_NV, IN_COLS).astype(BF16)

    def stage2():
        c32, s32 = cos_ref[...], sin_ref[...]
        cos = jnp.concatenate([c32, c32] * (LANES // MLA_ROPE), axis=1)
        sin = jnp.concatenate([-s32, s32] * (LANES // MLA_ROPE), axis=1)
        lane = lax.broadcasted_iota(jnp.int32, (tm, LANES), 1)
        lo_half = (lane % MLA_ROPE) < (MLA_ROPE // 2)

        cqn = _rms(cq_ref[...], qg_ref[...]).astype(BF16)
        q = _dot(cqn, wuq_ref[...]) * (LOG2_E * MLA_QK ** -0.5)
        rope0 = HEADS * MLA_NOPE
        for pair in range(HEADS // 2):
            qr = _rope_lanes(q[:, rope0 + pair * LANES: rope0 + (pair + 1) * LANES], cos, sin, lo_half)
            qr = qr.astype(BF16)
            for hh in (2 * pair, 2 * pair + 1):
                q_ref[hh, :, 0:LANES] = q[:, hh * MLA_NOPE:(hh + 1) * MLA_NOPE].astype(BF16)
                q_ref[hh, :, LANES:QK_PAD] = qr

        ckvn = _rms(ckv_ref[...], kvg_ref[...]).astype(BF16)
        kv = _dot(ckvn, wukv_ref[...])
        kr = _rope_lanes(kr_ref[...], cos, sin, lo_half)
        kr_even = jnp.where(lane < MLA_ROPE, kr, 0.0).astype(BF16)
        kr_odd = jnp.where(lane >= MLA_ROPE, kr, 0.0).astype(BF16)
        v0 = HEADS * MLA_NOPE
        ones_col = jnp.where(lane == 0, 1.0, 0.0).astype(BF16)
        for hh in range(HEADS):
            k_ref[hh, :, 0:LANES] = kv[:, hh * MLA_NOPE:(hh + 1) * MLA_NOPE].astype(BF16)
            k_ref[hh, :, LANES:QK_PAD] = kr_even if hh % 2 == 0 else kr_odd
            v_ref[hh, :, 0:MLA_V] = kv[:, v0 + hh * MLA_V: v0 + (hh + 1) * MLA_V].astype(BF16)
            v_ref[hh, :, MLA_V:V_PAD] = ones_col

    pl.when(i == 0)(stage1)

    @pl.when((i > 0) & (i < n_tiles))
    def _():
        stage2()
        stage1()

    pl.when(i == n_tiles)(stage2)


def _proj(x, g, wt, qg, wuq, kvg, wukv, cos, sin, tm=512):
    n_tiles = SEQ // tm
    cur = lambda i: jnp.minimum(i, n_tiles - 1)
    prev = lambda i: jnp.maximum(i - 1, 0)
    row = lambda w, tile: pl.BlockSpec((tm, w), lambda i: (tile(i), 0))
    head = lambda w: pl.BlockSpec((HEADS, tm, w), lambda i: (0, prev(i), 0))
    return pl.pallas_call(
        functools.partial(_proj_kernel, n_tiles=n_tiles),
        grid=(n_tiles + 1,),
        in_specs=[row(D_MODEL, cur), _const_spec((1, D_MODEL)), _const_spec((IN_COLS, D_MODEL)),
                  _const_spec((1, MLA_Q_RANK)), _const_spec((MLA_Q_RANK, HEADS * MLA_QK)),
                  _const_spec((1, MLA_KV_RANK)), _const_spec((MLA_KV_RANK, HEADS * (MLA_NOPE + MLA_V))),
                  row(MLA_ROPE // 2, prev), row(MLA_ROPE // 2, prev)],
        out_specs=[head(QK_PAD), head(QK_PAD), head(V_PAD),
                   row(NA_WIDTH, cur), row(NA_WIDTH, cur), row(NA_WIDTH, cur)],
        out_shape=[jax.ShapeDtypeStruct((HEADS, SEQ, QK_PAD), BF16),
                   jax.ShapeDtypeStruct((HEADS, SEQ, QK_PAD), BF16),
                   jax.ShapeDtypeStruct((HEADS, SEQ, V_PAD), BF16),
                   jax.ShapeDtypeStruct((SEQ, NA_WIDTH), BF16),
                   jax.ShapeDtypeStruct((SEQ, NA_WIDTH), BF16),
                   jax.ShapeDtypeStruct((SEQ, NA_WIDTH), BF16)],
        scratch_shapes=[pltpu.VMEM((tm, MLA_Q_RANK), F32), pltpu.VMEM((tm, MLA_KV_RANK), F32),
                        pltpu.VMEM((tm, LANES), F32)],
        compiler_params=pltpu.CompilerParams(dimension_semantics=("arbitrary",),
                                             vmem_limit_bytes=VMEM_LIMIT),
        name="proj",
    )(x, g, wt, qg, wuq, kvg, wukv, cos, sin)


MLA_BOUND_SLACK = 1.0 + 2.0 ** -8
MLA_L_MIN = 2.0 ** -60
MLA_NORM_ROWS = 1024


def _mla_kernel(q_ref, k_ref, v_ref, w1_ref, w2_ref, wo_ref, o_ref, w1b_ref, w2b_ref, wob_ref,
                p_ref, kmax_ref, *, tq, tk):
    w1b_ref[...] = w1_ref[...].astype(BF16)
    w2b_ref[...] = w2_ref[...].astype(BF16)
    wob_ref[...] = wo_ref[...].astype(BF16)
    n_sub = q_ref.shape[0] // tq
    nk = SEQ // tk

    @pl.when(pl.program_id(1) == 0)
    def _():
        def scan(c, best):
            kc = k_ref[pl.ds(pl.multiple_of(c * MLA_NORM_ROWS, MLA_NORM_ROWS), MLA_NORM_ROWS), :].astype(F32)
            norms = jnp.sum(kc * kc, axis=-1, keepdims=True)
            return jnp.maximum(best, jnp.max(norms, axis=0, keepdims=True))

        best = lax.fori_loop(0, SEQ // MLA_NORM_ROWS, scan, jnp.zeros((1, 1), F32))
        kmax_ref[...] = jnp.broadcast_to(best, kmax_ref.shape)

    def scores(sub, c):
        return _dot_nt(q_ref[sub * tq:(sub + 1) * tq, :], k_ref[c * tk:(c + 1) * tk, :])

    def attend(sub, shift):
        for c in range(nk):
            s = scores(sub, c)
            for j in range(tk // LANES):
                p_ref[sub, :, c * tk + j * LANES:c * tk + (j + 1) * LANES] = jnp.exp2(
                    s[:, j * LANES:(j + 1) * LANES] - shift).astype(BF16)
        acc = _dot(p_ref[sub], v_ref[...])
        return acc[:, :MLA_V], acc[:, MLA_V:MLA_V + 1]

    denominators = []
    for sub in range(n_sub):
        qf = q_ref[sub * tq:(sub + 1) * tq, :].astype(F32)
        qsq = jnp.sum(qf * qf, axis=-1, keepdims=True)
        bound = jnp.sqrt(qsq * kmax_ref[0:1, 0:1]) * MLA_BOUND_SLACK
        num, den = attend(sub, jnp.broadcast_to(bound, (tq, LANES)))
        o_ref[sub * tq:(sub + 1) * tq, :] = num / den
        denominators.append(jnp.min(den))

    for sub in range(n_sub):
        @pl.when(denominators[sub] < MLA_L_MIN)
        def _():
            m = jnp.full((tq, 1), -jnp.inf, F32)
            for c in range(nk):
                m = jnp.maximum(m, jnp.max(scores(sub, c), axis=-1, keepdims=True))
            num, den = attend(sub, jnp.broadcast_to(m, (tq, LANES)))
            o_ref[sub * tq:(sub + 1) * tq, :] = num / den


def _mla(q, k, v, w1, w2, wo, tq=512, tk=512, n_sub=2):
    tb = n_sub * tq
    nb = SEQ // tb
    ff_slice = D_FF // (HEADS * nb)
    wo_slice = (MLA_WIDTH + NA_WIDTH) // (HEADS * nb)
    assert ff_slice % LANES == 0 and wo_slice % 16 == 0
    step = lambda h, i: h * nb + i
    return pl.pallas_call(
        functools.partial(_mla_kernel, tq=tq, tk=tk),
        grid=(HEADS, nb),
        in_specs=[pl.BlockSpec((None, tb, QK_PAD), lambda h, i: (h, i, 0)),
                  pl.BlockSpec((None, SEQ, QK_PAD), lambda h, i: (h, 0, 0)),
                  pl.BlockSpec((None, SEQ, V_PAD), lambda h, i: (h, 0, 0)),
                  pl.BlockSpec((D_MODEL, ff_slice), lambda h, i: (0, step(h, i))),
                  pl.BlockSpec((ff_slice, D_MODEL), lambda h, i: (step(h, i), 0)),
                  pl.BlockSpec((wo_slice, D_MODEL), lambda h, i: (step(h, i), 0))],
        out_specs=[pl.BlockSpec((tb, MLA_V), lambda h, i: (i, h)),
                   pl.BlockSpec((D_MODEL, ff_slice), lambda h, i: (0, step(h, i))),
                   pl.BlockSpec((ff_slice, D_MODEL), lambda h, i: (step(h, i), 0)),
                   pl.BlockSpec((wo_slice, D_MODEL), lambda h, i: (step(h, i), 0))],
        out_shape=[jax.ShapeDtypeStruct((SEQ, MLA_WIDTH), F32),
                   jax.ShapeDtypeStruct((D_MODEL, D_FF), BF16),
                   jax.ShapeDtypeStruct((D_FF, D_MODEL), BF16),
                   jax.ShapeDtypeStruct((MLA_WIDTH + NA_WIDTH, D_MODEL), BF16)],
        scratch_shapes=[pltpu.VMEM((n_sub, tq, SEQ), BF16), pltpu.VMEM((8, LANES), F32)],
        compiler_params=pltpu.CompilerParams(dimension_semantics=("arbitrary", "arbitrary"),
                                             vmem_limit_bytes=VMEM_LIMIT),
        name="mla",
    )(q, k, v, w1, w2, wo)


NA_QROWS = 4
NA_KROWS = 12
NA_TQ = NA_QROWS * GRID_W
NA_TK = NA_KROWS * GRID_W
NA_STEPS = GRID_ROWS // NA_QROWS
NA_BIAS_ROWS = 2 * NA_ROWS - 1
NA_BIAS_COLS = 2 * NA_COLS - 1
NA_STRIP = 32


def _na_bias_plan(variant):
    plan = {}
    for a in range(NA_QROWS):
        for b in range(NA_KROWS):
            if variant == "first":
                dr, ok = b - a + NA_ROWS - 1, b < NA_ROWS
            elif variant == "last":
                dr, ok = b - a - 1, b >= NA_KROWS - NA_ROWS
            else:
                dr, ok = b - a + NA_ROWS // 2 - 1, a <= b < a + NA_ROWS
            plan[(a, b)] = dr if ok else None
    return plan


def _na_kernel(rpb_ref, q_ref, k0_ref, k1_ref, k2_ref, v0_ref, v1_ref, v2_ref, o_ref, t_ref, bias_ref,
               s0_ref, s1_ref, p0_ref, p1_ref, l0_ref, l1_ref):
    g = pl.program_id(0)
    s_refs, p_refs, l_refs = (s0_ref, s1_ref), (p0_ref, p1_ref), (l0_ref, l1_ref)

    @pl.when(g == 0)
    def _():
        c = lax.broadcasted_iota(jnp.int32, (GRID_W, GRID_W), 0)
        kc = lax.broadcasted_iota(jnp.int32, (GRID_W, GRID_W), 1)
        start = jnp.clip(c - NA_COLS // 2, 0, GRID_W - NA_COLS)
        dc = jnp.where((kc >= start) & (kc < start + NA_COLS), kc - c + NA_COLS - 1, -1)

        def per_row(i, carry):
            tile = jnp.full((GRID_W, GRID_W), NEG_BIG, F32)
            for d in range(NA_BIAS_COLS):
                tile = jnp.where(dc == d, rpb_ref[i * NA_BIAS_COLS + d] * LOG2_E, tile)
            t_ref[i] = tile
            return carry

        lax.fori_loop(0, HEADS * NA_BIAS_ROWS, per_row, 0)

    def build(variant):
        plan = _na_bias_plan(variant)

        def per_head(h, carry):
            for (a, b), dr in plan.items():
                if dr is None:
                    tile = jnp.full((GRID_W, GRID_W), NEG_BIG, F32)
                else:
                    tile = t_ref[h * NA_BIAS_ROWS + dr]
                bias_ref[h, a * GRID_W:(a + 1) * GRID_W, b * GRID_W:(b + 1) * GRID_W] = tile
            return carry

        lax.fori_loop(0, HEADS, per_head, 0)

    pl.when(g == 0)(lambda: build("first"))
    pl.when(g == 1)(lambda: build("mid"))
    pl.when(g == NA_STEPS - 1)(lambda: build("last"))

    k_refs = (k0_ref, k1_ref, k2_ref)
    v_refs = (v0_ref, v1_ref, v2_ref)
    for h in range(HEADS):
        cols = slice(h * NA_HEAD_DIM, (h + 1) * NA_HEAD_DIM)
        s_ref, p_ref, l_ref = s_refs[h % 2], p_refs[h % 2], l_refs[h % 2]
        qh = q_ref[:, cols]
        for t in range(3):
            s_ref[:, t * NA_TQ:(t + 1) * NA_TQ] = _dot_nt(qh, k_refs[t][:, cols])
        for r in range(NA_TQ // NA_STRIP):
            rows = slice(r * NA_STRIP, (r + 1) * NA_STRIP)
            s = s_ref[rows, :] + bias_ref[h, rows, :]
            p = jnp.exp2(s - jnp.max(s, axis=-1, keepdims=True))
            l_ref[rows, :] = jnp.broadcast_to(jnp.sum(p, axis=-1, keepdims=True), (NA_STRIP, LANES))
            p_ref[rows, :] = p.astype(BF16)
        o = _dot(p_ref[:, 0:NA_TQ], v_refs[0][:, cols])
        for t in (1, 2):
            o = o + _dot(p_ref[:, t * NA_TQ:(t + 1) * NA_TQ], v_refs[t][:, cols])
        o_ref[:, cols] = o / l_ref[...]


def _na(rpb, nq, nk, nv):
    first_blk = lambda g: jnp.clip(g - 1, 0, NA_STEPS - 3)
    kv_specs = [pl.BlockSpec((NA_TQ, NA_WIDTH), functools.partial(lambda g, t: (first_blk(g) + t, 0), t=t))
                for t in range(3)]
    return pl.pallas_call(
        _na_kernel,
        grid=(NA_STEPS,),
        in_specs=[pl.BlockSpec(memory_space=pltpu.SMEM), pl.BlockSpec((NA_TQ, NA_WIDTH), lambda g: (g, 0))]
                 + kv_specs + kv_specs,
        out_specs=pl.BlockSpec((NA_TQ, NA_WIDTH), lambda g: (g, 0)),
        out_shape=jax.ShapeDtypeStruct((SEQ, NA_WIDTH), F32),
        scratch_shapes=[pltpu.VMEM((HEADS * NA_BIAS_ROWS, GRID_W, GRID_W), F32),
                        pltpu.VMEM((HEADS, NA_TQ, NA_TK), F32),
                        pltpu.VMEM((NA_TQ, NA_TK), F32), pltpu.VMEM((NA_TQ, NA_TK), F32),
                        pltpu.VMEM((NA_TQ, NA_TK), BF16), pltpu.VMEM((NA_TQ, NA_TK), BF16),
                        pltpu.VMEM((NA_TQ, LANES), F32), pltpu.VMEM((NA_TQ, LANES), F32)],
        compiler_params=pltpu.CompilerParams(dimension_semantics=("arbitrary",),
                                             vmem_limit_bytes=VMEM_LIMIT),
        name="na",
    )(rpb.astype(F32).reshape(-1), nq, nk, nk, nk, nv, nv, nv)


def _outproj_kernel(a_ref, n_ref, ga_ref, gn_ref, wb_ref, x_ref, gm_ref, o_ref, h_ref):
    an = _rms(a_ref[...], ga_ref[...]).astype(BF16)
    nn = _rms(n_ref[...], gn_ref[...]).astype(BF16)
    y = x_ref[...] + _dot(an, wb_ref[0:MLA_WIDTH, :]) + _dot(nn, wb_ref[MLA_WIDTH:, :])
    o_ref[...] = y
    h_ref[...] = _rms(y, gm_ref[...]).astype(BF16)


def _outproj(a, n, ga, gn, w, x, gm, tm=512):
    row = lambda w_: pl.BlockSpec((tm, w_), lambda i: (i, 0))
    return pl.pallas_call(
        _outproj_kernel,
        grid=(SEQ // tm,),
        in_specs=[row(MLA_WIDTH), row(NA_WIDTH), _const_spec((1, MLA_WIDTH)), _const_spec((1, NA_WIDTH)),
                  _const_spec((MLA_WIDTH + NA_WIDTH, D_MODEL)), row(D_MODEL), _const_spec((1, D_MODEL))],
        out_specs=[row(D_MODEL), row(D_MODEL)],
        out_shape=[jax.ShapeDtypeStruct((SEQ, D_MODEL), F32), jax.ShapeDtypeStruct((SEQ, D_MODEL), BF16)],
        compiler_params=pltpu.CompilerParams(dimension_semantics=("arbitrary",),
                                             vmem_limit_bytes=VMEM_LIMIT),
        name="outproj",
    )(a, n, ga, gn, w, x, gm)


def _mlp_kernel(x_ref, h_ref, w1_ref, w2_ref, gf_ref, o_ref, acc_ref):
    j = pl.program_id(1)
    last = pl.num_programs(1) - 1

    def chunk():
        hid = jnp.maximum(_dot(h_ref[...], w1_ref[...]), 0.0)
        return _dot((hid * hid).astype(BF16), w2_ref[...])

    @pl.when(j == 0)
    def _():
        acc_ref[...] = x_ref[...] + chunk()

    @pl.when((j > 0) & (j < last))
    def _():
        acc_ref[...] += chunk()

    @pl.when(j == last)
    def _():
        o_ref[...] = _rms(acc_ref[...] + chunk(), gf_ref[...])


def _mlp(x, h, w1, w2, gf, tm=512, tf=1024):
    assert D_FF // tf >= 2
    tile = lambda: pl.BlockSpec((tm, D_MODEL), lambda i, j: (i, 0))
    return pl.pallas_call(
        _mlp_kernel,
        grid=(SEQ // tm, D_FF // tf),
        in_specs=[tile(), tile(),
                  pl.BlockSpec((D_MODEL, tf), lambda i, j: (0, j)),
                  pl.BlockSpec((tf, D_MODEL), lambda i, j: (j, 0)),
                  _const_spec((1, D_MODEL))],
        out_specs=tile(),
        out_shape=jax.ShapeDtypeStruct((SEQ, D_MODEL), F32),
        scratch_shapes=[pltpu.VMEM((tm, D_MODEL), F32)],
        compiler_params=pltpu.CompilerParams(dimension_semantics=("arbitrary", "arbitrary"),
                                             vmem_limit_bytes=VMEM_LIMIT),
        name="mlp",
    )(x, h, w1, w2, gf)


def _rope_tables():
    half = MLA_ROPE // 2
    inv = ROPE_THETA ** (-jnp.arange(half, dtype=F32) / half)
    ang_a = (jnp.arange(SEQ // ROPE_SPLIT) * ROPE_SPLIT).astype(F32)[:, None] * inv[None, :]
    ang_b = jnp.arange(ROPE_SPLIT).astype(F32)[:, None] * inv[None, :]
    ca, sa = jnp.cos(ang_a)[:, None, :], jnp.sin(ang_a)[:, None, :]
    cb, sb = jnp.cos(ang_b)[None, :, :], jnp.sin(ang_b)[None, :, :]
    return (ca * cb - sa * sb).reshape(SEQ, half), (sa * cb + ca * sb).reshape(SEQ, half)


def kernel(x, attn_norm_g, w_in, q_norm_g, w_uq, kv_norm_g, w_ukv, na_rpb, mla_out_norm_g, na_out_norm_g,
           w_out, mlp_norm_g, w_ff1, w_ff2, final_norm_g):
    assert x.shape == (1, SEQ, D_MODEL)
    assert w_in.shape[0] == 1, "the final norm is fused into the (single) layer's MLP kernel"
    cos, sin = _rope_tables()
    xs = x[0]
    for l in range(1):
        wt = _win(jnp.swapaxes(w_in[l], 0, 1))
        wq = w_uq[l].reshape(MLA_Q_RANK, HEADS, MLA_QK)
        wuq = jnp.concatenate([wq[:, :, :MLA_NOPE].reshape(MLA_Q_RANK, HEADS * MLA_NOPE),
                               wq[:, :, MLA_NOPE:].reshape(MLA_Q_RANK, HEADS * MLA_ROPE)], axis=1).astype(BF16)
        wk = w_ukv[l].reshape(MLA_KV_RANK, HEADS, MLA_NOPE + MLA_V)
        wukv = jnp.concatenate([wk[:, :, :MLA_NOPE].reshape(MLA_KV_RANK, HEADS * MLA_NOPE),
                                wk[:, :, MLA_NOPE:].reshape(MLA_KV_RANK, HEADS * MLA_V)], axis=1).astype(BF16)

        q, k, v, nq, nk, nv = _proj(xs, attn_norm_g[l][None], wt, q_norm_g[l][None], wuq,
                                    kv_norm_g[l][None], wukv, cos, sin)
        a_out, w1b, w2b, wob = _mla(q, k, v, w_ff1[l], w_ff2[l], w_out[l])
        n_out = _na(na_rpb[l], nq, nk, nv)
        xs, hs = _outproj(a_out, n_out, mla_out_norm_g[l][None], na_out_norm_g[l][None],
                          wob, xs, mlp_norm_g[l][None])
        xs = _mlp(xs, hs, w1b, w2b, final_norm_g[None])
    return xs[None]
```

```python
import functools

import jax
import jax.numpy as jnp
from jax import lax
from jax.experimental import pallas as pl
from jax.experimental.pallas import tpu as pltpu

F32 = jnp.float32
BF16 = jnp.bfloat16

D_MODEL = 2048
SEQ = 8192
GRID_W = 64
GRID_ROWS = SEQ // GRID_W
HEADS = 8
MLA_NOPE = 128
MLA_ROPE = 64
MLA_V = 128
MLA_QK = MLA_NOPE + MLA_ROPE
MLA_Q_RANK = 512
MLA_KV_RANK = 256
ROPE_THETA = 10000.0
NA_HEAD_DIM = 128
NA_ROWS = 8
NA_COLS = 16
NA_WIDTH = HEADS * NA_HEAD_DIM
MLA_WIDTH = HEADS * MLA_V
D_FF = 4 * D_MODEL
NORM_EPS = 1e-6

LANES = 128
QK_PAD = 2 * LANES
VT_ROWS = MLA_V + 16
LOG2_E = 1.4426950408889634
ROPE_SPLIT = 64
NEG_BIG = -1e30

_R_CQ = 0
_R_CKV = _R_CQ + MLA_Q_RANK
_R_KR = _R_CKV + MLA_KV_RANK
_R_NQ = _R_KR + MLA_ROPE
_R_NK = _R_NQ + NA_WIDTH
_R_NV = _R_NK + NA_WIDTH
IN_COLS = _R_NV + NA_WIDTH
WIN_STEPS = 4

VMEM_LIMIT = 56 * 1024 * 1024


def _rms(xf, g):
    y = xf * lax.rsqrt(jnp.mean(xf * xf, axis=-1, keepdims=True) + NORM_EPS)
    return y * g


def _dot(a, b):
    return jnp.dot(a, b, preferred_element_type=F32)


def _dot_nt(a, b):
    return lax.dot_general(a, b, (((1,), (1,)), ((), ())), preferred_element_type=F32)


def _const_spec(shape):
    nd = len(shape)
    return pl.BlockSpec(shape, lambda *_: (0,) * nd, pipeline_mode=pl.Buffered(1))


def _rope_lanes(x, c, s, lo_half):
    partner = jnp.where(lo_half, pltpu.roll(x, LANES - MLA_ROPE // 2, 1), pltpu.roll(x, MLA_ROPE // 2, 1))
    return x * c + partner * s


def _win_kernel(w_ref, wb_ref):
    wb_ref[...] = w_ref[...].astype(BF16)


def _win(wt):
    rows = pl.BlockSpec((IN_COLS // WIN_STEPS, D_MODEL), lambda i: (i, 0))
    return pl.pallas_call(
        _win_kernel,
        grid=(WIN_STEPS,),
        in_specs=[rows],
        out_specs=rows,
        out_shape=jax.ShapeDtypeStruct((IN_COLS, D_MODEL), BF16),
        compiler_params=pltpu.CompilerParams(dimension_semantics=("arbitrary",),
                                             vmem_limit_bytes=VMEM_LIMIT),
        name="win",
    )(wt)


def _proj_kernel(x_ref, g_ref, wt_ref, qg_ref, wuq_ref, kvg_ref, wuk_ref, wvt_ref, cos_ref, sin_ref,
                 q_ref, k_ref, vt_ref, nq_ref, nk_ref, nv_ref, cq_ref, ckv_ref, kr_ref, *, n_tiles):
    tm = x_ref.shape[0]
    i = pl.program_id(0)

    def stage1():
        h = _rms(x_ref[...], g_ref[...]).astype(BF16)

        def proj(lo, hi):
            return _dot_nt(h, wt_ref[lo:hi, :])

        cq_ref[...] = proj(_R_CQ, _R_CKV)
        ckv_ref[...] = proj(_R_CKV, _R_KR)
        wkr = wt_ref[_R_KR:_R_NQ, :]
        kr_ref[...] = _dot_nt(h, jnp.concatenate([wkr, wkr], axis=0))
        nq_ref[...] = (proj(_R_NQ, _R_NK) * (LOG2_E * NA_HEAD_DIM ** -0.5)).astype(BF16)
        nk_ref[...] = proj(_R_NK, _R_NV).astype(BF16)
        nv_ref[...] = proj(_R_NV, IN_COLS).astype(BF16)

    def stage2():
        c32, s32 = cos_ref[...], sin_ref[...]
        cos = jnp.concatenate([c32, c32] * (LANES // MLA_ROPE), axis=1)
        sin = jnp.concatenate([-s32, s32] * (LANES // MLA_ROPE), axis=1)
        lane = lax.broadcasted_iota(jnp.int32, (tm, LANES), 1)
        lo_half = (lane % MLA_ROPE) < (MLA_ROPE // 2)

        cqn = _rms(cq_ref[...], qg_ref[...]).astype(BF16)
        q = _dot(cqn, wuq_ref[...]) * (LOG2_E * MLA_QK ** -0.5)
        rope0 = HEADS * MLA_NOPE
        for pair in range(HEADS // 2):
            qr = _rope_lanes(q[:, rope0 + pair * LANES: rope0 + (pair + 1) * LANES], cos, sin, lo_half)
            qr = qr.astype(BF16)
            for hh in (2 * pair, 2 * pair + 1):
                q_ref[hh, :, 0:LANES] = q[:, hh * MLA_NOPE:(hh + 1) * MLA_NOPE].astype(BF16)
                q_ref[hh, :, LANES:QK_PAD] = qr

        ckvn = _rms(ckv_ref[...], kvg_ref[...])
        kn = _dot(ckvn.astype(BF16), wuk_ref[...])
        vt = _dot(wvt_ref[...], jnp.transpose(ckvn).astype(BF16))
        kr = _rope_lanes(kr_ref[...], cos, sin, lo_half)
        kr_even = jnp.where(lane < MLA_ROPE, kr, 0.0).astype(BF16)
        kr_odd = jnp.where(lane >= MLA_ROPE, kr, 0.0).astype(BF16)
        ones_row = jnp.where(lax.broadcasted_iota(jnp.int32, (VT_ROWS - MLA_V, tm), 0) == 0, 1.0, 0.0).astype(BF16)
        for hh in range(HEADS):
            k_ref[hh, :, 0:LANES] = kn[:, hh * MLA_NOPE:(hh + 1) * MLA_NOPE].astype(BF16)
            k_ref[hh, :, LANES:QK_PAD] = kr_even if hh % 2 == 0 else kr_odd
            vt_ref[hh, 0:MLA_V, :] = vt[hh * MLA_V:(hh + 1) * MLA_V, :].astype(BF16)
            vt_ref[hh, MLA_V:VT_ROWS, :] = ones_row

    pl.when(i == 0)(stage1)

    @pl.when((i > 0) & (i < n_tiles))
    def _():
        stage2()
        stage1()

    pl.when(i == n_tiles)(stage2)


def _proj(x, g, wt, qg, wuq, kvg, wuk, wvt, cos, sin, tm=512):
    n_tiles = SEQ // tm
    cur = lambda i: jnp.minimum(i, n_tiles - 1)
    prev = lambda i: jnp.maximum(i - 1, 0)
    row = lambda w, tile: pl.BlockSpec((tm, w), lambda i: (tile(i), 0))
    head = lambda w: pl.BlockSpec((HEADS, tm, w), lambda i: (0, prev(i), 0))
    return pl.pallas_call(
        functools.partial(_proj_kernel, n_tiles=n_tiles),
        grid=(n_tiles + 1,),
        in_specs=[row(D_MODEL, cur), _const_spec((1, D_MODEL)), _const_spec((IN_COLS, D_MODEL)),
                  _const_spec((1, MLA_Q_RANK)), _const_spec((MLA_Q_RANK, HEADS * MLA_QK)),
                  _const_spec((1, MLA_KV_RANK)), _const_spec((MLA_KV_RANK, HEADS * MLA_NOPE)),
                  _const_spec((HEADS * MLA_V, MLA_KV_RANK)),
                  row(MLA_ROPE // 2, prev), row(MLA_ROPE // 2, prev)],
        out_specs=[head(QK_PAD), head(QK_PAD),
                   pl.BlockSpec((HEADS, VT_ROWS, tm), lambda i: (0, 0, prev(i))),
                   row(NA_WIDTH, cur), row(NA_WIDTH, cur), row(NA_WIDTH, cur)],
        out_shape=[jax.ShapeDtypeStruct((HEADS, SEQ, QK_PAD), BF16),
                   jax.ShapeDtypeStruct((HEADS, SEQ, QK_PAD), BF16),
                   jax.ShapeDtypeStruct((HEADS, VT_ROWS, SEQ), BF16),
                   jax.ShapeDtypeStruct((SEQ, NA_WIDTH), BF16),
                   jax.ShapeDtypeStruct((SEQ, NA_WIDTH), BF16),
                   jax.ShapeDtypeStruct((SEQ, NA_WIDTH), BF16)],
        scratch_shapes=[pltpu.VMEM((tm, MLA_Q_RANK), F32), pltpu.VMEM((tm, MLA_KV_RANK), F32),
                        pltpu.VMEM((tm, LANES), F32)],
        compiler_params=pltpu.CompilerParams(dimension_semantics=("arbitrary",),
                                             vmem_limit_bytes=VMEM_LIMIT),
        name="proj",
    )(x, g, wt, qg, wuq, kvg, wuk, wvt, cos, sin)


MLA_BOUND_SLACK = 1.0 + 2.0 ** -6
MLA_L_MIN = 2.0 ** -60
MLA_NORM_ROWS = 1024


def _mla_kernel(q_ref, k_ref, vt_ref, w1_ref, w2_ref, wo_ref, o_ref, w1b_ref, w2b_ref, wob_ref,
                pt_ref, kmax_ref, *, tq, tk):
    w1b_ref[...] = w1_ref[...].astype(BF16)
    w2b_ref[...] = w2_ref[...].astype(BF16)
    wob_ref[...] = wo_ref[...].astype(BF16)
    n_sub = q_ref.shape[0] // tq
    nk = SEQ // tk

    @pl.when(pl.program_id(1) == 0)
    def _():
        def scan(c, best):
            kc = k_ref[pl.ds(pl.multiple_of(c * MLA_NORM_ROWS, MLA_NORM_ROWS), MLA_NORM_ROWS), :].astype(F32)
            norms = jnp.sum(kc * kc, axis=-1, keepdims=True)
            return jnp.maximum(best, jnp.max(norms, axis=0, keepdims=True))

        best = lax.fori_loop(0, SEQ // MLA_NORM_ROWS, scan, jnp.zeros((1, 1), F32))
        kmax_ref[...] = jnp.broadcast_to(best, kmax_ref.shape)

    def scores_t(sub, c):
        return _dot_nt(k_ref[c * tk:(c + 1) * tk, :], q_ref[sub * tq:(sub + 1) * tq, :])

    def attend(sub, shift):
        for c in range(nk):
            pt_ref[sub, c * tk:(c + 1) * tk, :] = jnp.exp2(scores_t(sub, c) - shift).astype(BF16)
        acc = _dot(vt_ref[...], pt_ref[sub])
        return acc[:MLA_V, :], acc[MLA_V:MLA_V + 1, :]

    ones = jnp.ones((8, QK_PAD), BF16)
    denominators = []
    for sub in range(n_sub):
        rows = slice(sub * tq, (sub + 1) * tq)
        qf = q_ref[rows, :].astype(F32)
        qsq = _dot_nt(ones, (qf * qf).astype(BF16))[0:1, :]
        bound = jnp.sqrt(qsq * kmax_ref[0:1, 0:1]) * MLA_BOUND_SLACK
        num, den = attend(sub, bound)
        o_ref[rows, :] = jnp.transpose(num / den)
        denominators.append(jnp.min(den))

    for sub in range(n_sub):
        @pl.when(denominators[sub] < MLA_L_MIN)
        def _():
            m = jnp.full((1, tq), -jnp.inf, F32)
            for c in range(nk):
                m = jnp.maximum(m, jnp.max(scores_t(sub, c), axis=0, keepdims=True))
            num, den = attend(sub, m)
            o_ref[sub * tq:(sub + 1) * tq, :] = jnp.transpose(num / den)


def _mla(q, k, v, w1, w2, wo, tq=512, tk=512, n_sub=2):
    tb = n_sub * tq
    nb = SEQ // tb
    ff_slice = D_FF // (HEADS * nb)
    wo_slice = (MLA_WIDTH + NA_WIDTH) // (HEADS * nb)
    assert ff_slice % LANES == 0 and wo_slice % 16 == 0
    step = lambda h, i: h * nb + i
    return pl.pallas_call(
        functools.partial(_mla_kernel, tq=tq, tk=tk),
        grid=(HEADS, nb),
        in_specs=[pl.BlockSpec((None, tb, QK_PAD), lambda h, i: (h, i, 0)),
                  pl.BlockSpec((None, SEQ, QK_PAD), lambda h, i: (h, 0, 0)),
                  pl.BlockSpec((None, VT_ROWS, SEQ), lambda h, i: (h, 0, 0)),
                  pl.BlockSpec((D_MODEL, ff_slice), lambda h, i: (0, step(h, i))),
                  pl.BlockSpec((ff_slice, D_MODEL), lambda h, i: (step(h, i), 0)),
                  pl.BlockSpec((wo_slice, D_MODEL), lambda h, i: (step(h, i), 0))],
        out_specs=[pl.BlockSpec((tb, MLA_V), lambda h, i: (i, h)),
                   pl.BlockSpec((D_MODEL, ff_slice), lambda h, i: (0, step(h, i))),
                   pl.BlockSpec((ff_slice, D_MODEL), lambda h, i: (step(h, i), 0)),
                   pl.BlockSpec((wo_slice, D_MODEL), lambda h, i: (step(h, i), 0))],
        out_shape=[jax.ShapeDtypeStruct((SEQ, MLA_WIDTH), F32),
                   jax.ShapeDtypeStruct((D_MODEL, D_FF), BF16),
                   jax.ShapeDtypeStruct((D_FF, D_MODEL), BF16),
                   jax.ShapeDtypeStruct((MLA_WIDTH + NA_WIDTH, D_MODEL), BF16)],
        scratch_shapes=[pltpu.VMEM((n_sub, SEQ, tq), BF16), pltpu.VMEM((8, LANES), F32)],
        compiler_params=pltpu.CompilerParams(dimension_semantics=("arbitrary", "arbitrary"),
                                             vmem_limit_bytes=VMEM_LIMIT),
        name="mla",
    )(q, k, v, w1, w2, wo)


NA_QROWS = 4
NA_KROWS = 12
NA_TQ = NA_QROWS * GRID_W
NA_TK = NA_KROWS * GRID_W
NA_STEPS = GRID_ROWS // NA_QROWS
NA_BIAS_ROWS = 2 * NA_ROWS - 1
NA_BIAS_COLS = 2 * NA_COLS - 1
NA_STRIP = 32


def _na_bias_plan(variant):
    plan = {}
    for a in range(NA_QROWS):
        for b in range(NA_KROWS):
            if variant == "first":
                dr, ok = b - a + NA_ROWS - 1, b < NA_ROWS
            elif variant == "last":
                dr, ok = b - a - 1, b >= NA_KROWS - NA_ROWS
            else:
                dr, ok = b - a + NA_ROWS // 2 - 1, a <= b < a + NA_ROWS
            plan[(a, b)] = dr if ok else None
    return plan


def _na_kernel(rpb_ref, q_ref, k0_ref, k1_ref, k2_ref, v0_ref, v1_ref, v2_ref, o_ref, t_ref, bias_ref,
               s0_ref, s1_ref, p0_ref, p1_ref, l0_ref, l1_ref):
    g = pl.program_id(0)
    s_refs, p_refs, l_refs = (s0_ref, s1_ref), (p0_ref, p1_ref), (l0_ref, l1_ref)

    @pl.when(g == 0)
    def _():
        c = lax.broadcasted_iota(jnp.int32, (GRID_W, GRID_W), 0)
        kc = lax.broadcasted_iota(jnp.int32, (GRID_W, GRID_W), 1)
        start = jnp.clip(c - NA_COLS // 2, 0, GRID_W - NA_COLS)
        dc = jnp.where((kc >= start) & (kc < start + NA_COLS), kc - c + NA_COLS - 1, -1)

        def per_row(i, carry):
            tile = jnp.full((GRID_W, GRID_W), NEG_BIG, F32)
            for d in range(NA_BIAS_COLS):
                tile = jnp.where(dc == d, rpb_ref[i * NA_BIAS_COLS + d] * LOG2_E, tile)
            t_ref[i] = tile
            return carry

        lax.fori_loop(0, HEADS * NA_BIAS_ROWS, per_row, 0)

    def build(variant):
        plan = _na_bias_plan(variant)

        def per_head(h, carry):
            for (a, b), dr in plan.items():
                if dr is None:
                    tile = jnp.full((GRID_W, GRID_W), NEG_BIG, F32)
                else:
                    tile = t_ref[h * NA_BIAS_ROWS + dr]
                bias_ref[h, a * GRID_W:(a + 1) * GRID_W, b * GRID_W:(b + 1) * GRID_W] = tile
            return carry

        lax.fori_loop(0, HEADS, per_head, 0)

    pl.when(g == 0)(lambda: build("first"))
    pl.when(g == 1)(lambda: build("mid"))
    pl.when(g == NA_STEPS - 1)(lambda: build("last"))

    k_refs = (k0_ref, k1_ref, k2_ref)
    v_refs = (v0_ref, v1_ref, v2_ref)
    for h in range(HEADS):
        cols = slice(h * NA_HEAD_DIM, (h + 1) * NA_HEAD_DIM)
        s_ref, p_ref, l_ref = s_refs[h % 2], p_refs[h % 2], l_refs[h % 2]
        qh = q_ref[:, cols]
        for t in range(3):
            s_ref[:, t * NA_TQ:(t + 1) * NA_TQ] = _dot_nt(qh, k_refs[t][:, cols])
        for r in range(NA_TQ // NA_STRIP):
            rows = slice(r * NA_STRIP, (r + 1) * NA_STRIP)
            s = s_ref[rows, :] + bias_ref[h, rows, :]
            p = jnp.exp2(s - jnp.max(s, axis=-1, keepdims=True))
            l_ref[rows, :] = jnp.broadcast_to(jnp.sum(p, axis=-1, keepdims=True), (NA_STRIP, LANES))
            p_ref[rows, :] = p.astype(BF16)
        o = _dot(p_ref[:, 0:NA_TQ], v_refs[0][:, cols])
        for t in (1, 2):
            o = o + _dot(p_ref[:, t * NA_TQ:(t + 1) * NA_TQ], v_refs[t][:, cols])
        o_ref[:, cols] = o / l_ref[...]


def _na(rpb, nq, nk, nv):
    first_blk = lambda g: jnp.clip(g - 1, 0, NA_STEPS - 3)
    kv_specs = [pl.BlockSpec((NA_TQ, NA_WIDTH), functools.partial(lambda g, t: (first_blk(g) + t, 0), t=t))
                for t in range(3)]
    return pl.pallas_call(
        _na_kernel,
        grid=(NA_STEPS,),
        in_specs=[pl.BlockSpec(memory_space=pltpu.SMEM), pl.BlockSpec((NA_TQ, NA_WIDTH), lambda g: (g, 0))]
                 + kv_specs + kv_specs,
        out_specs=pl.BlockSpec((NA_TQ, NA_WIDTH), lambda g: (g, 0)),
        out_shape=jax.ShapeDtypeStruct((SEQ, NA_WIDTH), F32),
        scratch_shapes=[pltpu.VMEM((HEADS * NA_BIAS_ROWS, GRID_W, GRID_W), F32),
                        pltpu.VMEM((HEADS, NA_TQ, NA_TK), F32),
                        pltpu.VMEM((NA_TQ, NA_TK), F32), pltpu.VMEM((NA_TQ, NA_TK), F32),
                        pltpu.VMEM((NA_TQ, NA_TK), BF16), pltpu.VMEM((NA_TQ, NA_TK), BF16),
                        pltpu.VMEM((NA_TQ, LANES), F32), pltpu.VMEM((NA_TQ, LANES), F32)],
        compiler_params=pltpu.CompilerParams(dimension_semantics=("arbitrary",),
                                             vmem_limit_bytes=VMEM_LIMIT),
        name="na",
    )(rpb.astype(F32).reshape(-1), nq, nk, nk, nk, nv, nv, nv)


def _outproj_kernel(a_ref, n_ref, ga_ref, gn_ref, wb_ref, x_ref, gm_ref, o_ref, h_ref):
    an = _rms(a_ref[...], ga_ref[...]).astype(BF16)
    nn = _rms(n_ref[...], gn_ref[...]).astype(BF16)
    y = x_ref[...] + _dot(an, wb_ref[0:MLA_WIDTH, :]) + _dot(nn, wb_ref[MLA_WIDTH:, :])
    o_ref[...] = y
    h_ref[...] = _rms(y, gm_ref[...]).astype(BF16)


def _outproj(a, n, ga, gn, w, x, gm, tm=512):
    row = lambda w_: pl.BlockSpec((tm, w_), lambda i: (i, 0))
    return pl.pallas_call(
        _outproj_kernel,
        grid=(SEQ // tm,),
        in_specs=[row(MLA_WIDTH), row(NA_WIDTH), _const_spec((1, MLA_WIDTH)), _const_spec((1, NA_WIDTH)),
                  _const_spec((MLA_WIDTH + NA_WIDTH, D_MODEL)), row(D_MODEL), _const_spec((1, D_MODEL))],
        out_specs=[row(D_MODEL), row(D_MODEL)],
        out_shape=[jax.ShapeDtypeStruct((SEQ, D_MODEL), F32), jax.ShapeDtypeStruct((SEQ, D_MODEL), BF16)],
        compiler_params=pltpu.CompilerParams(dimension_semantics=("arbitrary",),
                                             vmem_limit_bytes=VMEM_LIMIT),
        name="outproj",
    )(a, n, ga, gn, w, x, gm)


def _mlp_kernel(x_ref, h_ref, w1_ref, w2_ref, gf_ref, o_ref, acc_ref):
    j = pl.program_id(1)
    last = pl.num_programs(1) - 1

    def chunk():
        hid = jnp.maximum(_dot(h_ref[...], w1_ref[...]), 0.0)
        return _dot((hid * hid).astype(BF16), w2_ref[...])

    @pl.when(j == 0)
    def _():
        acc_ref[...] = x_ref[...] + chunk()

    @pl.when((j > 0) & (j < last))
    def _():
        acc_ref[...] += chunk()

    @pl.when(j == last)
    def _():
        o_ref[...] = _rms(acc_ref[...] + chunk(), gf_ref[...])


def _mlp(x, h, w1, w2, gf, tm=512, tf=1024):
    assert D_FF // tf >= 2
    tile = lambda: pl.BlockSpec((tm, D_MODEL), lambda i, j: (i, 0))
    return pl.pallas_call(
        _mlp_kernel,
        grid=(SEQ // tm, D_FF // tf),
        in_specs=[tile(), tile(),
                  pl.BlockSpec((D_MODEL, tf), lambda i, j: (0, j)),
                  pl.BlockSpec((tf, D_MODEL), lambda i, j: (j, 0)),
                  _const_spec((1, D_MODEL))],
        out_specs=tile(),
        out_shape=jax.ShapeDtypeStruct((SEQ, D_MODEL), F32),
        scratch_shapes=[pltpu.VMEM((tm, D_MODEL), F32)],
        compiler_params=pltpu.CompilerParams(dimension_semantics=("arbitrary", "arbitrary"),
                                             vmem_limit_bytes=VMEM_LIMIT),
        name="mlp",
    )(x, h, w1, w2, gf)


def _rope_tables():
    half = MLA_ROPE // 2
    inv = ROPE_THETA ** (-jnp.arange(half, dtype=F32) / half)
    ang_a = (jnp.arange(SEQ // ROPE_SPLIT) * ROPE_SPLIT).astype(F32)[:, None] * inv[None, :]
    ang_b = jnp.arange(ROPE_SPLIT).astype(F32)[:, None] * inv[None, :]
    ca, sa = jnp.cos(ang_a)[:, None, :], jnp.sin(ang_a)[:, None, :]
    cb, sb = jnp.cos(ang_b)[None, :, :], jnp.sin(ang_b)[None, :, :]
    return (ca * cb - sa * sb).reshape(SEQ, half), (sa * cb + ca * sb).reshape(SEQ, half)


def kernel(x, attn_norm_g, w_in, q_norm_g, w_uq, kv_norm_g, w_ukv, na_rpb, mla_out_norm_g, na_out_norm_g,
           w_out, mlp_norm_g, w_ff1, w_ff2, final_norm_g):
    assert x.shape == (1, SEQ, D_MODEL)
    assert w_in.shape[0] == 1, "the final norm is fused into the (single) layer's MLP kernel"
    cos, sin = _rope_tables()
    xs = x[0]
    for l in range(1):
        wt = _win(jnp.swapaxes(w_in[l], 0, 1))
        wq = w_uq[l].reshape(MLA_Q_RANK, HEADS, MLA_QK)
        wuq = jnp.concatenate([wq[:, :, :MLA_NOPE].reshape(MLA_Q_RANK, HEADS * MLA_NOPE),
                               wq[:, :, MLA_NOPE:].reshape(MLA_Q_RANK, HEADS * MLA_ROPE)], axis=1).astype(BF16)
        wk = w_ukv[l].reshape(MLA_KV_RANK, HEADS, MLA_NOPE + MLA_V)
        wuk = wk[:, :, :MLA_NOPE].reshape(MLA_KV_RANK, HEADS * MLA_NOPE).astype(BF16)
        wvt = jnp.transpose(wk[:, :, MLA_NOPE:].reshape(MLA_KV_RANK, HEADS * MLA_V)).astype(BF16)

        q, k, vt, nq, nk, nv = _proj(xs, attn_norm_g[l][None], wt, q_norm_g[l][None], wuq,
                                    kv_norm_g[l][None], wuk, wvt, cos, sin)
        a_out, w1b, w2b, wob = _mla(q, k, vt, w_ff1[l], w_ff2[l], w_out[l])
        n_out = _na(na_rpb[l], nq, nk, nv)
        xs, hs = _outproj(a_out, n_out, mla_out_norm_g[l][None], na_out_norm_g[l][None],
                          wob, xs, mlp_norm_g[l][None])
        xs = _mlp(xs, hs, w1b, w2b, final_norm_g[None])
    return xs[None]
```

```python
import functools

import jax
import jax.numpy as jnp
from jax import lax
from jax.experimental import pallas as pl
from jax.experimental.pallas import tpu as pltpu

F32 = jnp.float32
BF16 = jnp.bfloat16

D_MODEL = 2048
SEQ = 8192
GRID_W = 64
GRID_ROWS = SEQ // GRID_W
HEADS = 8
MLA_NOPE = 128
MLA_ROPE = 64
MLA_V = 128
MLA_QK = MLA_NOPE + MLA_ROPE
MLA_Q_RANK = 512
MLA_KV_RANK = 256
ROPE_THETA = 10000.0
NA_HEAD_DIM = 128
NA_ROWS = 8
NA_COLS = 16
NA_WIDTH = HEADS * NA_HEAD_DIM
MLA_WIDTH = HEADS * MLA_V
D_FF = 4 * D_MODEL
NORM_EPS = 1e-6

LANES = 128
QK_PAD = 2 * LANES
VT_ROWS = MLA_V + 16
LOG2_E = 1.4426950408889634
ROPE_SPLIT = 64
NEG_BIG = -1e30

_R_CQ = 0
_R_CKV = _R_CQ + MLA_Q_RANK
_R_KR = _R_CKV + MLA_KV_RANK
_R_NQ = _R_KR + MLA_ROPE
_R_NK = _R_NQ + NA_WIDTH
_R_NV = _R_NK + NA_WIDTH
IN_COLS = _R_NV + NA_WIDTH
WIN_STEPS = 4

VMEM_LIMIT = 56 * 1024 * 1024


def _rms(xf, g):
    y = xf * lax.rsqrt(jnp.mean(xf * xf, axis=-1, keepdims=True) + NORM_EPS)
    return y * g


def _dot(a, b):
    return jnp.dot(a, b, preferred_element_type=F32)


def _dot_nt(a, b):
    return lax.dot_general(a, b, (((1,), (1,)), ((), ())), preferred_element_type=F32)


def _const_spec(shape):
    nd = len(shape)
    return pl.BlockSpec(shape, lambda *_: (0,) * nd, pipeline_mode=pl.Buffered(1))


def _rope_lanes(x, c, s, lo_half):
    partner = jnp.where(lo_half, pltpu.roll(x, LANES - MLA_ROPE // 2, 1), pltpu.roll(x, MLA_ROPE // 2, 1))
    return x * c + partner * s


def _win_kernel(w_ref, wb_ref):
    wb_ref[...] = w_ref[...].astype(BF16)


def _win(wt):
    rows = pl.BlockSpec((IN_COLS // WIN_STEPS, D_MODEL), lambda i: (i, 0))
    return pl.pallas_call(
        _win_kernel,
        grid=(WIN_STEPS,),
        in_specs=[rows],
        out_specs=rows,
        out_shape=jax.ShapeDtypeStruct((IN_COLS, D_MODEL), BF16),
        compiler_params=pltpu.CompilerParams(dimension_semantics=("arbitrary",),
                                             vmem_limit_bytes=VMEM_LIMIT),
        name="win",
    )(wt)


def _proj_kernel(x_ref, g_ref, wt_ref, qg_ref, wuq_ref, kvg_ref, wuk_ref, wvt_ref, cos_ref, sin_ref,
                 q_ref, k_ref, vt_ref, nq_ref, nk_ref, nv_ref, cq_ref, ckv_ref, kr_ref, *, n_tiles):
    tm = x_ref.shape[0]
    i = pl.program_id(0)

    def stage1():
        h = _rms(x_ref[...], g_ref[...]).astype(BF16)

        def proj(lo, hi):
            return _dot_nt(h, wt_ref[lo:hi, :])

        cq_ref[...] = proj(_R_CQ, _R_CKV)
        ckv_ref[...] = proj(_R_CKV, _R_KR)
        wkr = wt_ref[_R_KR:_R_NQ, :]
        kr_ref[...] = _dot_nt(h, jnp.concatenate([wkr, wkr], axis=0))
        nq_ref[...] = (proj(_R_NQ, _R_NK) * (LOG2_E * NA_HEAD_DIM ** -0.5)).astype(BF16)
        nk_ref[...] = proj(_R_NK, _R_NV).astype(BF16)
        nv_ref[...] = proj(_R_NV, IN_COLS).astype(BF16)

    def stage2():
        c32, s32 = cos_ref[...], sin_ref[...]
        cos = jnp.concatenate([c32, c32] * (LANES // MLA_ROPE), axis=1)
        sin = jnp.concatenate([-s32, s32] * (LANES // MLA_ROPE), axis=1)
        lane = lax.broadcasted_iota(jnp.int32, (tm, LANES), 1)
        lo_half = (lane % MLA_ROPE) < (MLA_ROPE // 2)

        cqn = _rms(cq_ref[...], qg_ref[...]).astype(BF16)
        q = _dot(cqn, wuq_ref[...]) * (LOG2_E * MLA_QK ** -0.5)
        rope0 = HEADS * MLA_NOPE
        for pair in range(HEADS // 2):
            qr = _rope_lanes(q[:, rope0 + pair * LANES: rope0 + (pair + 1) * LANES], cos, sin, lo_half)
            qr = qr.astype(BF16)
            for hh in (2 * pair, 2 * pair + 1):
                q_ref[hh, :, 0:LANES] = q[:, hh * MLA_NOPE:(hh + 1) * MLA_NOPE].astype(BF16)
                q_ref[hh, :, LANES:QK_PAD] = qr

        ckvn = _rms(ckv_ref[...], kvg_ref[...])
        kn = _dot(ckvn.astype(BF16), wuk_ref[...])
        vt = _dot(wvt_ref[...], jnp.transpose(ckvn).astype(BF16))
        kr = _rope_lanes(kr_ref[...], cos, sin, lo_half)
        kr_even = jnp.where(lane < MLA_ROPE, kr, 0.0).astype(BF16)
        kr_odd = jnp.where(lane >= MLA_ROPE, kr, 0.0).astype(BF16)
        ones_row = jnp.where(lax.broadcasted_iota(jnp.int32, (VT_ROWS - MLA_V, tm), 0) == 0, 1.0, 0.0).astype(BF16)
        for hh in range(HEADS):
            k_ref[hh, :, 0:LANES] = kn[:, hh * MLA_NOPE:(hh + 1) * MLA_NOPE].astype(BF16)
            k_ref[hh, :, LANES:QK_PAD] = kr_even if hh % 2 == 0 else kr_odd
            vt_ref[hh, 0:MLA_V, :] = vt[hh * MLA_V:(hh + 1) * MLA_V, :].astype(BF16)
            vt_ref[hh, MLA_V:VT_ROWS, :] = ones_row

    pl.when(i == 0)(stage1)

    @pl.when((i > 0) & (i < n_tiles))
    def _():
        stage2()
        stage1()

    pl.when(i == n_tiles)(stage2)


def _proj(x, g, wt, qg, wuq, kvg, wuk, wvt, cos, sin, tm=512):
    n_tiles = SEQ // tm
    cur = lambda i: jnp.minimum(i, n_tiles - 1)
    prev = lambda i: jnp.maximum(i - 1, 0)
    row = lambda w, tile: pl.BlockSpec((tm, w), lambda i: (tile(i), 0))
    head = lambda w: pl.BlockSpec((HEADS, tm, w), lambda i: (0, prev(i), 0))
    return pl.pallas_call(
        functools.partial(_proj_kernel, n_tiles=n_tiles),
        grid=(n_tiles + 1,),
        in_specs=[row(D_MODEL, cur), _const_spec((1, D_MODEL)), _const_spec((IN_COLS, D_MODEL)),
                  _const_spec((1, MLA_Q_RANK)), _const_spec((MLA_Q_RANK, HEADS * MLA_QK)),
                  _const_spec((1, MLA_KV_RANK)), _const_spec((MLA_KV_RANK, HEADS * MLA_NOPE)),
                  _const_spec((HEADS * MLA_V, MLA_KV_RANK)),
                  row(MLA_ROPE // 2, prev), row(MLA_ROPE // 2, prev)],
        out_specs=[head(QK_PAD), head(QK_PAD),
                   pl.BlockSpec((HEADS, VT_ROWS, tm), lambda i: (0, 0, prev(i))),
                   row(NA_WIDTH, cur), row(NA_WIDTH, cur), row(NA_WIDTH, cur)],
        out_shape=[jax.ShapeDtypeStruct((HEADS, SEQ, QK_PAD), BF16),
                   jax.ShapeDtypeStruct((HEADS, SEQ, QK_PAD), BF16),
                   jax.ShapeDtypeStruct((HEADS, VT_ROWS, SEQ), BF16),
                   jax.ShapeDtypeStruct((SEQ, NA_WIDTH), BF16),
                   jax.ShapeDtypeStruct((SEQ, NA_WIDTH), BF16),
                   jax.ShapeDtypeStruct((SEQ, NA_WIDTH), BF16)],
        scratch_shapes=[pltpu.VMEM((tm, MLA_Q_RANK), F32), pltpu.VMEM((tm, MLA_KV_RANK), F32),
                        pltpu.VMEM((tm, LANES), F32)],
        compiler_params=pltpu.CompilerParams(dimension_semantics=("arbitrary",),
                                             vmem_limit_bytes=VMEM_LIMIT),
        name="proj",
    )(x, g, wt, qg, wuq, kvg, wuk, wvt, cos, sin)


NA_QROWS = 4
NA_KROWS = 12
NA_TQ = NA_QROWS * GRID_W
NA_TK = NA_KROWS * GRID_W
NA_BLOCKS = GRID_ROWS // NA_QROWS
NA_GROUP = 4
NA_GROUP_W = NA_GROUP * NA_HEAD_DIM
NA_BIAS_ROWS = 2 * NA_ROWS - 1
NA_BIAS_COLS = 2 * NA_COLS - 1
NA_STRIP = 32


def _na_bias_plan(variant):
    plan = {}
    for a in range(NA_QROWS):
        for b in range(NA_KROWS):
            if variant == "first":
                dr, ok = b - a + NA_ROWS - 1, b < NA_ROWS
            elif variant == "last":
                dr, ok = b - a - 1, b >= NA_KROWS - NA_ROWS
            else:
                dr, ok = b - a + NA_ROWS // 2 - 1, a <= b < a + NA_ROWS
            plan[(a, b)] = dr if ok else None
    return plan


def _na_prepare(g, group, rpb_ref, t_ref, bias_ref):
    fresh = group == 0

    @pl.when(fresh & (g == 0))
    def _():
        c = lax.broadcasted_iota(jnp.int32, (GRID_W, GRID_W), 0)
        kc = lax.broadcasted_iota(jnp.int32, (GRID_W, GRID_W), 1)
        start = jnp.clip(c - NA_COLS // 2, 0, GRID_W - NA_COLS)
        dc = jnp.where((kc >= start) & (kc < start + NA_COLS), kc - c + NA_COLS - 1, -1)

        def per_row(i, carry):
            tile = jnp.full((GRID_W, GRID_W), NEG_BIG, F32)
            for d in range(NA_BIAS_COLS):
                tile = jnp.where(dc == d, rpb_ref[i * NA_BIAS_COLS + d] * LOG2_E, tile)
            t_ref[i] = tile
            return carry

        lax.fori_loop(0, HEADS * NA_BIAS_ROWS, per_row, 0)

    def build(variant):
        plan = _na_bias_plan(variant)

        def per_head(h, carry):
            for (a, b), dr in plan.items():
                if dr is None:
                    tile = jnp.full((GRID_W, GRID_W), NEG_BIG, F32)
                else:
                    tile = t_ref[h * NA_BIAS_ROWS + dr]
                bias_ref[h, a * GRID_W:(a + 1) * GRID_W, b * GRID_W:(b + 1) * GRID_W] = tile
            return carry

        lax.fori_loop(0, HEADS, per_head, 0)

    pl.when(fresh & (g == 0))(lambda: build("first"))
    pl.when(fresh & (g == 1))(lambda: build("mid"))
    pl.when(fresh & (g == NA_BLOCKS - 1))(lambda: build("last"))


def _na_heads(group, q_ref, k_refs, v_refs, o_ref, bias_ref, s_refs, p_refs, l_refs):
    for hh in range(NA_GROUP):
        cols = slice(hh * NA_HEAD_DIM, (hh + 1) * NA_HEAD_DIM)
        s_ref, p_ref, l_ref = s_refs[hh % 2], p_refs[hh % 2], l_refs[hh % 2]
        head = group * NA_GROUP + hh
        qh = q_ref[:, cols]
        for t in range(3):
            s_ref[:, t * NA_TQ:(t + 1) * NA_TQ] = _dot_nt(qh, k_refs[t][:, cols])
        for r in range(NA_TQ // NA_STRIP):
            rows = slice(r * NA_STRIP, (r + 1) * NA_STRIP)
            s = s_ref[rows, :] + bias_ref[head, rows, :]
            p = jnp.exp2(s - jnp.max(s, axis=-1, keepdims=True))
            l_ref[rows, :] = jnp.broadcast_to(jnp.sum(p, axis=-1, keepdims=True), (NA_STRIP, LANES))
            p_ref[rows, :] = p.astype(BF16)
        o = _dot(p_ref[:, 0:NA_TQ], v_refs[0][:, cols])
        for t in (1, 2):
            o = o + _dot(p_ref[:, t * NA_TQ:(t + 1) * NA_TQ], v_refs[t][:, cols])
        o_ref[:, cols] = o / l_ref[...]


MLA_BOUND_SLACK = 1.0 + 2.0 ** -6
MLA_L_MIN = 2.0 ** -60
MLA_NORM_ROWS = 1024


def _mla_kernel(rpb_ref, q_ref, k_ref, vt_ref, w1_ref, w2_ref, wo_ref,
                nq_ref, nk0_ref, nk1_ref, nk2_ref, nv0_ref, nv1_ref, nv2_ref,
                o_ref, w1b_ref, w2b_ref, wob_ref, no_ref,
                pt_ref, kmax_ref, t_ref, bias_ref, s0_ref, s1_ref, p0_ref, p1_ref, l0_ref, l1_ref, *, tq, tk):
    step = pl.program_id(0) * pl.num_programs(1) + pl.program_id(1)
    _na_prepare(step // (HEADS // NA_GROUP), step % (HEADS // NA_GROUP), rpb_ref, t_ref, bias_ref)
    w1b_ref[...] = w1_ref[...].astype(BF16)
    w2b_ref[...] = w2_ref[...].astype(BF16)
    wob_ref[...] = wo_ref[...].astype(BF16)
    n_sub = q_ref.shape[0] // tq
    nk = SEQ // tk

    @pl.when(pl.program_id(1) == 0)
    def _():
        def scan(c, best):
            kc = k_ref[pl.ds(pl.multiple_of(c * MLA_NORM_ROWS, MLA_NORM_ROWS), MLA_NORM_ROWS), :].astype(F32)
            norms = jnp.sum(kc * kc, axis=-1, keepdims=True)
            return jnp.maximum(best, jnp.max(norms, axis=0, keepdims=True))

        best = lax.fori_loop(0, SEQ // MLA_NORM_ROWS, scan, jnp.zeros((1, 1), F32))
        kmax_ref[...] = jnp.broadcast_to(best, kmax_ref.shape)

    def scores_t(sub, c):
        return _dot_nt(k_ref[c * tk:(c + 1) * tk, :], q_ref[sub * tq:(sub + 1) * tq, :])

    def attend(sub, shift):
        for c in range(nk):
            pt_ref[sub, c * tk:(c + 1) * tk, :] = jnp.exp2(scores_t(sub, c) - shift).astype(BF16)
        acc = _dot(vt_ref[...], pt_ref[sub])
        return acc[:MLA_V, :], acc[MLA_V:MLA_V + 1, :]

    _na_heads(step % (HEADS // NA_GROUP), nq_ref, (nk0_ref, nk1_ref, nk2_ref), (nv0_ref, nv1_ref, nv2_ref),
              no_ref, bias_ref, (s0_ref, s1_ref), (p0_ref, p1_ref), (l0_ref, l1_ref))

    ones = jnp.ones((8, QK_PAD), BF16)
    denominators = []
    for sub in range(n_sub):
        rows = slice(sub * tq, (sub + 1) * tq)
        qf = q_ref[rows, :].astype(F32)
        qsq = _dot_nt(ones, (qf * qf).astype(BF16))[0:1, :]
        bound = jnp.sqrt(qsq * kmax_ref[0:1, 0:1]) * MLA_BOUND_SLACK
        num, den = attend(sub, bound)
        o_ref[rows, :] = jnp.transpose(num / den)
        denominators.append(jnp.min(den))

    for sub in range(n_sub):
        @pl.when(denominators[sub] < MLA_L_MIN)
        def _():
            m = jnp.full((1, tq), -jnp.inf, F32)
            for c in range(nk):
                m = jnp.maximum(m, jnp.max(scores_t(sub, c), axis=0, keepdims=True))
            num, den = attend(sub, m)
            o_ref[sub * tq:(sub + 1) * tq, :] = jnp.transpose(num / den)


def _mla(rpb, q, k, v, w1, w2, wo, nq, nk, nv, tq=512, tk=512, n_sub=2):
    tb = n_sub * tq
    nb = SEQ // tb
    groups = HEADS // NA_GROUP
    assert HEADS * nb == NA_BLOCKS * groups, "one neighbourhood (block, head group) piece per grid step"
    na_blk = lambda h, i: (h * nb + i) // groups
    na_grp = lambda h, i: (h * nb + i) % groups
    first_blk = lambda h, i: jnp.clip(na_blk(h, i) - 1, 0, NA_BLOCKS - 3)
    na_kv = [pl.BlockSpec((NA_TQ, NA_GROUP_W), functools.partial(lambda h, i, t: (first_blk(h, i) + t, na_grp(h, i)), t=t))
             for t in range(3)]
    na_q = pl.BlockSpec((NA_TQ, NA_GROUP_W), lambda h, i: (na_blk(h, i), na_grp(h, i)))
    ff_slice = D_FF // (HEADS * nb)
    wo_slice = (MLA_WIDTH + NA_WIDTH) // (HEADS * nb)
    assert ff_slice % LANES == 0 and wo_slice % 16 == 0
    step = lambda h, i: h * nb + i
    return pl.pallas_call(
        functools.partial(_mla_kernel, tq=tq, tk=tk),
        grid=(HEADS, nb),
        in_specs=[pl.BlockSpec(memory_space=pltpu.SMEM),
                  pl.BlockSpec((None, tb, QK_PAD), lambda h, i: (h, i, 0)),
                  pl.BlockSpec((None, SEQ, QK_PAD), lambda h, i: (h, 0, 0)),
                  pl.BlockSpec((None, VT_ROWS, SEQ), lambda h, i: (h, 0, 0)),
                  pl.BlockSpec((D_MODEL, ff_slice), lambda h, i: (0, step(h, i))),
                  pl.BlockSpec((ff_slice, D_MODEL), lambda h, i: (step(h, i), 0)),
                  pl.BlockSpec((wo_slice, D_MODEL), lambda h, i: (step(h, i), 0)),
                  na_q] + na_kv + na_kv,
        out_specs=[pl.BlockSpec((tb, MLA_V), lambda h, i: (i, h)),
                   pl.BlockSpec((D_MODEL, ff_slice), lambda h, i: (0, step(h, i))),
                   pl.BlockSpec((ff_slice, D_MODEL), lambda h, i: (step(h, i), 0)),
                   pl.BlockSpec((wo_slice, D_MODEL), lambda h, i: (step(h, i), 0)),
                   na_q],
        out_shape=[jax.ShapeDtypeStruct((SEQ, MLA_WIDTH), F32),
                   jax.ShapeDtypeStruct((D_MODEL, D_FF), BF16),
                   jax.ShapeDtypeStruct((D_FF, D_MODEL), BF16),
                   jax.ShapeDtypeStruct((MLA_WIDTH + NA_WIDTH, D_MODEL), BF16),
                   jax.ShapeDtypeStruct((SEQ, NA_WIDTH), F32)],
        scratch_shapes=[pltpu.VMEM((n_sub, SEQ, tq), BF16), pltpu.VMEM((8, LANES), F32),
                        pltpu.VMEM((HEADS * NA_BIAS_ROWS, GRID_W, GRID_W), F32),
                        pltpu.VMEM((HEADS, NA_TQ, NA_TK), F32),
                        pltpu.VMEM((NA_TQ, NA_TK), F32), pltpu.VMEM((NA_TQ, NA_TK), F32),
                        pltpu.VMEM((NA_TQ, NA_TK), BF16), pltpu.VMEM((NA_TQ, NA_TK), BF16),
                        pltpu.VMEM((NA_TQ, LANES), F32), pltpu.VMEM((NA_TQ, LANES), F32)],
        compiler_params=pltpu.CompilerParams(dimension_semantics=("arbitrary", "arbitrary"),
                                             vmem_limit_bytes=VMEM_LIMIT),
        name="mla",
    )(rpb.astype(F32).reshape(-1), q, k, v, w1, w2, wo, nq, nk, nk, nk, nv, nv, nv)


def _outproj_kernel(a_ref, n_ref, ga_ref, gn_ref, wb_ref, x_ref, gm_ref, o_ref, h_ref):
    an = _rms(a_ref[...], ga_ref[...]).astype(BF16)
    nn = _rms(n_ref[...], gn_ref[...]).astype(BF16)
    y = x_ref[...] + _dot(an, wb_ref[0:MLA_WIDTH, :]) + _dot(nn, wb_ref[MLA_WIDTH:, :])
    o_ref[...] = y
    h_ref[...] = _rms(y, gm_ref[...]).astype(BF16)


def _outproj(a, n, ga, gn, w, x, gm, tm=512):
    row = lambda w_: pl.BlockSpec((tm, w_), lambda i: (i, 0))
    return pl.pallas_call(
        _outproj_kernel,
        grid=(SEQ // tm,),
        in_specs=[row(MLA_WIDTH), row(NA_WIDTH), _const_spec((1, MLA_WIDTH)), _const_spec((1, NA_WIDTH)),
                  _const_spec((MLA_WIDTH + NA_WIDTH, D_MODEL)), row(D_MODEL), _const_spec((1, D_MODEL))],
        out_specs=[row(D_MODEL), row(D_MODEL)],
        out_shape=[jax.ShapeDtypeStruct((SEQ, D_MODEL), F32), jax.ShapeDtypeStruct((SEQ, D_MODEL), BF16)],
        compiler_params=pltpu.CompilerParams(dimension_semantics=("arbitrary",),
                                             vmem_limit_bytes=VMEM_LIMIT),
        name="outproj",
    )(a, n, ga, gn, w, x, gm)


def _mlp_kernel(x_ref, h_ref, w1_ref, w2_ref, gf_ref, o_ref, acc_ref):
    j = pl.program_id(1)
    last = pl.num_programs(1) - 1

    def chunk():
        hid = jnp.maximum(_dot(h_ref[...], w1_ref[...]), 0.0)
        return _dot((hid * hid).astype(BF16), w2_ref[...])

    @pl.when(j == 0)
    def _():
        acc_ref[...] = x_ref[...] + chunk()

    @pl.when((j > 0) & (j < last))
    def _():
        acc_ref[...] += chunk()

    @pl.when(j == last)
    def _():
        o_ref[...] = _rms(acc_ref[...] + chunk(), gf_ref[...])


def _mlp(x, h, w1, w2, gf, tm=512, tf=1024):
    assert D_FF // tf >= 2
    tile = lambda: pl.BlockSpec((tm, D_MODEL), lambda i, j: (i, 0))
    return pl.pallas_call(
        _mlp_kernel,
        grid=(SEQ // tm, D_FF // tf),
        in_specs=[tile(), tile(),
                  pl.BlockSpec((D_MODEL, tf), lambda i, j: (0, j)),
                  pl.BlockSpec((tf, D_MODEL), lambda i, j: (j, 0)),
                  _const_spec((1, D_MODEL))],
        out_specs=tile(),
        out_shape=jax.ShapeDtypeStruct((SEQ, D_MODEL), F32),
        scratch_shapes=[pltpu.VMEM((tm, D_MODEL), F32)],
        compiler_params=pltpu.CompilerParams(dimension_semantics=("arbitrary", "arbitrary"),
                                             vmem_limit_bytes=VMEM_LIMIT),
        name="mlp",
    )(x, h, w1, w2, gf)


def _rope_tables():
    half = MLA_ROPE // 2
    inv = ROPE_THETA ** (-jnp.arange(half, dtype=F32) / half)
    ang_a = (jnp.arange(SEQ // ROPE_SPLIT) * ROPE_SPLIT).astype(F32)[:, None] * inv[None, :]
    ang_b = jnp.arange(ROPE_SPLIT).astype(F32)[:, None] * inv[None, :]
    ca, sa = jnp.cos(ang_a)[:, None, :], jnp.sin(ang_a)[:, None, :]
    cb, sb = jnp.cos(ang_b)[None, :, :], jnp.sin(ang_b)[None, :, :]
    return (ca * cb - sa * sb).reshape(SEQ, half), (sa * cb + ca * sb).reshape(SEQ, half)


def kernel(x, attn_norm_g, w_in, q_norm_g, w_uq, kv_norm_g, w_ukv, na_rpb, mla_out_norm_g, na_out_norm_g,
           w_out, mlp_norm_g, w_ff1, w_ff2, final_norm_g):
    assert x.shape == (1, SEQ, D_MODEL)
    assert w_in.shape[0] == 1, "the final norm is fused into the (single) layer's MLP kernel"
    cos, sin = _rope_tables()
    xs = x[0]
    for l in range(1):
        wt = _win(jnp.swapaxes(w_in[l], 0, 1))
        wq = w_uq[l].reshape(MLA_Q_RANK, HEADS, MLA_QK)
        wuq = jnp.concatenate([wq[:, :, :MLA_NOPE].reshape(MLA_Q_RANK, HEADS * MLA_NOPE),
                               wq[:, :, MLA_NOPE:].reshape(MLA_Q_RANK, HEADS * MLA_ROPE)], axis=1).astype(BF16)
        wk = w_ukv[l].reshape(MLA_KV_RANK, HEADS, MLA_NOPE + MLA_V)
        wuk = wk[:, :, :MLA_NOPE].reshape(MLA_KV_RANK, HEADS * MLA_NOPE).astype(BF16)
        wvt = jnp.transpose(wk[:, :, MLA_NOPE:].reshape(MLA_KV_RANK, HEADS * MLA_V)).astype(BF16)

        q, k, vt, nq, nk, nv = _proj(xs, attn_norm_g[l][None], wt, q_norm_g[l][None], wuq,
                                    kv_norm_g[l][None], wuk, wvt, cos, sin)
        a_out, w1b, w2b, wob, n_out = _mla(na_rpb[l], q, k, vt, w_ff1[l], w_ff2[l], w_out[l], nq, nk, nv)
        xs, hs = _outproj(a_out, n_out, mla_out_norm_g[l][None], na_out_norm_g[l][None],
                          wob, xs, mlp_norm_g[l][None])
        xs = _mlp(xs, hs, w1b, w2b, final_norm_g[None])
    return xs[None]
```

```python
import functools

import jax
import jax.numpy as jnp
from jax import lax
from jax.experimental import pallas as pl
from jax.experimental.pallas import tpu as pltpu

F32 = jnp.float32
BF16 = jnp.bfloat16

D_MODEL = 2048
SEQ = 8192
GRID_W = 64
GRID_ROWS = SEQ // GRID_W
HEADS = 8
MLA_NOPE = 128
MLA_ROPE = 64
MLA_V = 128
MLA_QK = MLA_NOPE + MLA_ROPE
MLA_Q_RANK = 512
MLA_KV_RANK = 256
ROPE_THETA = 10000.0
NA_HEAD_DIM = 128
NA_ROWS = 8
NA_COLS = 16
NA_WIDTH = HEADS * NA_HEAD_DIM
MLA_WIDTH = HEADS * MLA_V
D_FF = 4 * D_MODEL
NORM_EPS = 1e-6

LANES = 128
QK_PAD = 2 * LANES
VT_ROWS = MLA_V + 16
LOG2_E = 1.4426950408889634
ROPE_SPLIT = 64
NEG_BIG = -1e30

_R_CQ = 0
_R_CKV = _R_CQ + MLA_Q_RANK
_R_KR = _R_CKV + MLA_KV_RANK
_R_NQ = _R_KR + MLA_ROPE
_R_NK = _R_NQ + NA_WIDTH
_R_NV = _R_NK + NA_WIDTH
IN_COLS = _R_NV + NA_WIDTH
WIN_STEPS = 4

VMEM_LIMIT = 56 * 1024 * 1024


def _rms(xf, g):
    y = xf * lax.rsqrt(jnp.mean(xf * xf, axis=-1, keepdims=True) + NORM_EPS)
    return y * g


def _dot(a, b):
    return jnp.dot(a, b, preferred_element_type=F32)


def _dot_nt(a, b):
    return lax.dot_general(a, b, (((1,), (1,)), ((), ())), preferred_element_type=F32)


def _const_spec(shape):
    nd = len(shape)
    return pl.BlockSpec(shape, lambda *_: (0,) * nd, pipeline_mode=pl.Buffered(1))


def _rope_lanes(x, c, s, lo_half):
    partner = jnp.where(lo_half, pltpu.roll(x, LANES - MLA_ROPE // 2, 1), pltpu.roll(x, MLA_ROPE // 2, 1))
    return x * c + partner * s


def _win_kernel(w_ref, wb_ref):
    wb_ref[...] = w_ref[...].astype(BF16)


def _win(wt):
    rows = pl.BlockSpec((IN_COLS // WIN_STEPS, D_MODEL), lambda i: (i, 0))
    return pl.pallas_call(
        _win_kernel,
        grid=(WIN_STEPS,),
        in_specs=[rows],
        out_specs=rows,
        out_shape=jax.ShapeDtypeStruct((IN_COLS, D_MODEL), BF16),
        compiler_params=pltpu.CompilerParams(dimension_semantics=("arbitrary",),
                                             vmem_limit_bytes=VMEM_LIMIT),
        name="win",
    )(wt)


def _proj_kernel(x_ref, g_ref, wt_ref, qg_ref, wuq_ref, kvg_ref, wuk_ref, wvt_ref, cos_ref, sin_ref,
                 q_ref, k_ref, vt_ref, nq_ref, nk_ref, nv_ref, cq_ref, ckv_ref, kr_ref, *, n_tiles):
    tm = x_ref.shape[0]
    i = pl.program_id(0)

    def stage1():
        h = _rms(x_ref[...], g_ref[...]).astype(BF16)

        def proj(lo, hi):
            return _dot_nt(h, wt_ref[lo:hi, :])

        cq_ref[...] = proj(_R_CQ, _R_CKV)
        ckv_ref[...] = proj(_R_CKV, _R_KR)
        wkr = wt_ref[_R_KR:_R_NQ, :]
        kr_ref[...] = _dot_nt(h, jnp.concatenate([wkr, wkr], axis=0))
        nq_ref[...] = (proj(_R_NQ, _R_NK) * (LOG2_E * NA_HEAD_DIM ** -0.5)).astype(BF16)
        nk_ref[...] = proj(_R_NK, _R_NV).astype(BF16)
        nv_ref[...] = proj(_R_NV, IN_COLS).astype(BF16)

    def stage2():
        c32, s32 = cos_ref[...], sin_ref[...]
        cos = jnp.concatenate([c32, c32] * (LANES // MLA_ROPE), axis=1)
        sin = jnp.concatenate([-s32, s32] * (LANES // MLA_ROPE), axis=1)
        lane = lax.broadcasted_iota(jnp.int32, (tm, LANES), 1)
        lo_half = (lane % MLA_ROPE) < (MLA_ROPE // 2)

        cqn = _rms(cq_ref[...], qg_ref[...]).astype(BF16)
        q = _dot(cqn, wuq_ref[...]) * (LOG2_E * MLA_QK ** -0.5)
        rope0 = HEADS * MLA_NOPE
        for pair in range(HEADS // 2):
            qr = _rope_lanes(q[:, rope0 + pair * LANES: rope0 + (pair + 1) * LANES], cos, sin, lo_half)
            qr = qr.astype(BF16)
            for hh in (2 * pair, 2 * pair + 1):
                q_ref[hh, :, 0:LANES] = q[:, hh * MLA_NOPE:(hh + 1) * MLA_NOPE].astype(BF16)
                q_ref[hh, :, LANES:QK_PAD] = qr

        ckvn = _rms(ckv_ref[...], kvg_ref[...])
        kn = _dot(ckvn.astype(BF16), wuk_ref[...])
        vt = _dot(wvt_ref[...], jnp.transpose(ckvn).astype(BF16))
        kr = _rope_lanes(kr_ref[...], cos, sin, lo_half)
        kr_even = jnp.where(lane < MLA_ROPE, kr, 0.0).astype(BF16)
        kr_odd = jnp.where(lane >= MLA_ROPE, kr, 0.0).astype(BF16)
        ones_row = jnp.where(lax.broadcasted_iota(jnp.int32, (VT_ROWS - MLA_V, tm), 0) == 0, 1.0, 0.0).astype(BF16)
        for hh in range(HEADS):
            k_ref[hh, :, 0:LANES] = kn[:, hh * MLA_NOPE:(hh + 1) * MLA_NOPE].astype(BF16)
            k_ref[hh, :, LANES:QK_PAD] = kr_even if hh % 2 == 0 else kr_odd
            vt_ref[hh, 0:MLA_V, :] = vt[hh * MLA_V:(hh + 1) * MLA_V, :].astype(BF16)
            vt_ref[hh, MLA_V:VT_ROWS, :] = ones_row

    pl.when(i == 0)(stage1)

    @pl.when((i > 0) & (i < n_tiles))
    def _():
        stage2()
        stage1()

    pl.when(i == n_tiles)(stage2)


def _proj(x, g, wt, qg, wuq, kvg, wuk, wvt, cos, sin, tm=512):
    n_tiles = SEQ // tm
    cur = lambda i: jnp.minimum(i, n_tiles - 1)
    prev = lambda i: jnp.maximum(i - 1, 0)
    row = lambda w, tile: pl.BlockSpec((tm, w), lambda i: (tile(i), 0))
    head = lambda w: pl.BlockSpec((HEADS, tm, w), lambda i: (0, prev(i), 0))
    return pl.pallas_call(
        functools.partial(_proj_kernel, n_tiles=n_tiles),
        grid=(n_tiles + 1,),
        in_specs=[row(D_MODEL, cur), _const_spec((1, D_MODEL)), _const_spec((IN_COLS, D_MODEL)),
                  _const_spec((1, MLA_Q_RANK)), _const_spec((MLA_Q_RANK, HEADS * MLA_QK)),
                  _const_spec((1, MLA_KV_RANK)), _const_spec((MLA_KV_RANK, HEADS * MLA_NOPE)),
                  _const_spec((HEADS * MLA_V, MLA_KV_RANK)),
                  row(MLA_ROPE // 2, prev), row(MLA_ROPE // 2, prev)],
        out_specs=[head(QK_PAD), head(QK_PAD),
                   pl.BlockSpec((HEADS, VT_ROWS, tm), lambda i: (0, 0, prev(i))),
                   row(NA_WIDTH, cur), row(NA_WIDTH, cur), row(NA_WIDTH, cur)],
        out_shape=[jax.ShapeDtypeStruct((HEADS, SEQ, QK_PAD), BF16),
                   jax.ShapeDtypeStruct((HEADS, SEQ, QK_PAD), BF16),
                   jax.ShapeDtypeStruct((HEADS, VT_ROWS, SEQ), BF16),
                   jax.ShapeDtypeStruct((SEQ, NA_WIDTH), BF16),
                   jax.ShapeDtypeStruct((SEQ, NA_WIDTH), BF16),
                   jax.ShapeDtypeStruct((SEQ, NA_WIDTH), BF16)],
        scratch_shapes=[pltpu.VMEM((tm, MLA_Q_RANK), F32), pltpu.VMEM((tm, MLA_KV_RANK), F32),
                        pltpu.VMEM((tm, LANES), F32)],
        compiler_params=pltpu.CompilerParams(dimension_semantics=("arbitrary",),
                                             vmem_limit_bytes=VMEM_LIMIT),
        name="proj",
    )(x, g, wt, qg, wuq, kvg, wuk, wvt, cos, sin)


MLA_BOUND_SLACK = 1.0 + 2.0 ** -6
MLA_L_MIN = 2.0 ** -60
MLA_NORM_ROWS = 1024


def _mla_kernel(q_ref, k_ref, vt_ref, w1_ref, w2_ref, wo_ref, o_ref, w1b_ref, w2b_ref, wob_ref,
                pt_ref, kmax_ref, *, tq, tk):
    w1b_ref[...] = w1_ref[...].astype(BF16)
    w2b_ref[...] = w2_ref[...].astype(BF16)
    wob_ref[...] = wo_ref[...].astype(BF16)
    n_sub = q_ref.shape[0] // tq
    nk = SEQ // tk

    @pl.when(pl.program_id(1) == 0)
    def _():
        def scan(c, best):
            kc = k_ref[pl.ds(pl.multiple_of(c * MLA_NORM_ROWS, MLA_NORM_ROWS), MLA_NORM_ROWS), :].astype(F32)
            norms = jnp.sum(kc * kc, axis=-1, keepdims=True)
            return jnp.maximum(best, jnp.max(norms, axis=0, keepdims=True))

        best = lax.fori_loop(0, SEQ // MLA_NORM_ROWS, scan, jnp.zeros((1, 1), F32))
        kmax_ref[...] = jnp.broadcast_to(best, kmax_ref.shape)

    def scores_t(sub, c):
        return _dot_nt(k_ref[c * tk:(c + 1) * tk, :], q_ref[sub * tq:(sub + 1) * tq, :])

    def attend(sub, shift):
        for c in range(nk):
            pt_ref[sub, c * tk:(c + 1) * tk, :] = jnp.exp2(scores_t(sub, c) - shift).astype(BF16)
        acc = _dot(vt_ref[...], pt_ref[sub])
        return acc[:MLA_V, :], acc[MLA_V:MLA_V + 1, :]

    ones = jnp.ones((8, QK_PAD), BF16)
    denominators = []
    for sub in range(n_sub):
        rows = slice(sub * tq, (sub + 1) * tq)
        qf = q_ref[rows, :].astype(F32)
        qsq = _dot_nt(ones, (qf * qf).astype(BF16))[0:1, :]
        bound = jnp.sqrt(qsq * kmax_ref[0:1, 0:1]) * MLA_BOUND_SLACK
        num, den = attend(sub, bound)
        o_ref[rows, :] = jnp.transpose(num / den)
        denominators.append(jnp.min(den))

    for sub in range(n_sub):
        @pl.when(denominators[sub] < MLA_L_MIN)
        def _():
            m = jnp.full((1, tq), -jnp.inf, F32)
            for c in range(nk):
                m = jnp.maximum(m, jnp.max(scores_t(sub, c), axis=0, keepdims=True))
            num, den = attend(sub, m)
            o_ref[sub * tq:(sub + 1) * tq, :] = jnp.transpose(num / den)


def _mla(q, k, v, w1, w2, wo, tq=512, tk=512, n_sub=2):
    tb = n_sub * tq
    nb = SEQ // tb
    ff_slice = D_FF // (HEADS * nb)
    wo_slice = (MLA_WIDTH + NA_WIDTH) // (HEADS * nb)
    assert ff_slice % LANES == 0 and wo_slice % 16 == 0
    step = lambda h, i: h * nb + i
    return pl.pallas_call(
        functools.partial(_mla_kernel, tq=tq, tk=tk),
        grid=(HEADS, nb),
        in_specs=[pl.BlockSpec((None, tb, QK_PAD), lambda h, i: (h, i, 0)),
                  pl.BlockSpec((None, SEQ, QK_PAD), lambda h, i: (h, 0, 0)),
                  pl.BlockSpec((None, VT_ROWS, SEQ), lambda h, i: (h, 0, 0)),
                  pl.BlockSpec((D_MODEL, ff_slice), lambda h, i: (0, step(h, i))),
                  pl.BlockSpec((ff_slice, D_MODEL), lambda h, i: (step(h, i), 0)),
                  pl.BlockSpec((wo_slice, D_MODEL), lambda h, i: (step(h, i), 0))],
        out_specs=[pl.BlockSpec((tb, MLA_V), lambda h, i: (i, h)),
                   pl.BlockSpec((D_MODEL, ff_slice), lambda h, i: (0, step(h, i))),
                   pl.BlockSpec((ff_slice, D_MODEL), lambda h, i: (step(h, i), 0)),
                   pl.BlockSpec((wo_slice, D_MODEL), lambda h, i: (step(h, i), 0))],
        out_shape=[jax.ShapeDtypeStruct((SEQ, MLA_WIDTH), F32),
                   jax.ShapeDtypeStruct((D_MODEL, D_FF), BF16),
                   jax.ShapeDtypeStruct((D_FF, D_MODEL), BF16),
                   jax.ShapeDtypeStruct((MLA_WIDTH + NA_WIDTH, D_MODEL), BF16)],
        scratch_shapes=[pltpu.VMEM((n_sub, SEQ, tq), BF16), pltpu.VMEM((8, LANES), F32)],
        compiler_params=pltpu.CompilerParams(dimension_semantics=("arbitrary", "arbitrary"),
                                             vmem_limit_bytes=VMEM_LIMIT),
        name="mla",
    )(q, k, v, w1, w2, wo)


NA_QROWS = 4
NA_KROWS = 12
NA_TQ = NA_QROWS * GRID_W
NA_TK = NA_KROWS * GRID_W
NA_STEPS = GRID_ROWS // NA_QROWS
NA_BIAS_ROWS = 2 * NA_ROWS - 1
NA_BIAS_COLS = 2 * NA_COLS - 1
NA_STRIP = 32


def _na_bias_plan(variant):
    plan = {}
    for a in range(NA_QROWS):
        for b in range(NA_KROWS):
            if variant == "first":
                dr, ok = b - a + NA_ROWS - 1, b < NA_ROWS
            elif variant == "last":
                dr, ok = b - a - 1, b >= NA_KROWS - NA_ROWS
            else:
                dr, ok = b - a + NA_ROWS // 2 - 1, a <= b < a + NA_ROWS
            plan[(a, b)] = dr if ok else None
    return plan


def _na_kernel(rpb_ref, q_ref, k0_ref, k1_ref, k2_ref, v0_ref, v1_ref, v2_ref, o_ref, t_ref, bias_ref,
               s0_ref, s1_ref, p0_ref, p1_ref, l0_ref, l1_ref):
    g = pl.program_id(0)
    s_refs, p_refs, l_refs = (s0_ref, s1_ref), (p0_ref, p1_ref), (l0_ref, l1_ref)

    @pl.when(g == 0)
    def _():
        c = lax.broadcasted_iota(jnp.int32, (GRID_W, GRID_W), 0)
        kc = lax.broadcasted_iota(jnp.int32, (GRID_W, GRID_W), 1)
        start = jnp.clip(c - NA_COLS // 2, 0, GRID_W - NA_COLS)
        dc = jnp.where((kc >= start) & (kc < start + NA_COLS), kc - c + NA_COLS - 1, -1)

        def per_row(i, carry):
            tile = jnp.full((GRID_W, GRID_W), NEG_BIG, F32)
            for d in range(NA_BIAS_COLS):
                tile = jnp.where(dc == d, rpb_ref[i * NA_BIAS_COLS + d] * LOG2_E, tile)
            t_ref[i] = tile
            return carry

        lax.fori_loop(0, HEADS * NA_BIAS_ROWS, per_row, 0)

    def build(variant):
        plan = _na_bias_plan(variant)

        def per_head(h, carry):
            for (a, b), dr in plan.items():
                if dr is None:
                    tile = jnp.full((GRID_W, GRID_W), NEG_BIG, F32)
                else:
                    tile = t_ref[h * NA_BIAS_ROWS + dr]
                bias_ref[h, a * GRID_W:(a + 1) * GRID_W, b * GRID_W:(b + 1) * GRID_W] = tile
            return carry

        lax.fori_loop(0, HEADS, per_head, 0)

    pl.when(g == 0)(lambda: build("first"))
    pl.when(g == 1)(lambda: build("mid"))
    pl.when(g == NA_STEPS - 1)(lambda: build("last"))

    k_refs = (k0_ref, k1_ref, k2_ref)
    v_refs = (v0_ref, v1_ref, v2_ref)
    for h in range(HEADS):
        cols = slice(h * NA_HEAD_DIM, (h + 1) * NA_HEAD_DIM)
        s_ref, p_ref, l_ref = s_refs[h % 2], p_refs[h % 2], l_refs[h % 2]
        qh = q_ref[:, cols]
        for t in range(3):
            s_ref[:, t * NA_TQ:(t + 1) * NA_TQ] = _dot_nt(qh, k_refs[t][:, cols])
        for r in range(NA_TQ // NA_STRIP):
            rows = slice(r * NA_STRIP, (r + 1) * NA_STRIP)
            s = s_ref[rows, :] + bias_ref[h, rows, :]
            p = jnp.exp2(s - jnp.max(s, axis=-1, keepdims=True))
            l_ref[rows, :] = jnp.broadcast_to(jnp.sum(p, axis=-1, keepdims=True), (NA_STRIP, LANES))
            p_ref[rows, :] = p.astype(BF16)
        o = _dot(p_ref[:, 0:NA_TQ], v_refs[0][:, cols])
        for t in (1, 2):
            o = o + _dot(p_ref[:, t * NA_TQ:(t + 1) * NA_TQ], v_refs[t][:, cols])
        o_ref[:, cols] = o / l_ref[...]


def _na(rpb, nq, nk, nv):
    first_blk = lambda g: jnp.clip(g - 1, 0, NA_STEPS - 3)
    kv_specs = [pl.BlockSpec((NA_TQ, NA_WIDTH), functools.partial(lambda g, t: (first_blk(g) + t, 0), t=t))
                for t in range(3)]
    return pl.pallas_call(
        _na_kernel,
        grid=(NA_STEPS,),
        in_specs=[pl.BlockSpec(memory_space=pltpu.SMEM), pl.BlockSpec((NA_TQ, NA_WIDTH), lambda g: (g, 0))]
                 + kv_specs + kv_specs,
        out_specs=pl.BlockSpec((NA_TQ, NA_WIDTH), lambda g: (g, 0)),
        out_shape=jax.ShapeDtypeStruct((SEQ, NA_WIDTH), F32),
        scratch_shapes=[pltpu.VMEM((HEADS * NA_BIAS_ROWS, GRID_W, GRID_W), F32),
                        pltpu.VMEM((HEADS, NA_TQ, NA_TK), F32),
                        pltpu.VMEM((NA_TQ, NA_TK), F32), pltpu.VMEM((NA_TQ, NA_TK), F32),
                        pltpu.VMEM((NA_TQ, NA_TK), BF16), pltpu.VMEM((NA_TQ, NA_TK), BF16),
                        pltpu.VMEM((NA_TQ, LANES), F32), pltpu.VMEM((NA_TQ, LANES), F32)],
        compiler_params=pltpu.CompilerParams(dimension_semantics=("arbitrary",),
                                             vmem_limit_bytes=VMEM_LIMIT),
        name="na",
    )(rpb.astype(F32).reshape(-1), nq, nk, nk, nk, nv, nv, nv)


def _outproj_kernel(a_ref, n_ref, ga_ref, gn_ref, wb_ref, x_ref, gm_ref, o_ref, h_ref):
    an = _rms(a_ref[...], ga_ref[...]).astype(BF16)
    nn = _rms(n_ref[...], gn_ref[...]).astype(BF16)
    y = x_ref[...] + _dot(an, wb_ref[0:MLA_WIDTH, :]) + _dot(nn, wb_ref[MLA_WIDTH:, :])
    o_ref[...] = y
    h_ref[...] = _rms(y, gm_ref[...]).astype(BF16)


def _outproj(a, n, ga, gn, w, x, gm, tm=512):
    row = lambda w_: pl.BlockSpec((tm, w_), lambda i: (i, 0))
    return pl.pallas_call(
        _outproj_kernel,
        grid=(SEQ // tm,),
        in_specs=[row(MLA_WIDTH), row(NA_WIDTH), _const_spec((1, MLA_WIDTH)), _const_spec((1, NA_WIDTH)),
                  _const_spec((MLA_WIDTH + NA_WIDTH, D_MODEL)), row(D_MODEL), _const_spec((1, D_MODEL))],
        out_specs=[row(D_MODEL), row(D_MODEL)],
        out_shape=[jax.ShapeDtypeStruct((SEQ, D_MODEL), F32), jax.ShapeDtypeStruct((SEQ, D_MODEL), BF16)],
        compiler_params=pltpu.CompilerParams(dimension_semantics=("arbitrary",),
                                             vmem_limit_bytes=VMEM_LIMIT),
        name="outproj",
    )(a, n, ga, gn, w, x, gm)


def _mlp_kernel(x_ref, h_ref, w1_ref, w2_ref, gf_ref, o_ref):
    j = pl.program_id(1)
    last = pl.num_programs(1) - 1

    def chunk():
        hid = jnp.maximum(_dot(h_ref[...], w1_ref[...]), 0.0)
        return _dot((hid * hid).astype(BF16), w2_ref[...])

    @pl.when(j == 0)
    def _():
        o_ref[...] = x_ref[...] + chunk()

    @pl.when((j > 0) & (j < last))
    def _():
        o_ref[...] += chunk()

    @pl.when(j == last)
    def _():
        o_ref[...] = _rms(o_ref[...] + chunk(), gf_ref[...])


def _mlp(x, h, w1, w2, gf, tm=1024, tf=512):
    assert D_FF // tf >= 2
    tile = lambda: pl.BlockSpec((tm, D_MODEL), lambda i, j: (i, 0))
    return pl.pallas_call(
        _mlp_kernel,
        grid=(SEQ // tm, D_FF // tf),
        in_specs=[tile(), tile(),
                  pl.BlockSpec((D_MODEL, tf), lambda i, j: (0, j)),
                  pl.BlockSpec((tf, D_MODEL), lambda i, j: (j, 0)),
                  _const_spec((1, D_MODEL))],
        out_specs=tile(),
        out_shape=jax.ShapeDtypeStruct((SEQ, D_MODEL), F32),
        compiler_params=pltpu.CompilerParams(dimension_semantics=("arbitrary", "arbitrary"),
                                             vmem_limit_bytes=VMEM_LIMIT),
        name="mlp",
    )(x, h, w1, w2, gf)


def _rope_tables():
    half = MLA_ROPE // 2
    inv = ROPE_THETA ** (-jnp.arange(half, dtype=F32) / half)
    ang_a = (jnp.arange(SEQ // ROPE_SPLIT) * ROPE_SPLIT).astype(F32)[:, None] * inv[None, :]
    ang_b = jnp.arange(ROPE_SPLIT).astype(F32)[:, None] * inv[None, :]
    ca, sa = jnp.cos(ang_a)[:, None, :], jnp.sin(ang_a)[:, None, :]
    cb, sb = jnp.cos(ang_b)[None, :, :], jnp.sin(ang_b)[None, :, :]
    return (ca * cb - sa * sb).reshape(SEQ, half), (sa * cb + ca * sb).reshape(SEQ, half)


def kernel(x, attn_norm_g, w_in, q_norm_g, w_uq, kv_norm_g, w_ukv, na_rpb, mla_out_norm_g, na_out_norm_g,
           w_out, mlp_norm_g, w_ff1, w_ff2, final_norm_g):
    assert x.shape == (1, SEQ, D_MODEL)
    assert w_in.shape[0] == 1, "the final norm is fused into the (single) layer's MLP kernel"
    cos, sin = _rope_tables()
    xs = x[0]
    for l in range(1):
        wt = _win(jnp.swapaxes(w_in[l], 0, 1))
        wq = w_uq[l].reshape(MLA_Q_RANK, HEADS, MLA_QK)
        wuq = jnp.concatenate([wq[:, :, :MLA_NOPE].reshape(MLA_Q_RANK, HEADS * MLA_NOPE),
                               wq[:, :, MLA_NOPE:].reshape(MLA_Q_RANK, HEADS * MLA_ROPE)], axis=1).astype(BF16)
        wk = w_ukv[l].reshape(MLA_KV_RANK, HEADS, MLA_NOPE + MLA_V)
        wuk = wk[:, :, :MLA_NOPE].reshape(MLA_KV_RANK, HEADS * MLA_NOPE).astype(BF16)
        wvt = jnp.transpose(wk[:, :, MLA_NOPE:].reshape(MLA_KV_RANK, HEADS * MLA_V)).astype(BF16)

        q, k, vt, nq, nk, nv = _proj(xs, attn_norm_g[l][None], wt, q_norm_g[l][None], wuq,
                                    kv_norm_g[l][None], wuk, wvt, cos, sin)
        a_out, w1b, w2b, wob = _mla(q, k, vt, w_ff1[l], w_ff2[l], w_out[l])
        n_out = _na(na_rpb[l], nq, nk, nv)
        xs, hs = _outproj(a_out, n_out, mla_out_norm_g[l][None], na_out_norm_g[l][None],
                          wob, xs, mlp_norm_g[l][None])
        xs = _mlp(xs, hs, w1b, w2b, final_norm_g[None])
    return xs[None]
```

```python
import functools

import jax
import jax.numpy as jnp
from jax import lax
from jax.experimental import pallas as pl
from jax.experimental.pallas import tpu as pltpu

F32 = jnp.float32
BF16 = jnp.bfloat16

D_MODEL = 2048
SEQ = 8192
GRID_W = 64
GRID_ROWS = SEQ // GRID_W
HEADS = 8
MLA_NOPE = 128
MLA_ROPE = 64
MLA_V = 128
MLA_QK = MLA_NOPE + MLA_ROPE
MLA_Q_RANK = 512
MLA_KV_RANK = 256
ROPE_THETA = 10000.0
NA_HEAD_DIM = 128
NA_ROWS = 8
NA_COLS = 16
NA_WIDTH = HEADS * NA_HEAD_DIM
MLA_WIDTH = HEADS * MLA_V
D_FF = 4 * D_MODEL
NORM_EPS = 1e-6

LANES = 128
SUBLANES = 8
BF16_ROWS = 2 * SUBLANES
QK_PAD = 2 * LANES
VT_ROWS = MLA_V + BF16_ROWS
LOG2_E = 1.4426950408889634
ROPE_SPLIT = 64
NEG_BIG = -1e30

_R_CQ = 0
_R_CKV = _R_CQ + MLA_Q_RANK
_R_KR = _R_CKV + MLA_KV_RANK
_R_NQ = _R_KR + MLA_ROPE
_R_NK = _R_NQ + NA_WIDTH
_R_NV = _R_NK + NA_WIDTH
IN_COLS = _R_NV + NA_WIDTH
WIN_STEPS = 4

VMEM_LIMIT = 56 * 1024 * 1024


def _rms(xf, g):
    y = xf * lax.rsqrt(jnp.mean(xf * xf, axis=-1, keepdims=True) + NORM_EPS)
    return y * g


def _dot(a, b):
    return jnp.dot(a, b, preferred_element_type=F32)


def _dot_nt(a, b):
    return lax.dot_general(a, b, (((1,), (1,)), ((), ())), preferred_element_type=F32)


def _const_spec(shape):
    nd = len(shape)
    return pl.BlockSpec(shape, lambda *_: (0,) * nd, pipeline_mode=pl.Buffered(1))


def _rope_lanes(x, c, s, lo_half):
    partner = jnp.where(lo_half, pltpu.roll(x, LANES - MLA_ROPE // 2, 1), pltpu.roll(x, MLA_ROPE // 2, 1))
    return x * c + partner * s


def _win_kernel(w_ref, wb_ref):
    wb_ref[...] = w_ref[...].astype(BF16)


def _win(wt):
    rows = pl.BlockSpec((IN_COLS // WIN_STEPS, D_MODEL), lambda i: (i, 0))
    return pl.pallas_call(
        _win_kernel,
        grid=(WIN_STEPS,),
        in_specs=[rows],
        out_specs=rows,
        out_shape=jax.ShapeDtypeStruct((IN_COLS, D_MODEL), BF16),
        compiler_params=pltpu.CompilerParams(dimension_semantics=("arbitrary",),
                                             vmem_limit_bytes=VMEM_LIMIT),
        name="win",
    )(wt)


def _proj_kernel(x_ref, g_ref, wt_ref, qg_ref, wuq_ref, kvg_ref, wuk_ref, wvt_ref, cos_ref, sin_ref,
                 q_ref, k_ref, vt_ref, kn2_ref, nq_ref, nk_ref, nv_ref, cq_ref, ckv_ref, kr_ref, *, n_tiles):
    tm = x_ref.shape[0]
    i = pl.program_id(0)

    def stage1():
        h = _rms(x_ref[...], g_ref[...]).astype(BF16)

        def proj(lo, hi):
            return _dot_nt(h, wt_ref[lo:hi, :])

        cq_ref[...] = proj(_R_CQ, _R_CKV)
        ckv_ref[...] = proj(_R_CKV, _R_KR)
        wkr = wt_ref[_R_KR:_R_NQ, :]
        kr_ref[...] = _dot_nt(h, jnp.concatenate([wkr, wkr], axis=0))
        nq_ref[...] = (proj(_R_NQ, _R_NK) * (LOG2_E * NA_HEAD_DIM ** -0.5)).astype(BF16)
        nk_ref[...] = proj(_R_NK, _R_NV).astype(BF16)
        nv_ref[...] = proj(_R_NV, IN_COLS).astype(BF16)

    def stage2():
        c32, s32 = cos_ref[...], sin_ref[...]
        cos = jnp.concatenate([c32, c32] * (LANES // MLA_ROPE), axis=1)
        sin = jnp.concatenate([-s32, s32] * (LANES // MLA_ROPE), axis=1)
        lane = lax.broadcasted_iota(jnp.int32, (tm, LANES), 1)
        lo_half = (lane % MLA_ROPE) < (MLA_ROPE // 2)

        cqn = _rms(cq_ref[...], qg_ref[...]).astype(BF16)
        q = _dot(cqn, wuq_ref[...]) * (LOG2_E * MLA_QK ** -0.5)
        rope0 = HEADS * MLA_NOPE
        for pair in range(HEADS // 2):
            qr = _rope_lanes(q[:, rope0 + pair * LANES: rope0 + (pair + 1) * LANES], cos, sin, lo_half)
            qr = qr.astype(BF16)
            for hh in (2 * pair, 2 * pair + 1):
                q_ref[hh, :, 0:LANES] = q[:, hh * MLA_NOPE:(hh + 1) * MLA_NOPE].astype(BF16)
                q_ref[hh, :, LANES:QK_PAD] = qr

        ckvn = _rms(ckv_ref[...], kvg_ref[...])
        kn = _dot(ckvn.astype(BF16), wuk_ref[...])
        vt = _dot(wvt_ref[...], jnp.transpose(ckvn).astype(BF16))
        kr = _rope_lanes(kr_ref[...], cos, sin, lo_half)
        kr_even = jnp.where(lane < MLA_ROPE, kr, 0.0).astype(BF16)
        kr_odd = jnp.where(lane >= MLA_ROPE, kr, 0.0).astype(BF16)
        ones_row = jnp.where(lax.broadcasted_iota(jnp.int32, (VT_ROWS - MLA_V, tm), 0) == 0, 1.0, 0.0).astype(BF16)
        kr_sq = jnp.sum(jnp.where(lane < MLA_ROPE, kr * kr, 0.0), axis=-1, keepdims=True)
        for hh in range(HEADS):
            kn_h = kn[:, hh * MLA_NOPE:(hh + 1) * MLA_NOPE]
            k_sq = jnp.sum(kn_h * kn_h, axis=-1, keepdims=True) + kr_sq
            kn2_ref[0, hh:hh + 1, :] = jnp.broadcast_to(jnp.max(k_sq, axis=0, keepdims=True), (1, LANES))
            k_ref[hh, :, 0:LANES] = kn_h.astype(BF16)
            k_ref[hh, :, LANES:QK_PAD] = kr_even if hh % 2 == 0 else kr_odd
            vt_ref[hh, 0:MLA_V, :] = vt[hh * MLA_V:(hh + 1) * MLA_V, :].astype(BF16)
            vt_ref[hh, MLA_V:VT_ROWS, :] = ones_row

    pl.when(i == 0)(stage1)

    @pl.when((i > 0) & (i < n_tiles))
    def _():
        stage2()
        stage1()

    pl.when(i == n_tiles)(stage2)


def _proj(x, g, wt, qg, wuq, kvg, wuk, wvt, cos, sin, tm=512):
    n_tiles = SEQ // tm
    cur = lambda i: jnp.minimum(i, n_tiles - 1)
    prev = lambda i: jnp.maximum(i - 1, 0)
    row = lambda w, tile: pl.BlockSpec((tm, w), lambda i: (tile(i), 0))
    head = lambda w: pl.BlockSpec((HEADS, tm, w), lambda i: (0, prev(i), 0))
    return pl.pallas_call(
        functools.partial(_proj_kernel, n_tiles=n_tiles),
        grid=(n_tiles + 1,),
        in_specs=[row(D_MODEL, cur), _const_spec((1, D_MODEL)), _const_spec((IN_COLS, D_MODEL)),
                  _const_spec((1, MLA_Q_RANK)), _const_spec((MLA_Q_RANK, HEADS * MLA_QK)),
                  _const_spec((1, MLA_KV_RANK)), _const_spec((MLA_KV_RANK, HEADS * MLA_NOPE)),
                  _const_spec((HEADS * MLA_V, MLA_KV_RANK)),
                  row(MLA_ROPE // 2, prev), row(MLA_ROPE // 2, prev)],
        out_specs=[head(QK_PAD), head(QK_PAD),
                   pl.BlockSpec((HEADS, VT_ROWS, tm), lambda i: (0, 0, prev(i))),
                   pl.BlockSpec((1, HEADS, LANES), lambda i: (prev(i), 0, 0)),
                   row(NA_WIDTH, cur), row(NA_WIDTH, cur), row(NA_WIDTH, cur)],
        out_shape=[jax.ShapeDtypeStruct((HEADS, SEQ, QK_PAD), BF16),
                   jax.ShapeDtypeStruct((HEADS, SEQ, QK_PAD), BF16),
                   jax.ShapeDtypeStruct((HEADS, VT_ROWS, SEQ), BF16),
                   jax.ShapeDtypeStruct((n_tiles, HEADS, LANES), F32),
                   jax.ShapeDtypeStruct((SEQ, NA_WIDTH), BF16),
                   jax.ShapeDtypeStruct((SEQ, NA_WIDTH), BF16),
                   jax.ShapeDtypeStruct((SEQ, NA_WIDTH), BF16)],
        scratch_shapes=[pltpu.VMEM((tm, MLA_Q_RANK), F32), pltpu.VMEM((tm, MLA_KV_RANK), F32),
                        pltpu.VMEM((tm, LANES), F32)],
        compiler_params=pltpu.CompilerParams(dimension_semantics=("arbitrary",),
                                             vmem_limit_bytes=VMEM_LIMIT),
        name="proj",
    )(x, g, wt, qg, wuq, kvg, wuk, wvt, cos, sin)


MLA_BOUND_SLACK = 1.0 + 2.0 ** -6
MLA_L_MIN = 2.0 ** -60


def _mla_kernel(kmax_ref, q_ref, k_ref, vt_ref, w1_ref, w2_ref, wo_ref, o_ref, w1b_ref, w2b_ref, wob_ref,
                pt_ref, *, tq, tk):
    w1b_ref[...] = w1_ref[...].astype(BF16)
    w2b_ref[...] = w2_ref[...].astype(BF16)
    wob_ref[...] = wo_ref[...].astype(BF16)
    n_sub = q_ref.shape[0] // tq
    nk = SEQ // tk

    kmax = kmax_ref[pl.program_id(0)]

    def scores_t(sub, c):
        return _dot_nt(k_ref[c * tk:(c + 1) * tk, :], q_ref[sub * tq:(sub + 1) * tq, :])

    def attend(sub, shift):
        for c in range(nk):
            pt_ref[sub, c * tk:(c + 1) * tk, :] = jnp.exp2(scores_t(sub, c) - shift).astype(BF16)
        acc = _dot(vt_ref[...], pt_ref[sub])
        return acc[:MLA_V, :], acc[MLA_V:MLA_V + 1, :]

    ones = jnp.ones((SUBLANES, QK_PAD), BF16)
    denominators = []
    for sub in range(n_sub):
        rows = slice(sub * tq, (sub + 1) * tq)
        qf = q_ref[rows, :].astype(F32)
        qsq = _dot_nt(ones, (qf * qf).astype(BF16))[0:1, :]
        bound = jnp.sqrt(qsq * kmax) * MLA_BOUND_SLACK
        num, den = attend(sub, bound)
        o_ref[rows, :] = jnp.transpose(num / den)
        denominators.append(jnp.min(den))

    for sub in range(n_sub):
        @pl.when(denominators[sub] < MLA_L_MIN)
        def _():
            m = jnp.full((1, tq), -jnp.inf, F32)
            for c in range(nk):
                m = jnp.maximum(m, jnp.max(scores_t(sub, c), axis=0, keepdims=True))
            num, den = attend(sub, m)
            o_ref[sub * tq:(sub + 1) * tq, :] = jnp.transpose(num / den)


def _mla(kmax, q, k, v, w1, w2, wo, tq=512, tk=512, n_sub=2):
    tb = n_sub * tq
    nb = SEQ // tb
    ff_slice = D_FF // (HEADS * nb)
    wo_slice = (MLA_WIDTH + NA_WIDTH) // (HEADS * nb)
    assert ff_slice % LANES == 0 and wo_slice % BF16_ROWS == 0
    step = lambda h, i: h * nb + i
    return pl.pallas_call(
        functools.partial(_mla_kernel, tq=tq, tk=tk),
        grid=(HEADS, nb),
        in_specs=[pl.BlockSpec(memory_space=pltpu.SMEM),
                  pl.BlockSpec((None, tb, QK_PAD), lambda h, i: (h, i, 0)),
                  pl.BlockSpec((None, SEQ, QK_PAD), lambda h, i: (h, 0, 0)),
                  pl.BlockSpec((None, VT_ROWS, SEQ), lambda h, i: (h, 0, 0)),
                  pl.BlockSpec((D_MODEL, ff_slice), lambda h, i: (0, step(h, i))),
                  pl.BlockSpec((ff_slice, D_MODEL), lambda h, i: (step(h, i), 0)),
                  pl.BlockSpec((wo_slice, D_MODEL), lambda h, i: (step(h, i), 0))],
        out_specs=[pl.BlockSpec((tb, MLA_V), lambda h, i: (i, h)),
                   pl.BlockSpec((D_MODEL, ff_slice), lambda h, i: (0, step(h, i))),
                   pl.BlockSpec((ff_slice, D_MODEL), lambda h, i: (step(h, i), 0)),
                   pl.BlockSpec((wo_slice, D_MODEL), lambda h, i: (step(h, i), 0))],
        out_shape=[jax.ShapeDtypeStruct((SEQ, MLA_WIDTH), F32),
                   jax.ShapeDtypeStruct((D_MODEL, D_FF), BF16),
                   jax.ShapeDtypeStruct((D_FF, D_MODEL), BF16),
                   jax.ShapeDtypeStruct((MLA_WIDTH + NA_WIDTH, D_MODEL), BF16)],
        scratch_shapes=[pltpu.VMEM((n_sub, SEQ, tq), BF16)],
        compiler_params=pltpu.CompilerParams(dimension_semantics=("arbitrary", "arbitrary"),
                                             vmem_limit_bytes=VMEM_LIMIT),
        name="mla",
    )(kmax, q, k, v, w1, w2, wo)


NA_QROWS = 4
NA_KROWS = 12
NA_TQ = NA_QROWS * GRID_W
NA_TK = NA_KROWS * GRID_W
NA_STEPS = GRID_ROWS // NA_QROWS
NA_BIAS_ROWS = 2 * NA_ROWS - 1
NA_BIAS_COLS = 2 * NA_COLS - 1
NA_STRIP = 32


def _na_bias_plan(variant):
    plan = {}
    for a in range(NA_QROWS):
        for b in range(NA_KROWS):
            if variant == "first":
                dr, ok = b - a + NA_ROWS - 1, b < NA_ROWS
            elif variant == "last":
                dr, ok = b - a - 1, b >= NA_KROWS - NA_ROWS
            else:
                dr, ok = b - a + NA_ROWS // 2 - 1, a <= b < a + NA_ROWS
            plan[(a, b)] = dr if ok else None
    return plan


def _na_kernel(rpb_ref, q_ref, k0_ref, k1_ref, k2_ref, v0_ref, v1_ref, v2_ref, o_ref, t_ref, bias_ref,
               s0_ref, s1_ref, p0_ref, p1_ref, l0_ref, l1_ref):
    g = pl.program_id(0)
    s_refs, p_refs, l_refs = (s0_ref, s1_ref), (p0_ref, p1_ref), (l0_ref, l1_ref)

    @pl.when(g == 0)
    def _():
        c = lax.broadcasted_iota(jnp.int32, (GRID_W, GRID_W), 0)
        kc = lax.broadcasted_iota(jnp.int32, (GRID_W, GRID_W), 1)
        start = jnp.clip(c - NA_COLS // 2, 0, GRID_W - NA_COLS)
        dc = jnp.where((kc >= start) & (kc < start + NA_COLS), kc - c + NA_COLS - 1, -1)

        def per_row(i, carry):
            tile = jnp.full((GRID_W, GRID_W), NEG_BIG, F32)
            for d in range(NA_BIAS_COLS):
                tile = jnp.where(dc == d, rpb_ref[i * NA_BIAS_COLS + d] * LOG2_E, tile)
            t_ref[i] = tile
            return carry

        lax.fori_loop(0, HEADS * NA_BIAS_ROWS, per_row, 0)

    def build(variant):
        plan = _na_bias_plan(variant)

        def per_head(h, carry):
            for (a, b), dr in plan.items():
                if dr is None:
                    tile = jnp.full((GRID_W, GRID_W), NEG_BIG, F32)
                else:
                    tile = t_ref[h * NA_BIAS_ROWS + dr]
                bias_ref[h, a * GRID_W:(a + 1) * GRID_W, b * GRID_W:(b + 1) * GRID_W] = tile
            return carry

        lax.fori_loop(0, HEADS, per_head, 0)

    pl.when(g == 0)(lambda: build("first"))
    pl.when(g == 1)(lambda: build("mid"))
    pl.when(g == NA_STEPS - 1)(lambda: build("last"))

    k_refs = (k0_ref, k1_ref, k2_ref)
    v_refs = (v0_ref, v1_ref, v2_ref)
    for h in range(HEADS):
        cols = slice(h * NA_HEAD_DIM, (h + 1) * NA_HEAD_DIM)
        s_ref, p_ref, l_ref = s_refs[h % 2], p_refs[h % 2], l_refs[h % 2]
        qh = q_ref[:, cols]
        for t in range(3):
            s_ref[:, t * NA_TQ:(t + 1) * NA_TQ] = _dot_nt(qh, k_refs[t][:, cols])
        for r in range(NA_TQ // NA_STRIP):
            rows = slice(r * NA_STRIP, (r + 1) * NA_STRIP)
            s = s_ref[rows, :] + bias_ref[h, rows, :]
            p = jnp.exp2(s - jnp.max(s, axis=-1, keepdims=True))
            l_ref[rows, :] = jnp.broadcast_to(jnp.sum(p, axis=-1, keepdims=True), (NA_STRIP, LANES))
            p_ref[rows, :] = p.astype(BF16)
        o = _dot(p_ref[:, 0:NA_TQ], v_refs[0][:, cols])
        for t in (1, 2):
            o = o + _dot(p_ref[:, t * NA_TQ:(t + 1) * NA_TQ], v_refs[t][:, cols])
        o_ref[:, cols] = o / l_ref[...]


def _na(rpb, nq, nk, nv):
    first_blk = lambda g: jnp.clip(g - 1, 0, NA_STEPS - 3)
    kv_specs = [pl.BlockSpec((NA_TQ, NA_WIDTH), functools.partial(lambda g, t: (first_blk(g) + t, 0), t=t))
                for t in range(3)]
    return pl.pallas_call(
        _na_kernel,
        grid=(NA_STEPS,),
        in_specs=[pl.BlockSpec(memory_space=pltpu.SMEM), pl.BlockSpec((NA_TQ, NA_WIDTH), lambda g: (g, 0))]
                 + kv_specs + kv_specs,
        out_specs=pl.BlockSpec((NA_TQ, NA_WIDTH), lambda g: (g, 0)),
        out_shape=jax.ShapeDtypeStruct((SEQ, NA_WIDTH), F32),
        scratch_shapes=[pltpu.VMEM((HEADS * NA_BIAS_ROWS, GRID_W, GRID_W), F32),
                        pltpu.VMEM((HEADS, NA_TQ, NA_TK), F32),
                        pltpu.VMEM((NA_TQ, NA_TK), F32), pltpu.VMEM((NA_TQ, NA_TK), F32),
                        pltpu.VMEM((NA_TQ, NA_TK), BF16), pltpu.VMEM((NA_TQ, NA_TK), BF16),
                        pltpu.VMEM((NA_TQ, LANES), F32), pltpu.VMEM((NA_TQ, LANES), F32)],
        compiler_params=pltpu.CompilerParams(dimension_semantics=("arbitrary",),
                                             vmem_limit_bytes=VMEM_LIMIT),
        name="na",
    )(rpb.astype(F32).reshape(-1), nq, nk, nk, nk, nv, nv, nv)


def _outproj_kernel(a_ref, n_ref, ga_ref, gn_ref, wb_ref, x_ref, gm_ref, o_ref, h_ref):
    an = _rms(a_ref[...], ga_ref[...]).astype(BF16)
    nn = _rms(n_ref[...], gn_ref[...]).astype(BF16)
    y = x_ref[...] + _dot(an, wb_ref[0:MLA_WIDTH, :]) + _dot(nn, wb_ref[MLA_WIDTH:, :])
    o_ref[...] = y
    h_ref[...] = _rms(y, gm_ref[...]).astype(BF16)


def _outproj(a, n, ga, gn, w, x, gm, tm=512):
    row = lambda w_: pl.BlockSpec((tm, w_), lambda i: (i, 0))
    return pl.pallas_call(
        _outproj_kernel,
        grid=(SEQ // tm,),
        in_specs=[row(MLA_WIDTH), row(NA_WIDTH), _const_spec((1, MLA_WIDTH)), _const_spec((1, NA_WIDTH)),
                  _const_spec((MLA_WIDTH + NA_WIDTH, D_MODEL)), row(D_MODEL), _const_spec((1, D_MODEL))],
        out_specs=[row(D_MODEL), row(D_MODEL)],
        out_shape=[jax.ShapeDtypeStruct((SEQ, D_MODEL), F32), jax.ShapeDtypeStruct((SEQ, D_MODEL), BF16)],
        compiler_params=pltpu.CompilerParams(dimension_semantics=("arbitrary",),
                                             vmem_limit_bytes=VMEM_LIMIT),
        name="outproj",
    )(a, n, ga, gn, w, x, gm)


def _mlp_kernel(x_ref, h_ref, w1_ref, w2_ref, gf_ref, o_ref, acc_ref):
    j = pl.program_id(1)
    last = pl.num_programs(1) - 1

    def chunk():
        hid = jnp.maximum(_dot(h_ref[...], w1_ref[...]), 0.0)
        return _dot((hid * hid).astype(BF16), w2_ref[...])

    @pl.when(j == 0)
    def _():
        acc_ref[...] = x_ref[...] + chunk()

    @pl.when((j > 0) & (j < last))
    def _():
        acc_ref[...] += chunk()

    @pl.when(j == last)
    def _():
        o_ref[...] = _rms(acc_ref[...] + chunk(), gf_ref[...])


def _mlp(x, h, w1, w2, gf, tm=512, tf=1024):
    assert D_FF // tf >= 2
    tile = lambda: pl.BlockSpec((tm, D_MODEL), lambda i, j: (i, 0))
    return pl.pallas_call(
        _mlp_kernel,
        grid=(SEQ // tm, D_FF // tf),
        in_specs=[tile(), tile(),
                  pl.BlockSpec((D_MODEL, tf), lambda i, j: (0, j)),
                  pl.BlockSpec((tf, D_MODEL), lambda i, j: (j, 0)),
                  _const_spec((1, D_MODEL))],
        out_specs=tile(),
        out_shape=jax.ShapeDtypeStruct((SEQ, D_MODEL), F32),
        scratch_shapes=[pltpu.VMEM((tm, D_MODEL), F32)],
        compiler_params=pltpu.CompilerParams(dimension_semantics=("arbitrary", "arbitrary"),
                                             vmem_limit_bytes=VMEM_LIMIT),
        name="mlp",
    )(x, h, w1, w2, gf)


def _rope_tables():
    half = MLA_ROPE // 2
    inv = ROPE_THETA ** (-jnp.arange(half, dtype=F32) / half)
    ang_a = (jnp.arange(SEQ // ROPE_SPLIT) * ROPE_SPLIT).astype(F32)[:, None] * inv[None, :]
    ang_b = jnp.arange(ROPE_SPLIT).astype(F32)[:, None] * inv[None, :]
    ca, sa = jnp.cos(ang_a)[:, None, :], jnp.sin(ang_a)[:, None, :]
    cb, sb = jnp.cos(ang_b)[None, :, :], jnp.sin(ang_b)[None, :, :]
    return (ca * cb - sa * sb).reshape(SEQ, half), (sa * cb + ca * sb).reshape(SEQ, half)


def kernel(x, attn_norm_g, w_in, q_norm_g, w_uq, kv_norm_g, w_ukv, na_rpb, mla_out_norm_g, na_out_norm_g,
           w_out, mlp_norm_g, w_ff1, w_ff2, final_norm_g):
    assert x.shape == (1, SEQ, D_MODEL)
    assert w_in.shape[0] == 1, "the final norm is fused into the (single) layer's MLP kernel"
    cos, sin = _rope_tables()
    xs = x[0]
    for l in range(1):
        wt = _win(jnp.swapaxes(w_in[l], 0, 1))
        wq = w_uq[l].reshape(MLA_Q_RANK, HEADS, MLA_QK)
        wuq = jnp.concatenate([wq[:, :, :MLA_NOPE].reshape(MLA_Q_RANK, HEADS * MLA_NOPE),
                               wq[:, :, MLA_NOPE:].reshape(MLA_Q_RANK, HEADS * MLA_ROPE)], axis=1).astype(BF16)
        wk = w_ukv[l].reshape(MLA_KV_RANK, HEADS, MLA_NOPE + MLA_V)
        wuk = wk[:, :, :MLA_NOPE].reshape(MLA_KV_RANK, HEADS * MLA_NOPE).astype(BF16)
        wvt = jnp.transpose(wk[:, :, MLA_NOPE:].reshape(MLA_KV_RANK, HEADS * MLA_V)).astype(BF16)

        q, k, vt, kn2, nq, nk, nv = _proj(xs, attn_norm_g[l][None], wt, q_norm_g[l][None], wuq,
                                    kv_norm_g[l][None], wuk, wvt, cos, sin)
        a_out, w1b, w2b, wob = _mla(jnp.max(kn2[:, :, 0], axis=0), q, k, vt, w_ff1[l], w_ff2[l], w_out[l])
        n_out = _na(na_rpb[l], nq, nk, nv)
        xs, hs = _outproj(a_out, n_out, mla_out_norm_g[l][None], na_out_norm_g[l][None],
                          wob, xs, mlp_norm_g[l][None])
        xs = _mlp(xs, hs, w1b, w2b, final_norm_g[None])
    return xs[None]
```

```python
import functools

import jax
import jax.numpy as jnp
from jax import lax
from jax.experimental import pallas as pl
from jax.experimental.pallas import tpu as pltpu

F32 = jnp.float32
BF16 = jnp.bfloat16

D_MODEL = 2048
SEQ = 8192
GRID_W = 64
GRID_ROWS = SEQ // GRID_W
HEADS = 8
MLA_NOPE = 128
MLA_ROPE = 64
MLA_V = 128
MLA_QK = MLA_NOPE + MLA_ROPE
MLA_Q_RANK = 512
MLA_KV_RANK = 256
ROPE_THETA = 10000.0
NA_HEAD_DIM = 128
NA_ROWS = 8
NA_COLS = 16
NA_WIDTH = HEADS * NA_HEAD_DIM
MLA_WIDTH = HEADS * MLA_V
D_FF = 4 * D_MODEL
NORM_EPS = 1e-6

LANES = 128
SUBLANES = 8
BF16_ROWS = 2 * SUBLANES
QK_PAD = 2 * LANES
VT_ROWS = MLA_V + BF16_ROWS
LOG2_E = 1.4426950408889634
ROPE_SPLIT = 64
NEG_BIG = -1e30

_R_CQ = 0
_R_CKV = _R_CQ + MLA_Q_RANK
_R_KR = _R_CKV + MLA_KV_RANK
_R_NQ = _R_KR + MLA_ROPE
_R_NK = _R_NQ + NA_WIDTH
_R_NV = _R_NK + NA_WIDTH
IN_COLS = _R_NV + NA_WIDTH
WIN_STEPS = 4

VMEM_LIMIT = 56 * 1024 * 1024


def _rms(xf, g):
    y = xf * lax.rsqrt(jnp.mean(xf * xf, axis=-1, keepdims=True) + NORM_EPS)
    return y * g


def _dot(a, b):
    return jnp.dot(a, b, preferred_element_type=F32)


def _dot_nt(a, b):
    return lax.dot_general(a, b, (((1,), (1,)), ((), ())), preferred_element_type=F32)


def _const_spec(shape):
    nd = len(shape)
    return pl.BlockSpec(shape, lambda *_: (0,) * nd, pipeline_mode=pl.Buffered(1))


def _rope_lanes(x, c, s, lo_half):
    partner = jnp.where(lo_half, pltpu.roll(x, LANES - MLA_ROPE // 2, 1), pltpu.roll(x, MLA_ROPE // 2, 1))
    return x * c + partner * s


def _win_kernel(w_ref, wb_ref):
    wb_ref[...] = w_ref[...].astype(BF16)


def _win(wt):
    rows = pl.BlockSpec((IN_COLS // WIN_STEPS, D_MODEL), lambda i: (i, 0))
    return pl.pallas_call(
        _win_kernel,
        grid=(WIN_STEPS,),
        in_specs=[rows],
        out_specs=rows,
        out_shape=jax.ShapeDtypeStruct((IN_COLS, D_MODEL), BF16),
        compiler_params=pltpu.CompilerParams(dimension_semantics=("arbitrary",),
                                             vmem_limit_bytes=VMEM_LIMIT),
        name="win",
    )(wt)


def _proj_kernel(x_ref, g_ref, wt_ref, qg_ref, wuq_ref, kvg_ref, wuk_ref, wvt_ref, cos_ref, sin_ref,
                 q_ref, k_ref, vt_ref, kn2_ref, nq_ref, nk_ref, nv_ref, cq_ref, ckv_ref, kr_ref, *, n_tiles):
    tm = x_ref.shape[0]
    i = pl.program_id(0)

    def stage1():
        h = _rms(x_ref[...], g_ref[...]).astype(BF16)

        def proj(lo, hi):
            return _dot_nt(h, wt_ref[lo:hi, :])

        cq_ref[...] = proj(_R_CQ, _R_CKV)
        ckv_ref[...] = proj(_R_CKV, _R_KR)
        wkr = wt_ref[_R_KR:_R_NQ, :]
        kr_ref[...] = _dot_nt(h, jnp.concatenate([wkr, wkr], axis=0))
        nq_ref[...] = (proj(_R_NQ, _R_NK) * (LOG2_E * NA_HEAD_DIM ** -0.5)).astype(BF16)
        nk_ref[...] = proj(_R_NK, _R_NV).astype(BF16)
        nv_ref[...] = proj(_R_NV, IN_COLS).astype(BF16)

    def stage2():
        c32, s32 = cos_ref[...], sin_ref[...]
        cos = jnp.concatenate([c32, c32] * (LANES // MLA_ROPE), axis=1)
        sin = jnp.concatenate([-s32, s32] * (LANES // MLA_ROPE), axis=1)
        lane = lax.broadcasted_iota(jnp.int32, (tm, LANES), 1)
        lo_half = (lane % MLA_ROPE) < (MLA_ROPE // 2)

        cqn = _rms(cq_ref[...], qg_ref[...]).astype(BF16)
        q = _dot(cqn, wuq_ref[...]) * (LOG2_E * MLA_QK ** -0.5)
        rope0 = HEADS * MLA_NOPE
        for pair in range(HEADS // 2):
            qr = _rope_lanes(q[:, rope0 + pair * LANES: rope0 + (pair + 1) * LANES], cos, sin, lo_half)
            qr = qr.astype(BF16)
            for hh in (2 * pair, 2 * pair + 1):
                q_ref[hh, :, 0:LANES] = q[:, hh * MLA_NOPE:(hh + 1) * MLA_NOPE].astype(BF16)
                q_ref[hh, :, LANES:QK_PAD] = qr

        ckvn = _rms(ckv_ref[...], kvg_ref[...])
        kn = _dot(ckvn.astype(BF16), wuk_ref[...])
        vt = _dot(wvt_ref[...], jnp.transpose(ckvn).astype(BF16))
        kr = _rope_lanes(kr_ref[...], cos, sin, lo_half)
        kr_even = jnp.where(lane < MLA_ROPE, kr, 0.0).astype(BF16)
        kr_odd = jnp.where(lane >= MLA_ROPE, kr, 0.0).astype(BF16)
        ones_row = jnp.where(lax.broadcasted_iota(jnp.int32, (VT_ROWS - MLA_V, tm), 0) == 0, 1.0, 0.0).astype(BF16)
        kr_sq = jnp.sum(jnp.where(lane < MLA_ROPE, kr * kr, 0.0), axis=-1, keepdims=True)
        for hh in range(HEADS):
            kn_h = kn[:, hh * MLA_NOPE:(hh + 1) * MLA_NOPE]
            k_sq = jnp.sum(kn_h * kn_h, axis=-1, keepdims=True) + kr_sq
            kn2_ref[0, hh:hh + 1, :] = jnp.broadcast_to(jnp.max(k_sq, axis=0, keepdims=True), (1, LANES))
            k_ref[hh, :, 0:LANES] = kn_h.astype(BF16)
            k_ref[hh, :, LANES:QK_PAD] = kr_even if hh % 2 == 0 else kr_odd
            vt_ref[hh, 0:MLA_V, :] = vt[hh * MLA_V:(hh + 1) * MLA_V, :].astype(BF16)
            vt_ref[hh, MLA_V:VT_ROWS, :] = ones_row

    pl.when(i == 0)(stage1)

    @pl.when((i > 0) & (i < n_tiles))
    def _():
        stage2()
        stage1()

    pl.when(i == n_tiles)(stage2)


def _proj(x, g, wt, qg, wuq, kvg, wuk, wvt, cos, sin, tm=512):
    n_tiles = SEQ // tm
    cur = lambda i: jnp.minimum(i, n_tiles - 1)
    prev = lambda i: jnp.maximum(i - 1, 0)
    row = lambda w, tile: pl.BlockSpec((tm, w), lambda i: (tile(i), 0))
    head = lambda w: pl.BlockSpec((HEADS, tm, w), lambda i: (0, prev(i), 0))
    return pl.pallas_call(
        functools.partial(_proj_kernel, n_tiles=n_tiles),
        grid=(n_tiles + 1,),
        in_specs=[row(D_MODEL, cur), _const_spec((1, D_MODEL)), _const_spec((IN_COLS, D_MODEL)),
                  _const_spec((1, MLA_Q_RANK)), _const_spec((MLA_Q_RANK, HEADS * MLA_QK)),
                  _const_spec((1, MLA_KV_RANK)), _const_spec((MLA_KV_RANK, HEADS * MLA_NOPE)),
                  _const_spec((HEADS * MLA_V, MLA_KV_RANK)),
                  row(MLA_ROPE // 2, prev), row(MLA_ROPE // 2, prev)],
        out_specs=[head(QK_PAD), head(QK_PAD),
                   pl.BlockSpec((HEADS, VT_ROWS, tm), lambda i: (0, 0, prev(i))),
                   pl.BlockSpec((1, HEADS, LANES), lambda i: (prev(i), 0, 0)),
                   row(NA_WIDTH, cur), row(NA_WIDTH, cur), row(NA_WIDTH, cur)],
        out_shape=[jax.ShapeDtypeStruct((HEADS, SEQ, QK_PAD), BF16),
                   jax.ShapeDtypeStruct((HEADS, SEQ, QK_PAD), BF16),
                   jax.ShapeDtypeStruct((HEADS, VT_ROWS, SEQ), BF16),
                   jax.ShapeDtypeStruct((n_tiles, HEADS, LANES), F32),
                   jax.ShapeDtypeStruct((SEQ, NA_WIDTH), BF16),
                   jax.ShapeDtypeStruct((SEQ, NA_WIDTH), BF16),
                   jax.ShapeDtypeStruct((SEQ, NA_WIDTH), BF16)],
        scratch_shapes=[pltpu.VMEM((tm, MLA_Q_RANK), F32), pltpu.VMEM((tm, MLA_KV_RANK), F32),
                        pltpu.VMEM((tm, LANES), F32)],
        compiler_params=pltpu.CompilerParams(dimension_semantics=("arbitrary",),
                                             vmem_limit_bytes=VMEM_LIMIT),
        name="proj",
    )(x, g, wt, qg, wuq, kvg, wuk, wvt, cos, sin)


MLA_BOUND_SLACK = 1.0 + 2.0 ** -6
MLA_L_MIN = 2.0 ** -60


def _mla_kernel(kmax_ref, q_ref, k_ref, vt_ref, w1_ref, w2_ref, wo_ref, o_ref, w1b_ref, w2b_ref, wob_ref,
                pt_ref, *, tq, tk):
    w1b_ref[...] = w1_ref[...].astype(BF16)
    w2b_ref[...] = w2_ref[...].astype(BF16)
    wob_ref[...] = wo_ref[...].astype(BF16)
    n_sub = q_ref.shape[0] // tq
    nk = SEQ // tk

    kmax = kmax_ref[pl.program_id(0)]

    def scores_t(sub, c):
        return _dot_nt(k_ref[c * tk:(c + 1) * tk, :], q_ref[sub * tq:(sub + 1) * tq, :])

    def attend(sub, shift):
        den = jnp.zeros((1, tq), F32)
        for c in range(nk):
            p = jnp.exp2(scores_t(sub, c) - shift)
            den = den + jnp.sum(p, axis=0, keepdims=True)
            pt_ref[sub, c * tk:(c + 1) * tk, :] = p.astype(BF16)
        return _dot(vt_ref[0:MLA_V, :], pt_ref[sub]), den

    ones = jnp.ones((SUBLANES, QK_PAD), BF16)
    denominators = []
    for sub in range(n_sub):
        rows = slice(sub * tq, (sub + 1) * tq)
        qf = q_ref[rows, :].astype(F32)
        qsq = _dot_nt(ones, (qf * qf).astype(BF16))[0:1, :]
        bound = jnp.sqrt(qsq * kmax) * MLA_BOUND_SLACK
        num, den = attend(sub, bound)
        o_ref[rows, :] = jnp.transpose(num / den)
        denominators.append(jnp.min(den))

    for sub in range(n_sub):
        @pl.when(denominators[sub] < MLA_L_MIN)
        def _():
            m = jnp.full((1, tq), -jnp.inf, F32)
            for c in range(nk):
                m = jnp.maximum(m, jnp.max(scores_t(sub, c), axis=0, keepdims=True))
            num, den = attend(sub, m)
            o_ref[sub * tq:(sub + 1) * tq, :] = jnp.transpose(num / den)


def _mla(kmax, q, k, v, w1, w2, wo, tq=512, tk=512, n_sub=2):
    tb = n_sub * tq
    nb = SEQ // tb
    ff_slice = D_FF // (HEADS * nb)
    wo_slice = (MLA_WIDTH + NA_WIDTH) // (HEADS * nb)
    assert ff_slice % LANES == 0 and wo_slice % BF16_ROWS == 0
    step = lambda h, i: h * nb + i
    return pl.pallas_call(
        functools.partial(_mla_kernel, tq=tq, tk=tk),
        grid=(HEADS, nb),
        in_specs=[pl.BlockSpec(memory_space=pltpu.SMEM),
                  pl.BlockSpec((None, tb, QK_PAD), lambda h, i: (h, i, 0)),
                  pl.BlockSpec((None, SEQ, QK_PAD), lambda h, i: (h, 0, 0)),
                  pl.BlockSpec((None, VT_ROWS, SEQ), lambda h, i: (h, 0, 0)),
                  pl.BlockSpec((D_MODEL, ff_slice), lambda h, i: (0, step(h, i))),
                  pl.BlockSpec((ff_slice, D_MODEL), lambda h, i: (step(h, i), 0)),
                  pl.BlockSpec((wo_slice, D_MODEL), lambda h, i: (step(h, i), 0))],
        out_specs=[pl.BlockSpec((tb, MLA_V), lambda h, i: (i, h)),
                   pl.BlockSpec((D_MODEL, ff_slice), lambda h, i: (0, step(h, i))),
                   pl.BlockSpec((ff_slice, D_MODEL), lambda h, i: (step(h, i), 0)),
                   pl.BlockSpec((wo_slice, D_MODEL), lambda h, i: (step(h, i), 0))],
        out_shape=[jax.ShapeDtypeStruct((SEQ, MLA_WIDTH), F32),
                   jax.ShapeDtypeStruct((D_MODEL, D_FF), BF16),
                   jax.ShapeDtypeStruct((D_FF, D_MODEL), BF16),
                   jax.ShapeDtypeStruct((MLA_WIDTH + NA_WIDTH, D_MODEL), BF16)],
        scratch_shapes=[pltpu.VMEM((n_sub, SEQ, tq), BF16)],
        compiler_params=pltpu.CompilerParams(dimension_semantics=("arbitrary", "arbitrary"),
                                             vmem_limit_bytes=VMEM_LIMIT),
        name="mla",
    )(kmax, q, k, v, w1, w2, wo)


NA_QROWS = 4
NA_KROWS = 12
NA_TQ = NA_QROWS * GRID_W
NA_TK = NA_KROWS * GRID_W
NA_STEPS = GRID_ROWS // NA_QROWS
NA_BIAS_ROWS = 2 * NA_ROWS - 1
NA_BIAS_COLS = 2 * NA_COLS - 1
NA_STRIP = 32


def _na_bias_plan(variant):
    plan = {}
    for a in range(NA_QROWS):
        for b in range(NA_KROWS):
            if variant == "first":
                dr, ok = b - a + NA_ROWS - 1, b < NA_ROWS
            elif variant == "last":
                dr, ok = b - a - 1, b >= NA_KROWS - NA_ROWS
            else:
                dr, ok = b - a + NA_ROWS // 2 - 1, a <= b < a + NA_ROWS
            plan[(a, b)] = dr if ok else None
    return plan


def _na_kernel(rpb_ref, q_ref, k0_ref, k1_ref, k2_ref, v0_ref, v1_ref, v2_ref, o_ref, t_ref, bias_ref,
               s0_ref, s1_ref, p0_ref, p1_ref, l0_ref, l1_ref):
    g = pl.program_id(0)
    s_refs, p_refs, l_refs = (s0_ref, s1_ref), (p0_ref, p1_ref), (l0_ref, l1_ref)

    @pl.when(g == 0)
    def _():
        c = lax.broadcasted_iota(jnp.int32, (GRID_W, GRID_W), 0)
        kc = lax.broadcasted_iota(jnp.int32, (GRID_W, GRID_W), 1)
        start = jnp.clip(c - NA_COLS // 2, 0, GRID_W - NA_COLS)
        dc = jnp.where((kc >= start) & (kc < start + NA_COLS), kc - c + NA_COLS - 1, -1)

        def per_row(i, carry):
            tile = jnp.full((GRID_W, GRID_W), NEG_BIG, F32)
            for d in range(NA_BIAS_COLS):
                tile = jnp.where(dc == d, rpb_ref[i * NA_BIAS_COLS + d] * LOG2_E, tile)
            t_ref[i] = tile
            return carry

        lax.fori_loop(0, HEADS * NA_BIAS_ROWS, per_row, 0)

    def build(variant):
        plan = _na_bias_plan(variant)

        def per_head(h, carry):
            for (a, b), dr in plan.items():
                if dr is None:
                    tile = jnp.full((GRID_W, GRID_W), NEG_BIG, F32)
                else:
                    tile = t_ref[h * NA_BIAS_ROWS + dr]
                bias_ref[h, a * GRID_W:(a + 1) * GRID_W, b * GRID_W:(b + 1) * GRID_W] = tile
            return carry

        lax.fori_loop(0, HEADS, per_head, 0)

    pl.when(g == 0)(lambda: build("first"))
    pl.when(g == 1)(lambda: build("mid"))
    pl.when(g == NA_STEPS - 1)(lambda: build("last"))

    k_refs = (k0_ref, k1_ref, k2_ref)
    v_refs = (v0_ref, v1_ref, v2_ref)
    for h in range(HEADS):
        cols = slice(h * NA_HEAD_DIM, (h + 1) * NA_HEAD_DIM)
        s_ref, p_ref, l_ref = s_refs[h % 2], p_refs[h % 2], l_refs[h % 2]
        qh = q_ref[:, cols]
        for t in range(3):
            s_ref[:, t * NA_TQ:(t + 1) * NA_TQ] = _dot_nt(qh, k_refs[t][:, cols])
        for r in range(NA_TQ // NA_STRIP):
            rows = slice(r * NA_STRIP, (r + 1) * NA_STRIP)
            s = s_ref[rows, :] + bias_ref[h, rows, :]
            p = jnp.exp2(s - jnp.max(s, axis=-1, keepdims=True))
            l_ref[rows, :] = jnp.broadcast_to(jnp.sum(p, axis=-1, keepdims=True), (NA_STRIP, LANES))
            p_ref[rows, :] = p.astype(BF16)
        o = _dot(p_ref[:, 0:NA_TQ], v_refs[0][:, cols])
        for t in (1, 2):
            o = o + _dot(p_ref[:, t * NA_TQ:(t + 1) * NA_TQ], v_refs[t][:, cols])
        o_ref[:, cols] = o / l_ref[...]


def _na(rpb, nq, nk, nv):
    first_blk = lambda g: jnp.clip(g - 1, 0, NA_STEPS - 3)
    kv_specs = [pl.BlockSpec((NA_TQ, NA_WIDTH), functools.partial(lambda g, t: (first_blk(g) + t, 0), t=t))
                for t in range(3)]
    return pl.pallas_call(
        _na_kernel,
        grid=(NA_STEPS,),
        in_specs=[pl.BlockSpec(memory_space=pltpu.SMEM), pl.BlockSpec((NA_TQ, NA_WIDTH), lambda g: (g, 0))]
                 + kv_specs + kv_specs,
        out_specs=pl.BlockSpec((NA_TQ, NA_WIDTH), lambda g: (g, 0)),
        out_shape=jax.ShapeDtypeStruct((SEQ, NA_WIDTH), F32),
        scratch_shapes=[pltpu.VMEM((HEADS * NA_BIAS_ROWS, GRID_W, GRID_W), F32),
                        pltpu.VMEM((HEADS, NA_TQ, NA_TK), F32),
                        pltpu.VMEM((NA_TQ, NA_TK), F32), pltpu.VMEM((NA_TQ, NA_TK), F32),
                        pltpu.VMEM((NA_TQ, NA_TK), BF16), pltpu.VMEM((NA_TQ, NA_TK), BF16),
                        pltpu.VMEM((NA_TQ, LANES), F32), pltpu.VMEM((NA_TQ, LANES), F32)],
        compiler_params=pltpu.CompilerParams(dimension_semantics=("arbitrary",),
                                             vmem_limit_bytes=VMEM_LIMIT),
        name="na",
    )(rpb.astype(F32).reshape(-1), nq, nk, nk, nk, nv, nv, nv)


def _outproj_kernel(a_ref, n_ref, ga_ref, gn_ref, wb_ref, x_ref, gm_ref, o_ref, h_ref):
    an = _rms(a_ref[...], ga_ref[...]).astype(BF16)
    nn = _rms(n_ref[...], gn_ref[...]).astype(BF16)
    y = x_ref[...] + _dot(an, wb_ref[0:MLA_WIDTH, :]) + _dot(nn, wb_ref[MLA_WIDTH:, :])
    o_ref[...] = y
    h_ref[...] = _rms(y, gm_ref[...]).astype(BF16)


def _outproj(a, n, ga, gn, w, x, gm, tm=512):
    row = lambda w_: pl.BlockSpec((tm, w_), lambda i: (i, 0))
    return pl.pallas_call(
        _outproj_kernel,
        grid=(SEQ // tm,),
        in_specs=[row(MLA_WIDTH), row(NA_WIDTH), _const_spec((1, MLA_WIDTH)), _const_spec((1, NA_WIDTH)),
                  _const_spec((MLA_WIDTH + NA_WIDTH, D_MODEL)), row(D_MODEL), _const_spec((1, D_MODEL))],
        out_specs=[row(D_MODEL), row(D_MODEL)],
        out_shape=[jax.ShapeDtypeStruct((SEQ, D_MODEL), F32), jax.ShapeDtypeStruct((SEQ, D_MODEL), BF16)],
        compiler_params=pltpu.CompilerParams(dimension_semantics=("arbitrary",),
                                             vmem_limit_bytes=VMEM_LIMIT),
        name="outproj",
    )(a, n, ga, gn, w, x, gm)


def _mlp_kernel(x_ref, h_ref, w1_ref, w2_ref, gf_ref, o_ref, acc_ref):
    j = pl.program_id(1)
    last = pl.num_programs(1) - 1

    def chunk():
        hid = jnp.maximum(_dot(h_ref[...], w1_ref[...]), 0.0)
        return _dot((hid * hid).astype(BF16), w2_ref[...])

    @pl.when(j == 0)
    def _():
        acc_ref[...] = x_ref[...] + chunk()

    @pl.when((j > 0) & (j < last))
    def _():
        acc_ref[...] += chunk()

    @pl.when(j == last)
    def _():
        o_ref[...] = _rms(acc_ref[...] + chunk(), gf_ref[...])


def _mlp(x, h, w1, w2, gf, tm=512, tf=1024):
    assert D_FF // tf >= 2
    tile = lambda: pl.BlockSpec((tm, D_MODEL), lambda i, j: (i, 0))
    return pl.pallas_call(
        _mlp_kernel,
        grid=(SEQ // tm, D_FF // tf),
        in_specs=[tile(), tile(),
                  pl.BlockSpec((D_MODEL, tf), lambda i, j: (0, j)),
                  pl.BlockSpec((tf, D_MODEL), lambda i, j: (j, 0)),
                  _const_spec((1, D_MODEL))],
        out_specs=tile(),
        out_shape=jax.ShapeDtypeStruct((SEQ, D_MODEL), F32),
        scratch_shapes=[pltpu.VMEM((tm, D_MODEL), F32)],
        compiler_params=pltpu.CompilerParams(dimension_semantics=("arbitrary", "arbitrary"),
                                             vmem_limit_bytes=VMEM_LIMIT),
        name="mlp",
    )(x, h, w1, w2, gf)


def _rope_tables():
    half = MLA_ROPE // 2
    inv = ROPE_THETA ** (-jnp.arange(half, dtype=F32) / half)
    ang_a = (jnp.arange(SEQ // ROPE_SPLIT) * ROPE_SPLIT).astype(F32)[:, None] * inv[None, :]
    ang_b = jnp.arange(ROPE_SPLIT).astype(F32)[:, None] * inv[None, :]
    ca, sa = jnp.cos(ang_a)[:, None, :], jnp.sin(ang_a)[:, None, :]
    cb, sb = jnp.cos(ang_b)[None, :, :], jnp.sin(ang_b)[None, :, :]
    return (ca * cb - sa * sb).reshape(SEQ, half), (sa * cb + ca * sb).reshape(SEQ, half)


def kernel(x, attn_norm_g, w_in, q_norm_g, w_uq, kv_norm_g, w_ukv, na_rpb, mla_out_norm_g, na_out_norm_g,
           w_out, mlp_norm_g, w_ff1, w_ff2, final_norm_g):
    assert x.shape == (1, SEQ, D_MODEL)
    assert w_in.shape[0] == 1, "the final norm is fused into the (single) layer's MLP kernel"
    cos, sin = _rope_tables()
    xs = x[0]
    for l in range(1):
        wt = _win(jnp.swapaxes(w_in[l], 0, 1))
        wq = w_uq[l].reshape(MLA_Q_RANK, HEADS, MLA_QK)
        wuq = jnp.concatenate([wq[:, :, :MLA_NOPE].reshape(MLA_Q_RANK, HEADS * MLA_NOPE),
                               wq[:, :, MLA_NOPE:].reshape(MLA_Q_RANK, HEADS * MLA_ROPE)], axis=1).astype(BF16)
        wk = w_ukv[l].reshape(MLA_KV_RANK, HEADS, MLA_NOPE + MLA_V)
        wuk = wk[:, :, :MLA_NOPE].reshape(MLA_KV_RANK, HEADS * MLA_NOPE).astype(BF16)
        wvt = jnp.transpose(wk[:, :, MLA_NOPE:].reshape(MLA_KV_RANK, HEADS * MLA_V)).astype(BF16)

        q, k, vt, kn2, nq, nk, nv = _proj(xs, attn_norm_g[l][None], wt, q_norm_g[l][None], wuq,
                                    kv_norm_g[l][None], wuk, wvt, cos, sin)
        a_out, w1b, w2b, wob = _mla(jnp.max(kn2[:, :, 0], axis=0), q, k, vt, w_ff1[l], w_ff2[l], w_out[l])
        n_out = _na(na_rpb[l], nq, nk, nv)
        xs, hs = _outproj(a_out, n_out, mla_out_norm_g[l][None], na_out_norm_g[l][None],
                          wob, xs, mlp_norm_g[l][None])
        xs = _mlp(xs, hs, w1b, w2b, final_norm_g[None])
    return xs[None]
```

```python
import functools

import jax
import jax.numpy as jnp
from jax import lax
from jax.experimental import pallas as pl
from jax.experimental.pallas import tpu as pltpu

F32 = jnp.float32
BF16 = jnp.bfloat16

D_MODEL = 2048
SEQ = 8192
GRID_W = 64
GRID_ROWS = SEQ // GRID_W
HEADS = 8
MLA_NOPE = 128
MLA_ROPE = 64
MLA_V = 128
MLA_QK = MLA_NOPE + MLA_ROPE
MLA_Q_RANK = 512
MLA_KV_RANK = 256
ROPE_THETA = 10000.0
NA_HEAD_DIM = 128
NA_ROWS = 8
NA_COLS = 16
NA_WIDTH = HEADS * NA_HEAD_DIM
MLA_WIDTH = HEADS * MLA_V
D_FF = 4 * D_MODEL
NORM_EPS = 1e-6

LANES = 128
SUBLANES = 8
BF16_ROWS = 2 * SUBLANES
QK_PAD = 2 * LANES
LOG2_E = 1.4426950408889634
ROPE_SPLIT = 64
NEG_BIG = -1e30

_R_CQ = 0
_R_CKV = _R_CQ + MLA_Q_RANK
_R_KR = _R_CKV + MLA_KV_RANK
_R_NQ = _R_KR + MLA_ROPE
_R_NK = _R_NQ + NA_WIDTH
_R_NV = _R_NK + NA_WIDTH
IN_COLS = _R_NV + NA_WIDTH
WIN_STEPS = 4

VMEM_LIMIT = 56 * 1024 * 1024


def _rms(xf, g):
    y = xf * lax.rsqrt(jnp.mean(xf * xf, axis=-1, keepdims=True) + NORM_EPS)
    return y * g


def _dot(a, b):
    return jnp.dot(a, b, preferred_element_type=F32)


def _dot_nt(a, b):
    return lax.dot_general(a, b, (((1,), (1,)), ((), ())), preferred_element_type=F32)


def _const_spec(shape):
    nd = len(shape)
    return pl.BlockSpec(shape, lambda *_: (0,) * nd, pipeline_mode=pl.Buffered(1))


def _rope_lanes(x, c, s, lo_half):
    partner = jnp.where(lo_half, pltpu.roll(x, LANES - MLA_ROPE // 2, 1), pltpu.roll(x, MLA_ROPE // 2, 1))
    return x * c + partner * s


def _win_kernel(w_ref, wb_ref):
    wb_ref[...] = w_ref[...].astype(BF16)


def _win(wt):
    rows = pl.BlockSpec((IN_COLS // WIN_STEPS, D_MODEL), lambda i: (i, 0))
    return pl.pallas_call(
        _win_kernel,
        grid=(WIN_STEPS,),
        in_specs=[rows],
        out_specs=rows,
        out_shape=jax.ShapeDtypeStruct((IN_COLS, D_MODEL), BF16),
        compiler_params=pltpu.CompilerParams(dimension_semantics=("arbitrary",),
                                             vmem_limit_bytes=VMEM_LIMIT),
        name="win",
    )(wt)


def _proj_kernel(x_ref, g_ref, wt_ref, qg_ref, wuq_ref, kvg_ref, wuk_ref, wvt_ref, cos_ref, sin_ref,
                 q_ref, k_ref, vt_ref, kn2_ref, nq_ref, nk_ref, nv_ref, cq_ref, ckv_ref, kr_ref, *, n_tiles):
    tm = x_ref.shape[0]
    i = pl.program_id(0)

    def stage1():
        h = _rms(x_ref[...], g_ref[...]).astype(BF16)

        def proj(lo, hi):
            return _dot_nt(h, wt_ref[lo:hi, :])

        cq_ref[...] = proj(_R_CQ, _R_CKV)
        ckv_ref[...] = proj(_R_CKV, _R_KR)
        wkr = wt_ref[_R_KR:_R_NQ, :]
        kr_ref[...] = _dot_nt(h, jnp.concatenate([wkr, wkr], axis=0))
        nq_ref[...] = (proj(_R_NQ, _R_NK) * (LOG2_E * NA_HEAD_DIM ** -0.5)).astype(BF16)
        nk_ref[...] = proj(_R_NK, _R_NV).astype(BF16)
        nv_ref[...] = proj(_R_NV, IN_COLS).astype(BF16)

    def stage2():
        c32, s32 = cos_ref[...], sin_ref[...]
        cos = jnp.concatenate([c32, c32] * (LANES // MLA_ROPE), axis=1)
        sin = jnp.concatenate([-s32, s32] * (LANES // MLA_ROPE), axis=1)
        lane = lax.broadcasted_iota(jnp.int32, (tm, LANES), 1)
        lo_half = (lane % MLA_ROPE) < (MLA_ROPE // 2)

        cqn = _rms(cq_ref[...], qg_ref[...]).astype(BF16)
        q = _dot(cqn, wuq_ref[...]) * (LOG2_E * MLA_QK ** -0.5)
        rope0 = HEADS * MLA_NOPE
        for pair in range(HEADS // 2):
            qr = _rope_lanes(q[:, rope0 + pair * LANES: rope0 + (pair + 1) * LANES], cos, sin, lo_half)
            qr = qr.astype(BF16)
            for hh in (2 * pair, 2 * pair + 1):
                q_ref[hh, :, 0:LANES] = q[:, hh * MLA_NOPE:(hh + 1) * MLA_NOPE].astype(BF16)
                q_ref[hh, :, LANES:QK_PAD] = qr

        ckvn = _rms(ckv_ref[...], kvg_ref[...])
        kn = _dot(ckvn.astype(BF16), wuk_ref[...])
        vt = _dot(wvt_ref[...], jnp.transpose(ckvn).astype(BF16))
        kr = _rope_lanes(kr_ref[...], cos, sin, lo_half)
        kr_even = jnp.where(lane < MLA_ROPE, kr, 0.0).astype(BF16)
        kr_odd = jnp.where(lane >= MLA_ROPE, kr, 0.0).astype(BF16)
        kr_sq = jnp.sum(jnp.where(lane < MLA_ROPE, kr * kr, 0.0), axis=-1, keepdims=True)
        for hh in range(HEADS):
            kn_h = kn[:, hh * MLA_NOPE:(hh + 1) * MLA_NOPE]
            k_sq = jnp.sum(kn_h * kn_h, axis=-1, keepdims=True) + kr_sq
            kn2_ref[0, hh:hh + 1, :] = jnp.broadcast_to(jnp.max(k_sq, axis=0, keepdims=True), (1, LANES))
            k_ref[hh, :, 0:LANES] = kn_h.astype(BF16)
            k_ref[hh, :, LANES:QK_PAD] = kr_even if hh % 2 == 0 else kr_odd
            vt_ref[hh] = vt[hh * MLA_V:(hh + 1) * MLA_V, :].astype(BF16)

    pl.when(i == 0)(stage1)

    @pl.when((i > 0) & (i < n_tiles))
    def _():
        stage2()
        stage1()

    pl.when(i == n_tiles)(stage2)


def _proj(x, g, wt, qg, wuq, kvg, wuk, wvt, cos, sin, tm=512):
    n_tiles = SEQ // tm
    cur = lambda i: jnp.minimum(i, n_tiles - 1)
    prev = lambda i: jnp.maximum(i - 1, 0)
    row = lambda w, tile: pl.BlockSpec((tm, w), lambda i: (tile(i), 0))
    head = lambda w: pl.BlockSpec((HEADS, tm, w), lambda i: (0, prev(i), 0))
    return pl.pallas_call(
        functools.partial(_proj_kernel, n_tiles=n_tiles),
        grid=(n_tiles + 1,),
        in_specs=[row(D_MODEL, cur), _const_spec((1, D_MODEL)), _const_spec((IN_COLS, D_MODEL)),
                  _const_spec((1, MLA_Q_RANK)), _const_spec((MLA_Q_RANK, HEADS * MLA_QK)),
                  _const_spec((1, MLA_KV_RANK)), _const_spec((MLA_KV_RANK, HEADS * MLA_NOPE)),
                  _const_spec((HEADS * MLA_V, MLA_KV_RANK)),
                  row(MLA_ROPE // 2, prev), row(MLA_ROPE // 2, prev)],
        out_specs=[head(QK_PAD), head(QK_PAD),
                   pl.BlockSpec((HEADS, MLA_V, tm), lambda i: (0, 0, prev(i))),
                   pl.BlockSpec((1, HEADS, LANES), lambda i: (prev(i), 0, 0)),
                   row(NA_WIDTH, cur), row(NA_WIDTH, cur), row(NA_WIDTH, cur)],
        out_shape=[jax.ShapeDtypeStruct((HEADS, SEQ, QK_PAD), BF16),
                   jax.ShapeDtypeStruct((HEADS, SEQ, QK_PAD), BF16),
                   jax.ShapeDtypeStruct((HEADS, MLA_V, SEQ), BF16),
                   jax.ShapeDtypeStruct((n_tiles, HEADS, LANES), F32),
                   jax.ShapeDtypeStruct((SEQ, NA_WIDTH), BF16),
                   jax.ShapeDtypeStruct((SEQ, NA_WIDTH), BF16),
                   jax.ShapeDtypeStruct((SEQ, NA_WIDTH), BF16)],
        scratch_shapes=[pltpu.VMEM((tm, MLA_Q_RANK), F32), pltpu.VMEM((tm, MLA_KV_RANK), F32),
                        pltpu.VMEM((tm, LANES), F32)],
        compiler_params=pltpu.CompilerParams(dimension_semantics=("arbitrary",),
                                             vmem_limit_bytes=VMEM_LIMIT),
        name="proj",
    )(x, g, wt, qg, wuq, kvg, wuk, wvt, cos, sin)


MLA_BOUND_SLACK = 1.0 + 2.0 ** -6
MLA_L_MIN = 2.0 ** -60


def _mla_kernel(kmax_ref, q_ref, k_ref, vt_ref, w1_ref, w2_ref, wo_ref, o_ref, w1b_ref, w2b_ref, wob_ref,
                pt_ref, *, tq, tk):
    w1b_ref[...] = w1_ref[...].astype(BF16)
    w2b_ref[...] = w2_ref[...].astype(BF16)
    wob_ref[...] = wo_ref[...].astype(BF16)
    n_sub = q_ref.shape[0] // tq
    nk = SEQ // tk

    kmax = kmax_ref[pl.program_id(0)]

    def scores_t(sub, c):
        return _dot_nt(k_ref[c * tk:(c + 1) * tk, :], q_ref[sub * tq:(sub + 1) * tq, :])

    def attend(sub, shift):
        den = jnp.zeros((1, tq), F32)
        for c in range(nk):
            p = jnp.exp2(scores_t(sub, c) - shift)
            den = den + jnp.sum(p, axis=0, keepdims=True)
            pt_ref[sub, c * tk:(c + 1) * tk, :] = p.astype(BF16)
        return _dot(vt_ref[...], pt_ref[sub]), den

    ones = jnp.ones((SUBLANES, QK_PAD), BF16)
    denominators = []
    for sub in range(n_sub):
        rows = slice(sub * tq, (sub + 1) * tq)
        qf = q_ref[rows, :].astype(F32)
        qsq = _dot_nt(ones, (qf * qf).astype(BF16))[0:1, :]
        bound = jnp.sqrt(qsq * kmax) * MLA_BOUND_SLACK
        num, den = attend(sub, bound)
        o_ref[rows, :] = jnp.transpose(num / den)
        denominators.append(jnp.min(den))

    for sub in range(n_sub):
        @pl.when(denominators[sub] < MLA_L_MIN)
        def _():
            m = jnp.full((1, tq), -jnp.inf, F32)
            for c in range(nk):
                m = jnp.maximum(m, jnp.max(scores_t(sub, c), axis=0, keepdims=True))
            num, den = attend(sub, m)
            o_ref[sub * tq:(sub + 1) * tq, :] = jnp.transpose(num / den)


def _mla(kmax, q, k, v, w1, w2, wo, tq=512, tk=512, n_sub=2):
    tb = n_sub * tq
    nb = SEQ // tb
    ff_slice = D_FF // (HEADS * nb)
    wo_slice = (MLA_WIDTH + NA_WIDTH) // (HEADS * nb)
    assert ff_slice % LANES == 0 and wo_slice % BF16_ROWS == 0
    step = lambda h, i: h * nb + i
    return pl.pallas_call(
        functools.partial(_mla_kernel, tq=tq, tk=tk),
        grid=(HEADS, nb),
        in_specs=[pl.BlockSpec(memory_space=pltpu.SMEM),
                  pl.BlockSpec((None, tb, QK_PAD), lambda h, i: (h, i, 0)),
                  pl.BlockSpec((None, SEQ, QK_PAD), lambda h, i: (h, 0, 0)),
                  pl.BlockSpec((None, MLA_V, SEQ), lambda h, i: (h, 0, 0)),
                  pl.BlockSpec((D_MODEL, ff_slice), lambda h, i: (0, step(h, i))),
                  pl.BlockSpec((ff_slice, D_MODEL), lambda h, i: (step(h, i), 0)),
                  pl.BlockSpec((wo_slice, D_MODEL), lambda h, i: (step(h, i), 0))],
        out_specs=[pl.BlockSpec((tb, MLA_V), lambda h, i: (i, h)),
                   pl.BlockSpec((D_MODEL, ff_slice), lambda h, i: (0, step(h, i))),
                   pl.BlockSpec((ff_slice, D_MODEL), lambda h, i: (step(h, i), 0)),
                   pl.BlockSpec((wo_slice, D_MODEL), lambda h, i: (step(h, i), 0))],
        out_shape=[jax.ShapeDtypeStruct((SEQ, MLA_WIDTH), F32),
                   jax.ShapeDtypeStruct((D_MODEL, D_FF), BF16),
                   jax.ShapeDtypeStruct((D_FF, D_MODEL), BF16),
                   jax.ShapeDtypeStruct((MLA_WIDTH + NA_WIDTH, D_MODEL), BF16)],
        scratch_shapes=[pltpu.VMEM((n_sub, SEQ, tq), BF16)],
        compiler_params=pltpu.CompilerParams(dimension_semantics=("arbitrary", "arbitrary"),
                                             vmem_limit_bytes=VMEM_LIMIT),
        name="mla",
    )(kmax, q, k, v, w1, w2, wo)


NA_QROWS = 4
NA_KROWS = 12
NA_TQ = NA_QROWS * GRID_W
NA_TK = NA_KROWS * GRID_W
NA_STEPS = GRID_ROWS // NA_QROWS
NA_BIAS_ROWS = 2 * NA_ROWS - 1
NA_BIAS_COLS = 2 * NA_COLS - 1
NA_STRIP = 32


def _na_bias_plan(variant):
    plan = {}
    for a in range(NA_QROWS):
        for b in range(NA_KROWS):
            if variant == "first":
                dr, ok = b - a + NA_ROWS - 1, b < NA_ROWS
            elif variant == "last":
                dr, ok = b - a - 1, b >= NA_KROWS - NA_ROWS
            else:
                dr, ok = b - a + NA_ROWS // 2 - 1, a <= b < a + NA_ROWS
            plan[(a, b)] = dr if ok else None
    return plan


def _na_kernel(rpb_ref, q_ref, k0_ref, k1_ref, k2_ref, v0_ref, v1_ref, v2_ref, o_ref, t_ref, bias_ref,
               s0_ref, s1_ref, p0_ref, p1_ref, l0_ref, l1_ref):
    g = pl.program_id(0)
    s_refs, p_refs, l_refs = (s0_ref, s1_ref), (p0_ref, p1_ref), (l0_ref, l1_ref)

    @pl.when(g == 0)
    def _():
        c = lax.broadcasted_iota(jnp.int32, (GRID_W, GRID_W), 0)
        kc = lax.broadcasted_iota(jnp.int32, (GRID_W, GRID_W), 1)
        start = jnp.clip(c - NA_COLS // 2, 0, GRID_W - NA_COLS)
        dc = jnp.where((kc >= start) & (kc < start + NA_COLS), kc - c + NA_COLS - 1, -1)

        def per_row(i, carry):
            tile = jnp.full((GRID_W, GRID_W), NEG_BIG, F32)
            for d in range(NA_BIAS_COLS):
                tile = jnp.where(dc == d, rpb_ref[i * NA_BIAS_COLS + d] * LOG2_E, tile)
            t_ref[i] = tile
            return carry

        lax.fori_loop(0, HEADS * NA_BIAS_ROWS, per_row, 0)

    def build(variant):
        plan = _na_bias_plan(variant)

        def per_head(h, carry):
            for (a, b), dr in plan.items():
                if dr is None:
                    tile = jnp.full((GRID_W, GRID_W), NEG_BIG, F32)
                else:
                    tile = t_ref[h * NA_BIAS_ROWS + dr]
                bias_ref[h, a * GRID_W:(a + 1) * GRID_W, b * GRID_W:(b + 1) * GRID_W] = tile
            return carry

        lax.fori_loop(0, HEADS, per_head, 0)

    pl.when(g == 0)(lambda: build("first"))
    pl.when(g == 1)(lambda: build("mid"))
    pl.when(g == NA_STEPS - 1)(lambda: build("last"))

    k_refs = (k0_ref, k1_ref, k2_ref)
    v_refs = (v0_ref, v1_ref, v2_ref)
    for h in range(HEADS):
        cols = slice(h * NA_HEAD_DIM, (h + 1) * NA_HEAD_DIM)
        s_ref, p_ref, l_ref = s_refs[h % 2], p_refs[h % 2], l_refs[h % 2]
        qh = q_ref[:, cols]
        for t in range(3):
            s_ref[:, t * NA_TQ:(t + 1) * NA_TQ] = _dot_nt(qh, k_refs[t][:, cols])
        for r in range(NA_TQ // NA_STRIP):
            rows = slice(r * NA_STRIP, (r + 1) * NA_STRIP)
            s = s_ref[rows, :] + bias_ref[h, rows, :]
            p = jnp.exp2(s - jnp.max(s, axis=-1, keepdims=True))
            l_ref[rows, :] = jnp.broadcast_to(jnp.sum(p, axis=-1, keepdims=True), (NA_STRIP, LANES))
            p_ref[rows, :] = p.astype(BF16)
        o = _dot(p_ref[:, 0:NA_TQ], v_refs[0][:, cols])
        for t in (1, 2):
            o = o + _dot(p_ref[:, t * NA_TQ:(t + 1) * NA_TQ], v_refs[t][:, cols])
        o_ref[:, cols] = o / l_ref[...]


def _na(rpb, nq, nk, nv):
    first_blk = lambda g: jnp.clip(g - 1, 0, NA_STEPS - 3)
    kv_specs = [pl.BlockSpec((NA_TQ, NA_WIDTH), functools.partial(lambda g, t: (first_blk(g) + t, 0), t=t))
                for t in range(3)]
    return pl.pallas_call(
        _na_kernel,
        grid=(NA_STEPS,),
        in_specs=[pl.BlockSpec(memory_space=pltpu.SMEM), pl.BlockSpec((NA_TQ, NA_WIDTH), lambda g: (g, 0))]
                 + kv_specs + kv_specs,
        out_specs=pl.BlockSpec((NA_TQ, NA_WIDTH), lambda g: (g, 0)),
        out_shape=jax.ShapeDtypeStruct((SEQ, NA_WIDTH), F32),
        scratch_shapes=[pltpu.VMEM((HEADS * NA_BIAS_ROWS, GRID_W, GRID_W), F32),
                        pltpu.VMEM((HEADS, NA_TQ, NA_TK), F32),
                        pltpu.VMEM((NA_TQ, NA_TK), F32), pltpu.VMEM((NA_TQ, NA_TK), F32),
                        pltpu.VMEM((NA_TQ, NA_TK), BF16), pltpu.VMEM((NA_TQ, NA_TK), BF16),
                        pltpu.VMEM((NA_TQ, LANES), F32), pltpu.VMEM((NA_TQ, LANES), F32)],
        compiler_params=pltpu.CompilerParams(dimension_semantics=("arbitrary",),
                                             vmem_limit_bytes=VMEM_LIMIT),
        name="na",
    )(rpb.astype(F32).reshape(-1), nq, nk, nk, nk, nv, nv, nv)


def _outproj_kernel(a_ref, n_ref, ga_ref, gn_ref, wb_ref, x_ref, gm_ref, o_ref, h_ref):
    an = _rms(a_ref[...], ga_ref[...]).astype(BF16)
    nn = _rms(n_ref[...], gn_ref[...]).astype(BF16)
    y = x_ref[...] + _dot(an, wb_ref[0:MLA_WIDTH, :]) + _dot(nn, wb_ref[MLA_WIDTH:, :])
    o_ref[...] = y
    h_ref[...] = _rms(y, gm_ref[...]).astype(BF16)


def _outproj(a, n, ga, gn, w, x, gm, tm=512):
    row = lambda w_: pl.BlockSpec((tm, w_), lambda i: (i, 0))
    return pl.pallas_call(
        _outproj_kernel,
        grid=(SEQ // tm,),
        in_specs=[row(MLA_WIDTH), row(NA_WIDTH), _const_spec((1, MLA_WIDTH)), _const_spec((1, NA_WIDTH)),
                  _const_spec((MLA_WIDTH + NA_WIDTH, D_MODEL)), row(D_MODEL), _const_spec((1, D_MODEL))],
        out_specs=[row(D_MODEL), row(D_MODEL)],
        out_shape=[jax.ShapeDtypeStruct((SEQ, D_MODEL), F32), jax.ShapeDtypeStruct((SEQ, D_MODEL), BF16)],
        compiler_params=pltpu.CompilerParams(dimension_semantics=("arbitrary",),
                                             vmem_limit_bytes=VMEM_LIMIT),
        name="outproj",
    )(a, n, ga, gn, w, x, gm)


def _mlp_kernel(x_ref, h_ref, w1_ref, w2_ref, gf_ref, o_ref, acc_ref):
    j = pl.program_id(1)
    last = pl.num_programs(1) - 1

    def chunk():
        hid = jnp.maximum(_dot(h_ref[...], w1_ref[...]), 0.0)
        return _dot((hid * hid).astype(BF16), w2_ref[...])

    @pl.when(j == 0)
    def _():
        acc_ref[...] = x_ref[...] + chunk()

    @pl.when((j > 0) & (j < last))
    def _():
        acc_ref[...] += chunk()

    @pl.when(j == last)
    def _():
        o_ref[...] = _rms(acc_ref[...] + chunk(), gf_ref[...])


def _mlp(x, h, w1, w2, gf, tm=512, tf=1024):
    assert D_FF // tf >= 2
    tile = lambda: pl.BlockSpec((tm, D_MODEL), lambda i, j: (i, 0))
    return pl.pallas_call(
        _mlp_kernel,
        grid=(SEQ // tm, D_FF // tf),
        in_specs=[tile(), tile(),
                  pl.BlockSpec((D_MODEL, tf), lambda i, j: (0, j)),
                  pl.BlockSpec((tf, D_MODEL), lambda i, j: (j, 0)),
                  _const_spec((1, D_MODEL))],
        out_specs=tile(),
        out_shape=jax.ShapeDtypeStruct((SEQ, D_MODEL), F32),
        scratch_shapes=[pltpu.VMEM((tm, D_MODEL), F32)],
        compiler_params=pltpu.CompilerParams(dimension_semantics=("arbitrary", "arbitrary"),
                                             vmem_limit_bytes=VMEM_LIMIT),
        name="mlp",
    )(x, h, w1, w2, gf)


def _rope_tables():
    half = MLA_ROPE // 2
    inv = ROPE_THETA ** (-jnp.arange(half, dtype=F32) / half)
    ang_a = (jnp.arange(SEQ // ROPE_SPLIT) * ROPE_SPLIT).astype(F32)[:, None] * inv[None, :]
    ang_b = jnp.arange(ROPE_SPLIT).astype(F32)[:, None] * inv[None, :]
    ca, sa = jnp.cos(ang_a)[:, None, :], jnp.sin(ang_a)[:, None, :]
    cb, sb = jnp.cos(ang_b)[None, :, :], jnp.sin(ang_b)[None, :, :]
    return (ca * cb - sa * sb).reshape(SEQ, half), (sa * cb + ca * sb).reshape(SEQ, half)


def kernel(x, attn_norm_g, w_in, q_norm_g, w_uq, kv_norm_g, w_ukv, na_rpb, mla_out_norm_g, na_out_norm_g,
           w_out, mlp_norm_g, w_ff1, w_ff2, final_norm_g):
    assert x.shape == (1, SEQ, D_MODEL)
    assert w_in.shape[0] == 1, "the final norm is fused into the (single) layer's MLP kernel"
    cos, sin = _rope_tables()
    xs = x[0]
    for l in range(1):
        wt = _win(jnp.swapaxes(w_in[l], 0, 1))
        wq = w_uq[l].reshape(MLA_Q_RANK, HEADS, MLA_QK)
        wuq = jnp.concatenate([wq[:, :, :MLA_NOPE].reshape(MLA_Q_RANK, HEADS * MLA_NOPE),
                               wq[:, :, MLA_NOPE:].reshape(MLA_Q_RANK, HEADS * MLA_ROPE)], axis=1).astype(BF16)
        wk = w_ukv[l].reshape(MLA_KV_RANK, HEADS, MLA_NOPE + MLA_V)
        wuk = wk[:, :, :MLA_NOPE].reshape(MLA_KV_RANK, HEADS * MLA_NOPE).astype(BF16)
        wvt = jnp.transpose(wk[:, :, MLA_NOPE:].reshape(MLA_KV_RANK, HEADS * MLA_V)).astype(BF16)

        q, k, vt, kn2, nq, nk, nv = _proj(xs, attn_norm_g[l][None], wt, q_norm_g[l][None], wuq,
                                    kv_norm_g[l][None], wuk, wvt, cos, sin)
        a_out, w1b, w2b, wob = _mla(jnp.max(kn2[:, :, 0], axis=0), q, k, vt, w_ff1[l], w_ff2[l], w_out[l])
        n_out = _na(na_rpb[l], nq, nk, nv)
        xs, hs = _outproj(a_out, n_out, mla_out_norm_g[l][None], na_out_norm_g[l][None],
                          wob, xs, mlp_norm_g[l][None])
        xs = _mlp(xs, hs, w1b, w2b, final_norm_g[None])
    return xs[None]
```
